```python
import math
import jax
import jax.numpy as jnp
from jax import lax
import numpy as np

D_MODEL = 2048
BATCH = 4
SEQ = 2048
DEPTH = 2
DEC_BATCH = 8
DEC_SEQ = 4
PAST_LEN = 16384
PAGE_SIZE = 128

SSM_WIDTH = D_MODEL // 2
ATTN_WIDTH = D_MODEL - SSM_WIDTH
SSM_CH = 16
SSM_GROUPS = SSM_WIDTH // SSM_CH
SSM_STATE = 64
DT_MIN = 1e-3
DT_MAX = 1e-1
HEAD_DIM = 128
N_HEADS = ATTN_WIDTH // HEAD_DIM
N_KV_HEADS = 2
GQA = N_HEADS // N_KV_HEADS
KV_WIDTH = N_KV_HEADS * HEAD_DIM
N_BRANCH = 3
CMP_LEN = 32
CMP_STRIDE = 16
CMP_HIDDEN = HEAD_DIM
SEL_LEN = 64
N_SEL = 16
WINDOW = 512
WIN_QBLK = 128
SEL_QBLK = 64
FORCED_SCORE = 1e4
D_FF = -(-8 * D_MODEL // (3 * 256)) * 256
NORM_EPS = 1e-6
IN_SPLITS = (SSM_WIDTH,) + tuple(SSM_WIDTH + ATTN_WIDTH + i * KV_WIDTH for i in range(7))
IN_WIDTH = SSM_WIDTH + ATTN_WIDTH + 6 * KV_WIDTH + N_HEADS * N_BRANCH

kernel_name = 'hymba_s5_nsa_decode_step'


def rmsnorm(x, g):
    xf = x.astype(jnp.float32)
    y = xf * lax.rsqrt(jnp.mean(xf * xf, axis=-1, keepdims=True) + NORM_EPS)
    return (y * g.astype(jnp.float32)).astype(x.dtype)


def alibi_slopes():
    h = jnp.arange(1, N_HEADS + 1, dtype=jnp.float32)
    return (2.0 ** (-8.0 * h / N_HEADS)).reshape(N_KV_HEADS, GQA)


def masked_softmax(s, valid):
    s = jnp.where(valid, s.astype(jnp.float32), -jnp.inf)
    m = jnp.max(s, axis=-1, keepdims=True)
    m = jnp.where(jnp.isfinite(m), m, 0.0)
    e = jnp.exp(s - m)
    den = jnp.sum(e, axis=-1, keepdims=True)
    return e / jnp.maximum(den, jnp.finfo(jnp.float32).tiny)


def dense_gqa(q, k, v, dist, valid):
    s = jnp.einsum('...tkgd,...skd->...kgts', q, k).astype(jnp.float32) * HEAD_DIM ** -0.5
    s = s - alibi_slopes()[:, :, None, None] * dist[..., None, None, :, :]
    p = masked_softmax(s, valid[..., None, None, :, :])
    o = jnp.einsum('...kgts,...skd->...tkgd', p.astype(v.dtype), v)
    return o, p


def _linear_combine(e1, e2):
    a1, b1 = e1
    a2, b2 = e2
    return a2 * a1, a2 * b1 + b2


def s5_mixer(u, h0, p):
    bsz, t_len, _ = u.shape
    uf = u.astype(jnp.float32).reshape(bsz, t_len, SSM_GROUPS, SSM_CH)
    lam = lax.complex(p['ssm_a_re'].astype(jnp.float32), p['ssm_a_im'].astype(jnp.float32))
    dt = jnp.exp(p['ssm_log_dt'].astype(jnp.float32))[:, None]
    lam_bar = jnp.exp(lam * dt)
    b = lax.complex(p['ssm_b_re'].astype(jnp.float32), p['ssm_b_im'].astype(jnp.float32))
    b_bar = ((lam_bar - 1.0) / lam)[..., None] * b
    bu = lax.complex(jnp.einsum('btgp,gnp->btgn', uf, b_bar.real),
                     jnp.einsum('btgp,gnp->btgn', uf, b_bar.imag))
    a = jnp.broadcast_to(lam_bar, bu.shape)
    a_cum, h = lax.associative_scan(_linear_combine, (a, bu), axis=1)
    h = h + a_cum * lax.complex(h0[..., 0].astype(jnp.float32), h0[..., 1].astype(jnp.float32))[:, None]
    y = (jnp.einsum('gpn,btgn->btgp', p['ssm_c_re'].astype(jnp.float32), h.real)
         - jnp.einsum('gpn,btgn->btgp', p['ssm_c_im'].astype(jnp.float32), h.imag))
    y = (y + p['ssm_d'].astype(jnp.float32) * uf).reshape(bsz, t_len, SSM_WIDTH)
    z = jax.nn.gelu(y)
    z = z * jax.nn.sigmoid(z @ p['ssm_w_glu'].astype(jnp.float32))
    h_last = jnp.stack([h[:, -1].real, h[:, -1].imag], axis=-1)
    return z.astype(u.dtype), h_last.astype(h0.dtype)


def project(xn, p):
    bsz, t_len, _ = xn.shape
    u, q, kc, vc, ks, vs, kw, vw, g = jnp.split(xn @ p['w_in'], IN_SPLITS, axis=-1)
    q = rmsnorm(q.reshape(bsz, t_len, N_KV_HEADS, GQA, HEAD_DIM), p['q_norm'])

    def heads(t):
        return t.reshape(bsz, t_len, N_KV_HEADS, HEAD_DIM)

    kv_cmp = jnp.stack([heads(kc), heads(vc)], axis=3)
    kv_slc = jnp.stack([rmsnorm(heads(ks), p['k_norm_slc']), heads(vs)], axis=3)
    kv_win = jnp.stack([rmsnorm(heads(kw), p['k_norm_win']), heads(vw)], axis=3)
    gates = jax.nn.sigmoid(g.astype(jnp.float32)).reshape(bsz, t_len, N_KV_HEADS, GQA, N_BRANCH)
    return u, q, kv_cmp, kv_slc, kv_win, gates


def compress(rows, w1, b1, w2):
    bsz, length = rows.shape[:2]
    n_chunk = length // CMP_STRIDE
    c = rows[:, :n_chunk * CMP_STRIDE].reshape(bsz, n_chunk, CMP_STRIDE, N_KV_HEADS, HEAD_DIM)
    c = c.transpose(0, 1, 3, 2, 4).reshape(bsz, n_chunk, N_KV_HEADS, CMP_STRIDE * HEAD_DIM)
    half = CMP_STRIDE * HEAD_DIM
    h = c[:, :-1] @ w1[:half] + c[:, 1:] @ w1[half:] + b1
    return jax.nn.gelu(h) @ w2


def compressed_kv(kv_rows, p):
    ck = rmsnorm(compress(kv_rows[..., 0, :], p['cmp_w1_k'], p['cmp_b1_k'], p['cmp_w2_k']), p['k_norm_cmp'])
    cv = compress(kv_rows[..., 1, :], p['cmp_w1_v'], p['cmp_b1_v'], p['cmp_w2_v'])
    return ck, cv


def cmp_attend(q, ck, cv, q_pos):
    nc = ck.shape[1]
    start = jnp.arange(nc) * CMP_STRIDE
    valid = (start + CMP_LEN - 1)[None, :] <= q_pos[:, None]
    dist = (q_pos[:, None] - start[None, :]).astype(jnp.float32) - (CMP_LEN - 1) / 2
    o, prob = dense_gqa(q, ck, cv, dist, valid)
    return o, prob.sum(axis=2)


def select_blocks(p_cmp, q_pos, n_blocks):
    nc = p_cmp.shape[-1]
    per = SEL_LEN // CMP_STRIDE
    ci = jnp.arange(nc)
    j = jnp.arange(n_blocks)
    overlap = ((ci // per)[None, :] == j[:, None]).astype(jnp.float32) + \
        (((ci + 1) // per)[None, :] == j[:, None]).astype(jnp.float32)
    score = jnp.einsum('bktc,jc->bktj', p_cmp, overlap)
    qb = q_pos // SEL_LEN
    forced = (j[None, :] == 0) | (j[None, :] == qb[:, None]) | (j[None, :] == qb[:, None] - 1)
    causal = j[None, :] * SEL_LEN <= q_pos[:, None]
    score = jnp.where(forced, FORCED_SCORE, jnp.where(causal, score, -1.0))
    _, idx = lax.top_k(score, min(N_SEL, n_blocks))
    return idx


def sel_attend(q, g, idx, q_pos):
    bsz, _, t_len, nsel = idx.shape
    nk = nsel * SEL_LEN
    g = g.reshape(bsz, N_KV_HEADS, t_len, nk, 2, HEAD_DIM)
    kpos = (idx[..., None] * SEL_LEN + jnp.arange(SEL_LEN)).reshape(bsz, N_KV_HEADS, t_len, nk)
    dist = q_pos[:, None] - kpos
    s = jnp.einsum('btkgd,bktsd->bkgts', q, g[..., 0, :]).astype(jnp.float32) * HEAD_DIM ** -0.5
    s = s - alibi_slopes()[None, :, :, None, None] * dist.astype(jnp.float32)[:, :, None]
    p = masked_softmax(s, (dist >= 0)[:, :, None])
    return jnp.einsum('bkgts,bktsd->btkgd', p.astype(g.dtype), g[..., 1, :])


def sel_prompt(q, kv_slc, idx):
    bsz, s_len = q.shape[:2]
    nq = s_len // SEL_QBLK
    kvb = kv_slc.reshape(bsz, s_len // SEL_LEN, SEL_LEN, N_KV_HEADS, 2, HEAD_DIM).transpose(0, 3, 1, 2, 4, 5)
    bi = jnp.arange(bsz)[:, None, None, None]
    ki = jnp.arange(N_KV_HEADS)[None, :, None, None]
    qc = q.reshape(bsz, nq, SEL_QBLK, N_KV_HEADS, GQA, HEAD_DIM).swapaxes(0, 1)
    ic = idx.reshape(bsz, N_KV_HEADS, nq, SEL_QBLK, -1).transpose(2, 0, 1, 3, 4)
    pc = jnp.arange(s_len).reshape(nq, SEL_QBLK)

    def block(args):
        qb, ib, pb = args
        return sel_attend(qb, kvb[bi, ki, ib], ib, pb)

    out = lax.map(block, (qc, ic, pc))
    return out.swapaxes(0, 1).reshape(bsz, s_len, N_KV_HEADS, GQA, HEAD_DIM)


def gather_selected_sample(cache_slc_kv, layer, page_table, new_kv, idx):
    bsz, _, t_len, _ = idx.shape
    past_blocks = PAST_LEN // SEL_LEN
    per_page = PAGE_SIZE // SEL_LEN
    bi = jnp.arange(bsz)[:, None, None, None]
    ki = jnp.arange(N_KV_HEADS)[None, :, None, None]
    jp = jnp.minimum(idx, past_blocks - 1)
    phys = page_table[bi, jp // per_page]
    rows = (jp % per_page)[..., None] * SEL_LEN + jnp.arange(SEL_LEN)
    g_past = cache_slc_kv[layer, phys[..., None], rows, ki[..., None]]
    nb_new = -(-t_len // SEL_LEN)
    newb = jnp.pad(new_kv, ((0, 0), (0, nb_new * SEL_LEN - t_len), (0, 0), (0, 0), (0, 0)))
    newb = newb.reshape(bsz, nb_new, SEL_LEN, N_KV_HEADS, 2, HEAD_DIM)
    jn = jnp.clip(idx - past_blocks, 0, nb_new - 1)
    g_new = newb[bi[..., None], jn[..., None], jnp.arange(SEL_LEN), ki[..., None]]
    return jnp.where((idx < past_blocks)[..., None, None, None], g_past, g_new.astype(g_past.dtype))


def win_prompt(q, kv_win):
    bsz, s_len = q.shape[:2]
    nb = s_len // WIN_QBLK
    wb = WINDOW // WIN_QBLK
    kvp = jnp.pad(kv_win, ((0, 0), (WINDOW, 0), (0, 0), (0, 0), (0, 0)))
    kvp = kvp.reshape(bsz, nb + wb, WIN_QBLK, N_KV_HEADS, 2, HEAD_DIM)
    band = jnp.concatenate([kvp[:, i:i + nb] for i in range(wb + 1)], axis=2)
    qpos = jnp.arange(s_len).reshape(nb, WIN_QBLK)
    kpos = (jnp.arange(nb) * WIN_QBLK - WINDOW)[:, None] + jnp.arange((wb + 1) * WIN_QBLK)
    d = qpos[:, :, None] - kpos[:, None, :]
    valid = (d >= 0) & (d < WINDOW) & (kpos[:, None, :] >= 0)
    qb = q.reshape(bsz, nb, WIN_QBLK, N_KV_HEADS, GQA, HEAD_DIM)
    o, _ = dense_gqa(qb, band[..., 0, :], band[..., 1, :], d.astype(jnp.float32), valid)
    return o.reshape(bsz, s_len, N_KV_HEADS, GQA, HEAD_DIM)


def mix_output(ssm_y, gates, o_cmp, o_slc, o_win, p):
    bsz, t_len = ssm_y.shape[:2]
    attn = gates[..., 0:1] * o_cmp + gates[..., 1:2] * o_slc + gates[..., 2:3] * o_win
    attn = attn.reshape(bsz, t_len, ATTN_WIDTH).astype(ssm_y.dtype)
    h = jnp.concatenate([rmsnorm(ssm_y, p['norm_ssm_out']), rmsnorm(attn, p['norm_attn_out'])], axis=-1)
    return h @ p['w_out']


def swiglu_ffn(x, p):
    h = rmsnorm(x, p['norm_ffn'])
    return (jax.nn.silu(h @ p['w_ffn_gate']) * (h @ p['w_ffn_up'])) @ p['w_ffn_down']


def prompt_mixer(x, p):
    bsz, s_len, _ = x.shape
    u, q, kv_cmp, kv_slc, kv_win, gates = project(rmsnorm(x, p['norm_mix']), p)
    h0 = jnp.zeros((bsz, SSM_GROUPS, SSM_STATE, 2), jnp.float32)
    ssm_y, h_last = s5_mixer(u, h0, p)
    pos = jnp.arange(s_len)
    ck, cv = compressed_kv(kv_cmp, p)
    o_cmp, p_cmp = cmp_attend(q, ck, cv, pos)
    idx = select_blocks(p_cmp, pos, s_len // SEL_LEN)
    o_slc = sel_prompt(q, kv_slc, idx)
    o_win = win_prompt(q, kv_win)
    y = mix_output(ssm_y, gates, o_cmp, o_slc, o_win, p)
    wbuf = min(WINDOW, PAST_LEN)
    win_state = jnp.pad(kv_win, ((0, 0), (max(wbuf - s_len, 0), 0), (0, 0), (0, 0), (0, 0)))[:, -wbuf:]
    return y, kv_cmp, kv_slc, win_state, h_last


def sample_mixer(x, p, cache_cmp_kv, cache_slc_kv, layer, page_table, win_buf, h0):
    bsz, t_len, _ = x.shape
    u, q, kv_cmp, kv_slc, kv_win, gates = project(rmsnorm(x, p['norm_mix']), p)
    ssm_y, h_last = s5_mixer(u, h0, p)
    pos = PAST_LEN + jnp.arange(t_len)
    past_cmp = cache_cmp_kv[layer, page_table].reshape(bsz, PAST_LEN, N_KV_HEADS, 2, HEAD_DIM)
    ck, cv = compressed_kv(jnp.concatenate([past_cmp, kv_cmp.astype(past_cmp.dtype)], axis=1), p)
    o_cmp, p_cmp = cmp_attend(q, ck, cv, pos)
    idx = select_blocks(p_cmp, pos, -(-(PAST_LEN + t_len) // SEL_LEN))
    o_slc = sel_attend(q, gather_selected_sample(cache_slc_kv, layer, page_table, kv_slc, idx), idx, pos)
    wbuf = win_buf.shape[1]
    kv_all = jnp.concatenate([win_buf, kv_win.astype(win_buf.dtype)], axis=1)
    kpos = PAST_LEN - wbuf + jnp.arange(wbuf + t_len)
    d = pos[:, None] - kpos[None, :]
    o_win, _ = dense_gqa(q, kv_all[..., 0, :], kv_all[..., 1, :], d.astype(jnp.float32), (d >= 0) & (d < WINDOW))
    y = mix_output(ssm_y, gates, o_cmp, o_slc, o_win, p)
    return y, kv_cmp, kv_slc, kv_all[:, -wbuf:], h_last


def setup_inputs(seed: int = 0) -> dict:
    key = jax.random.key(seed)
    ks = iter(jax.random.split(key, 48))

    def nrm(shape, scale=1.0):
        return jax.random.normal(next(ks), shape, jnp.float32) * scale

    def gain(n):
        return 1.0 + 0.02 * nrm((DEPTH, n))

    n_pages = PAST_LEN // PAGE_SIZE
    n_used = DEC_BATCH * n_pages
    n_pool = n_used + n_used // 4
    wbuf = min(WINDOW, PAST_LEN)
    n_idx = jnp.arange(SSM_STATE, dtype=jnp.float32)
    cache_shape = (DEPTH, n_pool, PAGE_SIZE, N_KV_HEADS, 2, HEAD_DIM)
    return {
        'x_prompt': nrm((BATCH, SEQ, D_MODEL)),
        'x_sample': nrm((DEC_BATCH, DEC_SEQ, D_MODEL)),
        'cache_cmp_kv': nrm(cache_shape),
        'cache_slc_kv': nrm(cache_shape),
        'state_win_kv': nrm((DEPTH, DEC_BATCH, wbuf, N_KV_HEADS, 2, HEAD_DIM)),
        'state_ssm': nrm((DEPTH, DEC_BATCH, SSM_GROUPS, SSM_STATE, 2), 0.1),
        'page_table': jax.random.permutation(next(ks), n_pool)[:n_used].reshape(DEC_BATCH, n_pages).astype(jnp.int32),
        'norm_mix': gain(D_MODEL),
        'w_in': nrm((DEPTH, D_MODEL, IN_WIDTH), D_MODEL ** -0.5),
        'ssm_a_re': -0.5 + 0.01 * nrm((DEPTH, SSM_GROUPS, SSM_STATE)),
        'ssm_a_im': math.pi * n_idx + 0.01 * nrm((DEPTH, SSM_GROUPS, SSM_STATE)),
        'ssm_log_dt': jax.random.uniform(next(ks), (DEPTH, SSM_GROUPS), jnp.float32, math.log(DT_MIN), math.log(DT_MAX)),
        'ssm_b_re': nrm((DEPTH, SSM_GROUPS, SSM_STATE, SSM_CH), (2 * SSM_CH) ** -0.5),
        'ssm_b_im': nrm((DEPTH, SSM_GROUPS, SSM_STATE, SSM_CH), (2 * SSM_CH) ** -0.5),
        'ssm_c_re': nrm((DEPTH, SSM_GROUPS, SSM_CH, SSM_STATE), SSM_STATE ** -0.5),
        'ssm_c_im': nrm((DEPTH, SSM_GROUPS, SSM_CH, SSM_STATE), SSM_STATE ** -0.5),
        'ssm_d': nrm((DEPTH, SSM_GROUPS, SSM_CH)),
        'ssm_w_glu': nrm((DEPTH, SSM_WIDTH, SSM_WIDTH), SSM_WIDTH ** -0.5),
        'q_norm': gain(HEAD_DIM),
        'k_norm_cmp': gain(HEAD_DIM),
        'k_norm_slc': gain(HEAD_DIM),
        'k_norm_win': gain(HEAD_DIM),
        'cmp_w1_k': nrm((DEPTH, CMP_LEN * HEAD_DIM, CMP_HIDDEN), (CMP_LEN * HEAD_DIM) ** -0.5),
        'cmp_b1_k': nrm((DEPTH, CMP_HIDDEN), 0.02),
        'cmp_w2_k': nrm((DEPTH, CMP_HIDDEN, HEAD_DIM), CMP_HIDDEN ** -0.5),
        'cmp_w1_v': nrm((DEPTH, CMP_LEN * HEAD_DIM, CMP_HIDDEN), (CMP_LEN * HEAD_DIM) ** -0.5),
        'cmp_b1_v': nrm((DEPTH, CMP_HIDDEN), 0.02),
        'cmp_w2_v': nrm((DEPTH, CMP_HIDDEN, HEAD_DIM), CMP_HIDDEN ** -0.5),
        'norm_ssm_out': gain(SSM_WIDTH),
        'norm_attn_out': gain(ATTN_WIDTH),
        'w_out': nrm((DEPTH, D_MODEL, D_MODEL), D_MODEL ** -0.5),
        'norm_ffn': gain(D_MODEL),
        'w_ffn_gate': nrm((DEPTH, D_MODEL, D_FF), D_MODEL ** -0.5),
        'w_ffn_up': nrm((DEPTH, D_MODEL, D_FF), D_MODEL ** -0.5),
        'w_ffn_down': nrm((DEPTH, D_FF, D_MODEL), D_FF ** -0.5),
    }


def reference(x_prompt, x_sample, cache_cmp_kv, cache_slc_kv, state_win_kv, state_ssm, page_table,
              norm_mix, w_in, ssm_a_re, ssm_a_im, ssm_log_dt, ssm_b_re, ssm_b_im, ssm_c_re, ssm_c_im,
              ssm_d, ssm_w_glu, q_norm, k_norm_cmp, k_norm_slc, k_norm_win,
              cmp_w1_k, cmp_b1_k, cmp_w2_k, cmp_w1_v, cmp_b1_v, cmp_w2_v,
              norm_ssm_out, norm_attn_out, w_out, norm_ffn, w_ffn_gate, w_ffn_up, w_ffn_down):
    y_prompt, y_sample = x_prompt, x_sample
    cmp_p, slc_p, win_p, ssm_p = [], [], [], []
    cmp_s, slc_s, win_s, ssm_s = [], [], [], []
    for l in range(DEPTH):
        p = {
            'norm_mix': norm_mix[l], 'w_in': w_in[l],
            'ssm_a_re': ssm_a_re[l], 'ssm_a_im': ssm_a_im[l], 'ssm_log_dt': ssm_log_dt[l],
            'ssm_b_re': ssm_b_re[l], 'ssm_b_im': ssm_b_im[l], 'ssm_c_re': ssm_c_re[l], 'ssm_c_im': ssm_c_im[l],
            'ssm_d': ssm_d[l], 'ssm_w_glu': ssm_w_glu[l],
            'q_norm': q_norm[l], 'k_norm_cmp': k_norm_cmp[l], 'k_norm_slc': k_norm_slc[l], 'k_norm_win': k_norm_win[l],
            'cmp_w1_k': cmp_w1_k[l], 'cmp_b1_k': cmp_b1_k[l], 'cmp_w2_k': cmp_w2_k[l],
            'cmp_w1_v': cmp_w1_v[l], 'cmp_b1_v': cmp_b1_v[l], 'cmp_w2_v': cmp_w2_v[l],
            'norm_ssm_out': norm_ssm_out[l], 'norm_attn_out': norm_attn_out[l], 'w_out': w_out[l],
            'norm_ffn': norm_ffn[l], 'w_ffn_gate': w_ffn_gate[l], 'w_ffn_up': w_ffn_up[l], 'w_ffn_down': w_ffn_down[l],
        }
        m, kc, ksl, kw, hs = prompt_mixer(y_prompt, p)
        y_prompt = y_prompt + m
        y_prompt = y_prompt + swiglu_ffn(y_prompt, p)
        cmp_p.append(kc)
        slc_p.append(ksl)
        win_p.append(kw)
        ssm_p.append(hs)
        m, kc, ksl, kw, hs = sample_mixer(y_sample, p, cache_cmp_kv, cache_slc_kv, l, page_table,
                                          state_win_kv[l], state_ssm[l])
        y_sample = y_sample + m
        y_sample = y_sample + swiglu_ffn(y_sample, p)
        cmp_s.append(kc)
        slc_s.append(ksl)
        win_s.append(kw)
        ssm_s.append(hs)
    new_cmp_kv_prompt = jnp.stack(cmp_p)
    new_slc_kv_prompt = jnp.stack(slc_p)
    new_win_kv_prompt = jnp.stack(win_p)
    new_ssm_prompt = jnp.stack(ssm_p)
    new_cmp_kv_sample = jnp.stack(cmp_s)
    new_slc_kv_sample = jnp.stack(slc_s)
    new_win_kv_sample = jnp.stack(win_s)
    new_ssm_sample = jnp.stack(ssm_s)
    return (y_prompt, y_sample, new_cmp_kv_prompt, new_slc_kv_prompt, new_win_kv_prompt, new_ssm_prompt,
            new_cmp_kv_sample, new_slc_kv_sample, new_win_kv_sample, new_ssm_sample)
```

```python
import functools
import math

import jax
import jax.numpy as jnp
from jax import lax
from jax.experimental import pallas as pl
from jax.experimental.pallas import tpu as pltpu

F32 = jnp.float32
BF16 = jnp.bfloat16

D_MODEL = 2048
DEPTH = 2
PAST_LEN = 16384
PAGE_SIZE = 128
SSM_WIDTH = 1024
ATTN_WIDTH = 1024
SSM_CH = 16
SSM_GROUPS = 64
SSM_STATE = 64
HEAD_DIM = 128
N_HEADS = 8
N_KV_HEADS = 2
GQA = 4
KV_WIDTH = 256
CMP_LEN = 32
CMP_STRIDE = 16
SEL_LEN = 64
SEL_SHIFT = SEL_LEN.bit_length() - 1
N_SEL = 16
WINDOW = 512
WIN_QBLK = 128
FORCED_SCORE = 1e4
D_FF = 5632
NORM_EPS = 1e-6
SCALE = HEAD_DIM ** -0.5
GATE_PAD = 128
IN_WIDTH_PAD = SSM_WIDTH + ATTN_WIDTH + 6 * KV_WIDTH + GATE_PAD
F32_TINY = float(jnp.finfo(jnp.float32).tiny)
NEG_INF = float("-inf")

V7X_VMEM_BYTES = 64 * 1024 * 1024
V7X_LANES = 128


def _cparams(semantics, vmem_mib):
    assert vmem_mib * 1024 * 1024 < V7X_VMEM_BYTES
    return pltpu.CompilerParams(dimension_semantics=semantics, vmem_limit_bytes=vmem_mib * 1024 * 1024)


def _resident(shape):
    nd = len(shape)
    return pl.BlockSpec(shape, lambda *_: (0,) * nd, pipeline_mode=pl.Buffered(1))


def _rms(x, g):
    return x * lax.rsqrt(jnp.mean(x * x, axis=-1, keepdims=True) + NORM_EPS) * g


def _gelu(x):
    c = math.sqrt(2.0 / math.pi)
    return x * (0.5 * (1.0 + jnp.tanh(c * (x + 0.044715 * (x * x * x)))))


def _bdot(a, b):
    return jnp.dot(a, b, preferred_element_type=F32)


def _bdot_nt(a, b):
    return lax.dot_general(a, b, (((1,), (1,)), ((), ())), preferred_element_type=F32)


def _split3(x):
    hi = x.astype(BF16)
    r1 = x - hi.astype(F32)
    mid = r1.astype(BF16)
    lo = (r1 - mid.astype(F32)).astype(BF16)
    return hi, mid, lo


def _dot_small_int(x, e):
    hi, mid, lo = _split3(x)
    return _bdot(hi, e) + _bdot(mid, e) + _bdot(lo, e)


def _dot_f32(a, b):
    ah, am, al = _split3(a)
    bh, bm, bl = _split3(b)
    return (_bdot(ah, bh) + (_bdot(ah, bm) + _bdot(am, bh))
            + (_bdot(ah, bl) + _bdot(al, bh) + _bdot(am, bm)))


def _softmax_parts(s, valid):
    s = jnp.where(valid, s, NEG_INF)
    m = jnp.max(s, axis=-1, keepdims=True)
    m = jnp.where(m == NEG_INF, 0.0, m)
    e = jnp.exp(s - m)
    return m, e, jnp.sum(e, axis=-1, keepdims=True)


def _online_update(m_ref, l_ref, acc_ref, g, s, valid, v):
    s = jnp.where(valid, s, NEG_INF)
    m_old = m_ref[g]
    m_new = jnp.maximum(m_old, jnp.max(s, axis=-1, keepdims=True))
    m_safe = jnp.where(m_new == NEG_INF, 0.0, m_new)
    alpha = jnp.exp(m_old - m_safe)
    p = jnp.exp(s - m_safe)
    l_ref[g] = alpha * l_ref[g] + jnp.sum(p, axis=-1, keepdims=True)
    acc_ref[g] = alpha * acc_ref[g] + _bdot(p.astype(BF16), v)
    m_ref[g] = m_new


def _head_slope(kvh, g):
    return jnp.where(kvh == 0, 2.0 ** -(g + 1), 2.0 ** -(g + 1 + GQA))


def _inproj_kernel(x_ref, gmix_ref, w_ref, qn_ref, ksn_ref, kwn_ref,
                   u_ref, q_ref, cmp_ref, slc_ref, win_ref, gate_ref):
    xn = _rms(x_ref[...], gmix_ref[...]).astype(BF16)

    def proj(c0, c1):
        return _bdot(xn, w_ref[:, c0:c1])

    u_ref[...] = proj(0, SSM_WIDTH)
    q = proj(SSM_WIDTH, D_MODEL)
    for h in range(N_HEADS):
        sl = slice(h * HEAD_DIM, (h + 1) * HEAD_DIM)
        q_ref[:, sl] = _rms(q[:, sl], qn_ref[...]).astype(BF16)
    c0 = D_MODEL
    cmp_ref[...] = proj(c0, c0 + 2 * KV_WIDTH)
    for ref, nref in ((slc_ref, ksn_ref), (win_ref, kwn_ref)):
        c0 += 2 * KV_WIDTH
        kv = proj(c0, c0 + 2 * KV_WIDTH)
        for kvh in range(N_KV_HEADS):
            ks = slice(kvh * 2 * HEAD_DIM, kvh * 2 * HEAD_DIM + HEAD_DIM)
            vs = slice(kvh * 2 * HEAD_DIM + HEAD_DIM, (kvh + 1) * 2 * HEAD_DIM)
            ref[:, ks] = _rms(kv[:, ks], nref[...])
            ref[:, vs] = kv[:, vs]
    c0 += 2 * KV_WIDTH
    gate_ref[...] = jax.nn.sigmoid(proj(c0, c0 + GATE_PAD))


def _inproj(x, gmix, w, qn, ksn, kwn, tm):
    m = x.shape[0]
    row = lambda n: pl.BlockSpec((tm, n), lambda i: (i, 0))
    kvw = 2 * KV_WIDTH
    return pl.pallas_call(
        _inproj_kernel,
        grid=(m // tm,),
        in_specs=[row(D_MODEL), _resident((1, D_MODEL)), _resident((D_MODEL, IN_WIDTH_PAD)),
                  _resident((1, HEAD_DIM)), _resident((1, HEAD_DIM)), _resident((1, HEAD_DIM))],
        out_specs=[row(SSM_WIDTH), row(ATTN_WIDTH), row(kvw), row(kvw), row(kvw), row(GATE_PAD)],
        out_shape=[jax.ShapeDtypeStruct((m, SSM_WIDTH), F32), jax.ShapeDtypeStruct((m, ATTN_WIDTH), BF16),
                   jax.ShapeDtypeStruct((m, kvw), F32), jax.ShapeDtypeStruct((m, kvw), F32),
                   jax.ShapeDtypeStruct((m, kvw), F32), jax.ShapeDtypeStruct((m, GATE_PAD), F32)],
        compiler_params=_cparams(("parallel",), 48),
        name="inproj",
    )(x, gmix, w, qn, ksn, kwn)


def _s5_kernel(u_ref, h0_ref, lrow_ref, lcol_ref, bt_ref, bn_ref, ct_ref, d_ref,
               y_ref, hl_ref, tm_ref, x_ref, hin_ref, *, lp, lreal, nb, nc):
    k = lp * SSM_CH
    n = SSM_STATE

    def lam_bar(a_re, a_im, log_dt):
        dt = jnp.exp(log_dt)
        e = jnp.exp(a_re * dt)
        return e * jnp.cos(a_im * dt), e * jnp.sin(a_im * dt)

    def zoh_coef(l_re, l_im, a_re, a_im):
        den = a_re * a_re + a_im * a_im
        x_re = l_re - 1.0
        return (x_re * a_re + l_im * a_im) / den, (l_im * a_re - x_re * a_im) / den

    lc = lcol_ref[0]
    a_re_c, a_im_c = lc[:, 0:1], lc[:, 1:2]
    l_re_c, l_im_c = lam_bar(a_re_c, a_im_c, lc[:, 2:3])
    lr = lrow_ref[0]
    a_re_r, a_im_r = lr[0:1], lr[1:2]
    l_re_r, l_im_r = lam_bar(a_re_r, a_im_r, lr[2:3])

    def pow_table(e):
        p_re = jnp.ones((n, k), F32)
        p_im = jnp.zeros((n, k), F32)
        s_re, s_im = l_re_c, l_im_c
        for j in range(max(lp.bit_length() - 1, 1)):
            bit = ((e >> j) & 1) == 1
            m_re = jnp.where(bit, s_re, 1.0)
            m_im = jnp.where(bit, s_im, 0.0)
            p_re, p_im = p_re * m_re - p_im * m_im, p_re * m_im + p_im * m_re
            s_re, s_im = s_re * s_re - s_im * s_im, 2.0 * s_re * s_im
        return p_re, p_im

    lane = lax.broadcasted_iota(jnp.int32, (1, k), 1)
    tau = lane >> 4
    ch_onehot = jnp.where((lax.broadcasted_iota(jnp.int32, (SSM_CH, k), 1) & (SSM_CH - 1))
                          == lax.broadcasted_iota(jnp.int32, (SSM_CH, k), 0), 1.0, 0.0).astype(BF16)

    p_re, p_im = pow_table(tau)
    c_re = _dot_small_int(ct_ref[0, 0], ch_onehot)
    c_im = _dot_small_int(ct_ref[0, 1], ch_onehot)
    g_re = c_re * p_re - c_im * p_im
    g_im = c_re * p_im + c_im * p_re

    cf_re_r, cf_im_r = zoh_coef(l_re_r, l_im_r, a_re_r, a_im_r)
    bbt_re = cf_re_r * bt_ref[0, 0] - cf_im_r * bt_ref[0, 1]
    bbt_im = cf_re_r * bt_ref[0, 1] + cf_im_r * bt_ref[0, 0]
    a = _dot_f32(bbt_re, g_re) - _dot_f32(bbt_im, g_im)

    lane16 = lax.broadcasted_iota(jnp.int32, (SSM_CH, k), 1)
    for s in range(lp):
        blk = a if s == 0 else jnp.where(lane16 >= SSM_CH * s, pltpu.roll(a, SSM_CH * s, axis=1), 0.0)
        tm_ref[SSM_CH * s:SSM_CH * (s + 1), :] = blk.astype(BF16)

    cf_re_c, cf_im_c = zoh_coef(l_re_c, l_im_c, a_re_c, a_im_c)
    bbn_re = cf_re_c * bn_ref[0, 0] - cf_im_c * bn_ref[0, 1]
    bbn_im = cf_re_c * bn_ref[0, 1] + cf_im_c * bn_ref[0, 0]
    bb_re = _dot_small_int(bbn_re, ch_onehot)
    bb_im = _dot_small_int(bbn_im, ch_onehot)
    rev = (lreal - 1) - tau
    q_re, q_im = pow_table(jnp.maximum(rev, 0))
    live = rev >= 0
    wx = jnp.concatenate([jnp.where(live, q_re * bb_re - q_im * bb_im, 0.0),
                          jnp.where(live, q_re * bb_im + q_im * bb_re, 0.0)], axis=0).astype(BF16)

    u = u_ref[0]
    ub = u.astype(BF16)
    x_ref[...] = _bdot_nt(ub, wx)

    ll_re, ll_im = l_re_r, l_im_r
    for _ in range(lreal.bit_length() - 1):
        ll_re, ll_im = ll_re * ll_re - ll_im * ll_im, 2.0 * ll_re * ll_im
    h = h0_ref[0]
    for c in range(nc):
        hin_ref[c * nb:(c + 1) * nb, :] = h
        h_re, h_im = h[:, :n], h[:, n:]
        h = jnp.concatenate([ll_re * h_re - ll_im * h_im, ll_re * h_im + ll_im * h_re], axis=1) \
            + x_ref[c * nb:(c + 1) * nb, :]
    hl_ref[0] = h

    g1_re = g_re * l_re_c - g_im * l_im_c
    g1_im = g_re * l_im_c + g_im * l_re_c
    gs = jnp.concatenate([g1_re, -g1_im], axis=0).astype(BF16)

    y_ref[0] = _bdot(ub, tm_ref[...]) + _bdot(hin_ref[...].astype(BF16), gs) + u * d_ref[0]


def _s5(u_g, h0_g, prm, lp, lreal, nb, nc):
    g, m, k = u_g.shape
    blk = lambda *s: pl.BlockSpec((1,) + s, lambda i: (i,) + (0,) * len(s))
    return pl.pallas_call(
        functools.partial(_s5_kernel, lp=lp, lreal=lreal, nb=nb, nc=nc),
        grid=(g,),
        in_specs=[blk(m, k), blk(nb, 2 * SSM_STATE), blk(3, SSM_STATE), blk(SSM_STATE, 3),
                  blk(2, SSM_CH, SSM_STATE), blk(2, SSM_STATE, SSM_CH), blk(2, SSM_STATE, SSM_CH), blk(1, k)],
        out_specs=[blk(m, k), blk(nb, 2 * SSM_STATE)],
        out_shape=[jax.ShapeDtypeStruct((g, m, k), F32), jax.ShapeDtypeStruct((g, nb, 2 * SSM_STATE), F32)],
        scratch_shapes=[pltpu.VMEM((k, k), BF16), pltpu.VMEM((m, 2 * SSM_STATE), F32),
                        pltpu.VMEM((m, 2 * SSM_STATE), F32)],
        compiler_params=_cparams(("parallel",), 32),
        name="s5_group",
    )(u_g, h0_g, prm["lrow"], prm["lcol"], prm["bt"], prm["bn"], prm["ct"], prm["dtile"][lp])


def _s5_mixer(u, h0, prm, lp, lreal):
    b, t, _ = u.shape
    nc = t // lreal
    ug = u.reshape(b, nc, lreal, SSM_GROUPS, SSM_CH)
    if lp != lreal:
        ug = jnp.pad(ug, ((0, 0), (0, 0), (0, lp - lreal), (0, 0), (0, 0)))
    ug = ug.transpose(3, 1, 0, 2, 4).reshape(SSM_GROUPS, nc * b, lp * SSM_CH)
    h0g = h0.transpose(1, 0, 3, 2).reshape(SSM_GROUPS, b, 2 * SSM_STATE)
    yg, hl = _s5(ug, h0g, prm, lp, lreal, b, nc)
    y = yg.reshape(SSM_GROUPS, nc, b, lp, SSM_CH)[:, :, :, :lreal].transpose(2, 1, 3, 0, 4)
    h_last = hl.reshape(SSM_GROUPS, b, 2, SSM_STATE).transpose(1, 0, 3, 2)
    return y.reshape(b, t, SSM_WIDTH), h_last


def _cmp_proj_kernel(*refs, n_in):
    refs = refs[len(refs) - n_in - 3:]
    x_refs, (w1k_ref, w1v_ref, o_ref) = refs[:n_in], refs[n_in:]
    blocks = [r[...].reshape(r.shape[-2], r.shape[-1]) for r in x_refs]
    x = blocks[0] if n_in == 1 else jnp.concatenate(blocks, axis=0)
    row_w = 4 * HEAD_DIM
    for j in range(4):
        c = jnp.concatenate([x[:, r * row_w + j * HEAD_DIM: r * row_w + (j + 1) * HEAD_DIM]
                             for r in range(CMP_STRIDE)], axis=1).astype(BF16)
        w = w1k_ref if j % 2 == 0 else w1v_ref
        o_ref[0, :, j * 2 * HEAD_DIM:(j + 1) * 2 * HEAD_DIM] = _bdot(c, w[...])


def _cmp_combine_kernel(h_ref, b1_ref, w2k_ref, w2v_ref, kn_ref, o_ref, *, nchunk):
    h = h_ref[0]
    keep = lax.broadcasted_iota(jnp.int32, (nchunk, 1), 0) < nchunk - 1
    for j in range(4):
        kvh, is_v = j // 2, j % 2
        lo = h[:, j * 2 * HEAD_DIM: j * 2 * HEAD_DIM + HEAD_DIM]
        hi = h[:, j * 2 * HEAD_DIM + HEAD_DIM:(j + 1) * 2 * HEAD_DIM]
        pre = lo + pltpu.roll(hi, nchunk - 1, axis=0) + b1_ref[is_v:is_v + 1, :]
        out = _bdot(_gelu(pre).astype(BF16), (w2v_ref if is_v else w2k_ref)[...])
        if not is_v:
            out = _rms(out, kn_ref[...])
        c0 = is_v * N_KV_HEADS * HEAD_DIM + kvh * HEAD_DIM
        o_ref[0, :, c0:c0 + HEAD_DIM] = jnp.where(keep, out, 0.0)


def _cmp_combine(h, prm):
    b, nchunk, w = h.shape
    return pl.pallas_call(
        functools.partial(_cmp_combine_kernel, nchunk=nchunk),
        grid=(b,),
        in_specs=[pl.BlockSpec((1, nchunk, w), lambda i: (i, 0, 0)), _resident((2, HEAD_DIM)),
                  _resident((HEAD_DIM, HEAD_DIM)), _resident((HEAD_DIM, HEAD_DIM)), _resident((1, HEAD_DIM))],
        out_specs=pl.BlockSpec((1, nchunk, 4 * HEAD_DIM), lambda i: (i, 0, 0)),
        out_shape=jax.ShapeDtypeStruct((b, nchunk, 4 * HEAD_DIM), F32),
        compiler_params=_cparams(("parallel",), 40),
        name="cmp_combine",
    )(h, prm["cmp_b1"], prm["cmp_w2_k"], prm["cmp_w2_v"], prm["k_norm_cmp"])


def _compress_prompt(kv_cmp, prm):
    b, t, w = kv_cmp.shape
    nchunk = t // CMP_STRIDE
    x = kv_cmp.reshape(b, nchunk, CMP_STRIDE * w)
    h = pl.pallas_call(
        functools.partial(_cmp_proj_kernel, n_in=1),
        grid=(b,),
        in_specs=[pl.BlockSpec((1, nchunk, CMP_STRIDE * w), lambda i: (i, 0, 0)),
                  _resident((CMP_STRIDE * HEAD_DIM, 2 * HEAD_DIM)), _resident((CMP_STRIDE * HEAD_DIM, 2 * HEAD_DIM))],
        out_specs=pl.BlockSpec((1, nchunk, 8 * HEAD_DIM), lambda i: (i, 0, 0)),
        out_shape=jax.ShapeDtypeStruct((b, nchunk, 8 * HEAD_DIM), F32),
        compiler_params=_cparams(("parallel",), 40),
        name="cmp_proj_prompt",
    )(x, prm["cmp_w1_k"], prm["cmp_w1_v"])
    return _cmp_combine(h, prm)


PAGES_PER_STEP = 16


def _compress_sample(cache_cmp, layer, page_table, prm):
    bsz, n_pages = page_table.shape
    chunks_per_page = PAGE_SIZE // CMP_STRIDE
    row_w = CMP_STRIDE * 4 * HEAD_DIM
    pages = cache_cmp.reshape(cache_cmp.shape[0], cache_cmp.shape[1], chunks_per_page, row_w)
    steps = n_pages // PAGES_PER_STEP

    def page_spec(p):
        return pl.BlockSpec((None, None, chunks_per_page, row_w),
                            lambda b, s, pt: (layer, pt[b * n_pages + s * PAGES_PER_STEP + p], 0, 0))

    rows = PAGES_PER_STEP * chunks_per_page
    h = pl.pallas_call(
        functools.partial(_cmp_proj_kernel, n_in=PAGES_PER_STEP),
        grid_spec=pltpu.PrefetchScalarGridSpec(
            num_scalar_prefetch=1,
            grid=(bsz, steps),
            in_specs=[page_spec(p) for p in range(PAGES_PER_STEP)] + [
                pl.BlockSpec((CMP_STRIDE * HEAD_DIM, 2 * HEAD_DIM), lambda b, s, pt: (0, 0)),
                pl.BlockSpec((CMP_STRIDE * HEAD_DIM, 2 * HEAD_DIM), lambda b, s, pt: (0, 0))],
            out_specs=pl.BlockSpec((1, rows, 8 * HEAD_DIM), lambda b, s, pt: (b, s, 0)),
        ),
        out_shape=jax.ShapeDtypeStruct((bsz, n_pages * chunks_per_page, 8 * HEAD_DIM), F32),
        compiler_params=_cparams(("parallel", "parallel"), 40),
        name="cmp_proj_sample",
    )(page_table.reshape(-1), *([pages] * PAGES_PER_STEP), prm["cmp_w1_k"], prm["cmp_w1_v"])
    return _cmp_combine(h, prm)


def _cmp_select_kernel(q_ref, ck_ref, cv_ref, o_ref, rank_ref, *idx_ref, tq, nc, nbp, nblk, qpos0):
    kvh = pl.program_id(1)
    qpos = qpos0 + pl.program_id(2) * tq + lax.broadcasted_iota(jnp.int32, (tq, 1), 0)
    start = lax.broadcasted_iota(jnp.int32, (1, nc), 1) * CMP_STRIDE
    valid = (start + (CMP_LEN - 1)) <= qpos
    dist = (qpos - start).astype(F32) - (CMP_LEN - 1) / 2
    ck = ck_ref[0].astype(BF16)
    cv = cv_ref[0].astype(BF16)
    psum = jnp.zeros((tq, nc), F32)
    for g in range(GQA):
        sl = slice(g * HEAD_DIM, (g + 1) * HEAD_DIM)
        s = _bdot_nt(q_ref[0, :, sl], ck) * SCALE - _head_slope(kvh, g) * dist
        _, e, den = _softmax_parts(s, valid)
        p = e / jnp.maximum(den, F32_TINY)
        o_ref[0, :, sl] = _bdot(p.astype(BF16), cv)
        psum = psum + p

    per_shift = (SEL_LEN // CMP_STRIDE).bit_length() - 1
    ci = lax.broadcasted_iota(jnp.int32, (nc, nbp), 0)
    bj = lax.broadcasted_iota(jnp.int32, (nc, nbp), 1)
    overlap = (jnp.where(ci >> per_shift == bj, 1.0, 0.0)
               + jnp.where((ci + 1) >> per_shift == bj, 1.0, 0.0)).astype(BF16)
    score = _dot_small_int(psum, overlap)

    j = lax.broadcasted_iota(jnp.int32, (1, nbp), 1)
    qb = qpos >> SEL_SHIFT
    forced = (j == 0) | (j == qb) | (j == qb - 1)
    causal = (j * SEL_LEN) <= qpos
    score = jnp.where(forced, FORCED_SCORE, jnp.where(causal, score, -1.0))
    score = jnp.where(j < nblk, score, -2.0)

    rank = jnp.zeros((tq, nbp), F32)
    for i in range(nblk):
        si = score[:, i:i + 1]
        rank = rank + jnp.where(j > i, jnp.where(si >= score, 1.0, 0.0), jnp.where(si > score, 1.0, 0.0))
    rank_ref[0, 0] = rank

    if idx_ref:
        jf = j.astype(F32)
        lane = lax.broadcasted_iota(jnp.int32, (tq, V7X_LANES), 1)
        idx = jnp.zeros((tq, V7X_LANES), F32)
        for r in range(N_SEL):
            col = jnp.sum(jnp.where(rank == float(r), jf, 0.0), axis=-1, keepdims=True)
            idx = jnp.where(lane == r, col, idx)
        idx_ref[0][0, 0] = idx.astype(jnp.int32)


def _cmp_select(q, ckv, tq, nblk, qpos0, want_idx):
    b, t, _ = q.shape
    nc = ckv.shape[1]
    nbp = -(-nblk // V7X_LANES) * V7X_LANES
    hw = GQA * HEAD_DIM
    out_specs = [pl.BlockSpec((1, tq, hw), lambda bi, k, i: (bi, i, k)),
                 pl.BlockSpec((1, 1, tq, nbp), lambda bi, k, i: (bi, k, i, 0))]
    out_shape = [jax.ShapeDtypeStruct((b, t, ATTN_WIDTH), F32),
                 jax.ShapeDtypeStruct((b, N_KV_HEADS, t, nbp), F32)]
    if want_idx:
        out_specs.append(pl.BlockSpec((1, 1, tq, V7X_LANES), lambda bi, k, i: (bi, k, i, 0)))
        out_shape.append(jax.ShapeDtypeStruct((b, N_KV_HEADS, t, V7X_LANES), jnp.int32))
    return pl.pallas_call(
        functools.partial(_cmp_select_kernel, tq=tq, nc=nc, nbp=nbp, nblk=nblk, qpos0=qpos0),
        grid=(b, N_KV_HEADS, t // tq),
        in_specs=[pl.BlockSpec((1, tq, hw), lambda bi, k, i: (bi, i, k)),
                  pl.BlockSpec((1, nc, HEAD_DIM), lambda bi, k, i: (bi, 0, k)),
                  pl.BlockSpec((1, nc, HEAD_DIM), lambda bi, k, i: (bi, 0, N_KV_HEADS + k))],
        out_specs=out_specs,
        out_shape=out_shape,
        compiler_params=_cparams(("parallel", "parallel", "parallel"), 40),
        name="cmp_select",
    )(q, ckv, ckv)


SEL_TQ = 128
SEL_TK = 512


def _sel_prompt_kernel(q_ref, k_ref, v_ref, rank_ref, o_ref, m_ref, l_ref, acc_ref):
    kvh = pl.program_id(1)
    q0 = pl.program_id(2) * SEL_TQ
    qpos = q0 + lax.broadcasted_iota(jnp.int32, (SEL_TQ, 1), 0)
    chosen = jnp.where(rank_ref[0, 0] < float(N_SEL), 1.0, 0.0).astype(BF16)
    nbp = chosen.shape[1]
    m_ref[...] = jnp.full(m_ref.shape, NEG_INF, F32)
    l_ref[...] = jnp.zeros(l_ref.shape, F32)
    acc_ref[...] = jnp.zeros(acc_ref.shape, F32)

    def tile(jt, carry):
        k0 = pl.multiple_of(jt * SEL_TK, SEL_TK)
        kb = k_ref[0, pl.ds(k0, SEL_TK), :].astype(BF16)
        vb = v_ref[0, pl.ds(k0, SEL_TK), :].astype(BF16)
        kpos = k0 + lax.broadcasted_iota(jnp.int32, (1, SEL_TK), 1)
        blk_of_key = (k0 + lax.broadcasted_iota(jnp.int32, (nbp, SEL_TK), 1)) >> SEL_SHIFT
        expand = jnp.where(blk_of_key == lax.broadcasted_iota(jnp.int32, (nbp, SEL_TK), 0), 1.0, 0.0).astype(BF16)
        valid = (_bdot(chosen, expand) > 0.5) & (kpos <= qpos)
        dist = (qpos - kpos).astype(F32)
        for g in range(GQA):
            s = _bdot_nt(q_ref[0, :, g * HEAD_DIM:(g + 1) * HEAD_DIM], kb) * SCALE - _head_slope(kvh, g) * dist
            _online_update(m_ref, l_ref, acc_ref, g, s, valid, vb)
        return carry

    lax.fori_loop(0, (q0 + SEL_TQ + SEL_TK - 1) // SEL_TK, tile, 0)
    for g in range(GQA):
        o_ref[0, :, g * HEAD_DIM:(g + 1) * HEAD_DIM] = acc_ref[g] / jnp.maximum(l_ref[g], F32_TINY)


def _sel_prompt(q, kv_slc, rank):
    b, t, _ = q.shape
    nbp = rank.shape[-1]
    hw = GQA * HEAD_DIM
    return pl.pallas_call(
        _sel_prompt_kernel,
        grid=(b, N_KV_HEADS, t // SEL_TQ),
        in_specs=[pl.BlockSpec((1, SEL_TQ, hw), lambda bi, k, i: (bi, i, k)),
                  pl.BlockSpec((1, t, HEAD_DIM), lambda bi, k, i: (bi, 0, 2 * k)),
                  pl.BlockSpec((1, t, HEAD_DIM), lambda bi, k, i: (bi, 0, 2 * k + 1)),
                  pl.BlockSpec((1, 1, SEL_TQ, nbp), lambda bi, k, i: (bi, k, i, 0))],
        out_specs=pl.BlockSpec((1, SEL_TQ, hw), lambda bi, k, i: (bi, i, k)),
        out_shape=jax.ShapeDtypeStruct((b, t, ATTN_WIDTH), F32),
        scratch_shapes=[pltpu.VMEM((GQA, SEL_TQ, 1), F32), pltpu.VMEM((GQA, SEL_TQ, 1), F32),
                        pltpu.VMEM((GQA, SEL_TQ, HEAD_DIM), F32)],
        compiler_params=_cparams(("parallel", "parallel", "parallel"), 32),
        name="sel_prompt",
    )(q, kv_slc, kv_slc, rank)


def _win_prompt_kernel(q_ref, k_ref, v_ref, o_ref):
    kvh = pl.program_id(1)
    i = pl.program_id(2)
    span = WINDOW + WIN_QBLK
    qpos = i * WIN_QBLK + lax.broadcasted_iota(jnp.int32, (WIN_QBLK, 1), 0)
    k0 = pl.multiple_of(jnp.maximum(i * WIN_QBLK - WINDOW, 0), WIN_QBLK)
    kb = k_ref[0, pl.ds(k0, span), :].astype(BF16)
    vb = v_ref[0, pl.ds(k0, span), :].astype(BF16)
    d = qpos - (k0 + lax.broadcasted_iota(jnp.int32, (1, span), 1))
    valid = (d >= 0) & (d < WINDOW)
    dist = d.astype(F32)
    for g in range(GQA):
        sl = slice(g * HEAD_DIM, (g + 1) * HEAD_DIM)
        s = _bdot_nt(q_ref[0, :, sl], kb) * SCALE - _head_slope(kvh, g) * dist
        _, e, den = _softmax_parts(s, valid)
        o_ref[0, :, sl] = _bdot(e.astype(BF16), vb) / jnp.maximum(den, F32_TINY)


def _win_prompt(q, kv_win):
    b, t, _ = q.shape
    hw = GQA * HEAD_DIM
    return pl.pallas_call(
        _win_prompt_kernel,
        grid=(b, N_KV_HEADS, t // WIN_QBLK),
        in_specs=[pl.BlockSpec((1, WIN_QBLK, hw), lambda bi, k, i: (bi, i, k)),
                  pl.BlockSpec((1, t, HEAD_DIM), lambda bi, k, i: (bi, 0, 2 * k)),
                  pl.BlockSpec((1, t, HEAD_DIM), lambda bi, k, i: (bi, 0, 2 * k + 1))],
        out_specs=pl.BlockSpec((1, WIN_QBLK, hw), lambda bi, k, i: (bi, i, k)),
        out_shape=jax.ShapeDtypeStruct((b, t, ATTN_WIDTH), F32),
        compiler_params=_cparams(("parallel", "parallel", "parallel"), 32),
        name="win_prompt",
    )(q, kv_win, kv_win)


def _sel_sample_kernel(idx_ref, pt_ref, q_ref, kp_ref, vp_ref, kn_ref, vn_ref, slope_ref, o_ref,
                       m_ref, l_ref, acc_ref, *, t_len):
    b, kvh, s = pl.program_id(0), pl.program_id(1), pl.program_id(2)
    t, r = s // N_SEL, s % N_SEL
    past_blocks = PAST_LEN // SEL_LEN

    @pl.when(r == 0)
    def _():
        m_ref[...] = jnp.full(m_ref.shape, NEG_INF, F32)
        l_ref[...] = jnp.zeros(l_ref.shape, F32)
        acc_ref[...] = jnp.zeros(acc_ref.shape, F32)

    blk = idx_ref[((b * N_KV_HEADS + kvh) * t_len + t) * N_SEL + r]
    is_new = blk >= past_blocks
    kb = jnp.where(is_new, kn_ref[0], kp_ref[...]).astype(BF16)
    vb = jnp.where(is_new, vn_ref[0], vp_ref[...]).astype(BF16)
    kpos = blk * SEL_LEN + lax.broadcasted_iota(jnp.int32, (1, SEL_LEN), 1)
    d = (PAST_LEN + t) - kpos
    sc = _bdot_nt(q_ref[0, 0, 0], kb) * SCALE - slope_ref[0] * d.astype(F32)
    _online_update(m_ref, l_ref, acc_ref, 0, sc, d >= 0, vb)

    @pl.when(r == N_SEL - 1)
    def _():
        o_ref[0, 0, 0] = acc_ref[0] / jnp.maximum(l_ref[0], F32_TINY)


def _sel_sample(q_h, cache_slc, layer, page_table, idx, kv_new_pad, slopes):
    bsz, _, t_len = q_h.shape[:3]
    n_pages = page_table.shape[1]
    per_page = PAGE_SIZE // SEL_LEN
    past_blocks = PAST_LEN // SEL_LEN
    pages = cache_slc.reshape(cache_slc.shape[0], cache_slc.shape[1], per_page, SEL_LEN, 4 * HEAD_DIM)

    def past_spec(is_v):
        def imap(b, k, s, idx_ref, pt_ref):
            blk = jnp.minimum(idx_ref[((b * N_KV_HEADS + k) * t_len + s // N_SEL) * N_SEL + s % N_SEL],
                              past_blocks - 1)
            return (layer, pt_ref[b * n_pages + blk // per_page], blk % per_page, 0, 2 * k + is_v)
        return pl.BlockSpec((None, None, None, SEL_LEN, HEAD_DIM), imap)

    qspec = pl.BlockSpec((1, 1, 1, GQA, HEAD_DIM), lambda b, k, s, i_, p_: (b, k, s // N_SEL, 0, 0))
    return pl.pallas_call(
        functools.partial(_sel_sample_kernel, t_len=t_len),
        grid_spec=pltpu.PrefetchScalarGridSpec(
            num_scalar_prefetch=2,
            grid=(bsz, N_KV_HEADS, t_len * N_SEL),
            in_specs=[qspec, past_spec(0), past_spec(1),
                      pl.BlockSpec((1, SEL_LEN, HEAD_DIM), lambda b, k, s, i_, p_: (b, 0, 2 * k)),
                      pl.BlockSpec((1, SEL_LEN, HEAD_DIM), lambda b, k, s, i_, p_: (b, 0, 2 * k + 1)),
                      pl.BlockSpec((1, GQA, 1), lambda b, k, s, i_, p_: (k, 0, 0))],
            out_specs=qspec,
            scratch_shapes=[pltpu.VMEM((1, GQA, 1), F32), pltpu.VMEM((1, GQA, 1), F32),
                            pltpu.VMEM((1, GQA, HEAD_DIM), F32)],
        ),
        out_shape=jax.ShapeDtypeStruct(q_h.shape, F32),
        compiler_params=_cparams(("parallel", "parallel", "arbitrary"), 16),
        name="sel_sample",
    )(idx.reshape(-1), page_table.reshape(-1), q_h, pages, pages, kv_new_pad, kv_new_pad, slopes)


def _win_sample_kernel(q_ref, kp_ref, vp_ref, kn_ref, vn_ref, slope_ref, o_ref, *, t_len, t_pad):
    wbuf = kp_ref.shape[1]
    t = lax.broadcasted_iota(jnp.int32, (t_len, 1), 0)
    d_past = wbuf + t - lax.broadcasted_iota(jnp.int32, (1, wbuf), 1)
    row_new = lax.broadcasted_iota(jnp.int32, (1, t_pad), 1)
    d_new = t - row_new
    valid_past = (d_past >= 0) & (d_past < WINDOW)
    valid_new = (d_new >= 0) & (d_new < WINDOW) & (row_new < t_len)
    kp, vp = kp_ref[0].astype(BF16), vp_ref[0].astype(BF16)
    kn, vn = kn_ref[0].astype(BF16), vn_ref[0].astype(BF16)
    for g in range(GQA):
        q = q_ref[0, 0, g]
        slope = slope_ref[0, g:g + 1, :]
        m1, e1, z1 = _softmax_parts(_bdot_nt(q, kp) * SCALE - slope * d_past.astype(F32), valid_past)
        m2, e2, z2 = _softmax_parts(_bdot_nt(q, kn) * SCALE - slope * d_new.astype(F32), valid_new)
        m = jnp.maximum(m1, m2)
        a1 = jnp.where(z1 > 0.0, jnp.exp(m1 - m), 0.0)
        a2 = jnp.where(z2 > 0.0, jnp.exp(m2 - m), 0.0)
        num = a1 * _bdot(e1.astype(BF16), vp) + a2 * _bdot(e2.astype(BF16), vn)
        o_ref[0, 0, g] = num / jnp.maximum(a1 * z1 + a2 * z2, F32_TINY)


def _win_sample(q_g, win_buf, kv_new_pad, slopes):
    bsz, _, _, t_len, _ = q_g.shape
    wbuf, t_pad = win_buf.shape[1], kv_new_pad.shape[1]
    qspec = pl.BlockSpec((1, 1, GQA, t_len, HEAD_DIM), lambda b, k: (b, k, 0, 0, 0))
    return pl.pallas_call(
        functools.partial(_win_sample_kernel, t_len=t_len, t_pad=t_pad),
        grid=(bsz, N_KV_HEADS),
        in_specs=[qspec,
                  pl.BlockSpec((1, wbuf, HEAD_DIM), lambda b, k: (b, 0, 2 * k)),
                  pl.BlockSpec((1, wbuf, HEAD_DIM), lambda b, k: (b, 0, 2 * k + 1)),
                  pl.BlockSpec((1, t_pad, HEAD_DIM), lambda b, k: (b, 0, 2 * k)),
                  pl.BlockSpec((1, t_pad, HEAD_DIM), lambda b, k: (b, 0, 2 * k + 1)),
                  pl.BlockSpec((1, GQA, 1), lambda b, k: (k, 0, 0))],
        out_specs=qspec,
        out_shape=jax.ShapeDtypeStruct(q_g.shape, F32),
        compiler_params=_cparams(("parallel", "parallel"), 16),
        name="win_sample",
    )(q_g, win_buf, win_buf, kv_new_pad, kv_new_pad, slopes)


def _mix_kernel(x_ref, y_ref, oc_ref, os_ref, ow_ref, gate_ref, wglu_ref, gs_ref, ga_ref, wout_ref, o_ref):
    z = _gelu(y_ref[...])
    z = z * jax.nn.sigmoid(_bdot(z.astype(BF16), wglu_ref[...]))
    gate = gate_ref[...]
    cols = []
    for h in range(N_HEADS):
        sl = slice(h * HEAD_DIM, (h + 1) * HEAD_DIM)
        cols.append(gate[:, h:h + 1] * oc_ref[:, sl] + gate[:, N_HEADS + h:N_HEADS + h + 1] * os_ref[:, sl]
                    + gate[:, 2 * N_HEADS + h:2 * N_HEADS + h + 1] * ow_ref[:, sl])
    attn = jnp.concatenate(cols, axis=1)
    hs = _rms(z, gs_ref[...]).astype(BF16)
    ha = _rms(attn, ga_ref[...]).astype(BF16)
    o_ref[...] = x_ref[...] + (_bdot(hs, wout_ref[:SSM_WIDTH, :]) + _bdot(ha, wout_ref[SSM_WIDTH:, :]))


def _mix(x, y, oc, osl, ow, gates, prm, tm):
    m = x.shape[0]
    row = lambda n: pl.BlockSpec((tm, n), lambda i: (i, 0))
    return pl.pallas_call(
        _mix_kernel,
        grid=(m // tm,),
        in_specs=[row(D_MODEL), row(SSM_WIDTH), row(ATTN_WIDTH), row(ATTN_WIDTH), row(ATTN_WIDTH), row(GATE_PAD),
                  _resident((SSM_WIDTH, SSM_WIDTH)), _resident((1, SSM_WIDTH)), _resident((1, ATTN_WIDTH)),
                  _resident((D_MODEL, D_MODEL))],
        out_specs=row(D_MODEL),
        out_shape=jax.ShapeDtypeStruct((m, D_MODEL), F32),
        compiler_params=_cparams(("parallel",), 48),
        name="mix_outproj",
    )(x, y, oc, osl, ow, gates, prm["ssm_w_glu"], prm["norm_ssm_out"], prm["norm_attn_out"], prm["w_out"])


FFN_TF = 512


def _ffn_kernel(x_ref, g_ref, wg_ref, wu_ref, wd_ref, o_ref, xn_ref, acc_ref):
    j = pl.program_id(1)

    @pl.when(j == 0)
    def _():
        xn_ref[...] = _rms(x_ref[...], g_ref[...]).astype(BF16)
        acc_ref[...] = jnp.zeros(acc_ref.shape, F32)

    xn = xn_ref[...]
    a = _bdot(xn, wg_ref[...])
    h = (a * jax.nn.sigmoid(a)) * _bdot(xn, wu_ref[...])
    acc_ref[...] += _bdot(h.astype(BF16), wd_ref[...])

    @pl.when(j == pl.num_programs(1) - 1)
    def _():
        o_ref[...] = x_ref[...] + acc_ref[...]


def _ffn(x, prm, tm):
    m = x.shape[0]
    return pl.pallas_call(
        _ffn_kernel,
        grid=(m // tm, D_FF // FFN_TF),
        in_specs=[pl.BlockSpec((tm, D_MODEL), lambda i, j: (i, 0)),
                  pl.BlockSpec((1, D_MODEL), lambda i, j: (0, 0)),
                  pl.BlockSpec((D_MODEL, FFN_TF), lambda i, j: (0, j)),
                  pl.BlockSpec((D_MODEL, FFN_TF), lambda i, j: (0, j)),
                  pl.BlockSpec((FFN_TF, D_MODEL), lambda i, j: (j, 0))],
        out_specs=pl.BlockSpec((tm, D_MODEL), lambda i, j: (i, 0)),
        out_shape=jax.ShapeDtypeStruct((m, D_MODEL), F32),
        scratch_shapes=[pltpu.VMEM((tm, D_MODEL), BF16), pltpu.VMEM((tm, D_MODEL), F32)],
        compiler_params=_cparams(("parallel", "arbitrary"), 48),
        name="ffn",
    )(x, prm["norm_ffn"], prm["w_ffn_gate"], prm["w_ffn_up"], prm["w_ffn_down"])


S5_CHUNK = 64
SAMPLE_S5_PAD = 8


def _layer_params(l, w):
    kvh = jnp.arange(N_KV_HEADS)[:, None] * HEAD_DIM + jnp.arange(HEAD_DIM)[None, :]
    base = D_MODEL
    cols = [jnp.arange(D_MODEL)]
    for br in range(3):
        k0, v0 = base + 2 * br * KV_WIDTH, base + (2 * br + 1) * KV_WIDTH
        cols.append(jnp.stack([k0 + kvh, v0 + kvh], axis=1).reshape(-1))
    g0 = base + 6 * KV_WIDTH
    cols.append((g0 + jnp.arange(N_HEADS)[None, :] * 3 + jnp.arange(3)[:, None]).reshape(-1))
    w_in = jnp.take(w["w_in"][l], jnp.concatenate(cols), axis=1)
    w_in = jnp.pad(w_in, ((0, 0), (0, IN_WIDTH_PAD - w_in.shape[1]))).astype(BF16)

    def w1cat(w1):
        half = CMP_STRIDE * HEAD_DIM
        return jnp.concatenate([w1[:half], w1[half:]], axis=1).astype(BF16)

    row = lambda v: v.reshape(1, -1)
    a_re, a_im = w["ssm_a_re"][l], w["ssm_a_im"][l]
    log_dt = jnp.broadcast_to(w["ssm_log_dt"][l][:, None], a_re.shape)
    lrow = jnp.stack([a_re, a_im, log_dt], axis=1)
    d = w["ssm_d"][l]
    return {
        "w_in": w_in, "norm_mix": row(w["norm_mix"][l]),
        "q_norm": row(w["q_norm"][l]), "k_norm_slc": row(w["k_norm_slc"][l]),
        "k_norm_win": row(w["k_norm_win"][l]), "k_norm_cmp": row(w["k_norm_cmp"][l]),
        "lrow": lrow, "lcol": lrow.transpose(0, 2, 1),
        "bt": jnp.stack([w["ssm_b_re"][l], w["ssm_b_im"][l]], axis=1).transpose(0, 1, 3, 2),
        "bn": jnp.stack([w["ssm_b_re"][l], w["ssm_b_im"][l]], axis=1),
        "ct": jnp.stack([w["ssm_c_re"][l], w["ssm_c_im"][l]], axis=1).transpose(0, 1, 3, 2),
        "dtile": {lp: jnp.tile(d, (1, lp))[:, None, :] for lp in (S5_CHUNK, SAMPLE_S5_PAD)},
        "ssm_w_glu": w["ssm_w_glu"][l].astype(BF16),
        "cmp_w1_k": w1cat(w["cmp_w1_k"][l]), "cmp_w1_v": w1cat(w["cmp_w1_v"][l]),
        "cmp_b1": jnp.stack([w["cmp_b1_k"][l], w["cmp_b1_v"][l]]),
        "cmp_w2_k": w["cmp_w2_k"][l].astype(BF16), "cmp_w2_v": w["cmp_w2_v"][l].astype(BF16),
        "norm_ssm_out": row(w["norm_ssm_out"][l]), "norm_attn_out": row(w["norm_attn_out"][l]),
        "w_out": w["w_out"][l].astype(BF16), "norm_ffn": row(w["norm_ffn"][l]),
        "w_ffn_gate": w["w_ffn_gate"][l].astype(BF16), "w_ffn_up": w["w_ffn_up"][l].astype(BF16),
        "w_ffn_down": w["w_ffn_down"][l].astype(BF16),
    }


def _kv_out(kv, b, t):
    return kv.reshape(b, t, N_KV_HEADS, 2, HEAD_DIM)


PROMPT_TM = 256
FFN_TM = 512


def _prompt_layer(x, prm):
    b, t, _ = x.shape
    m = b * t
    x2 = x.reshape(m, D_MODEL)
    u, q, kv_cmp, kv_slc, kv_win, gates = _inproj(x2, prm["norm_mix"], prm["w_in"], prm["q_norm"],
                                                  prm["k_norm_slc"], prm["k_norm_win"], PROMPT_TM)
    h0 = jnp.zeros((b, SSM_GROUPS, SSM_STATE, 2), F32)
    y, h_last = _s5_mixer(u.reshape(b, t, SSM_WIDTH), h0, prm, S5_CHUNK, S5_CHUNK)
    q3 = q.reshape(b, t, ATTN_WIDTH)
    kv_cmp3, kv_slc3, kv_win3 = (a.reshape(b, t, 2 * KV_WIDTH) for a in (kv_cmp, kv_slc, kv_win))
    ckv = _compress_prompt(kv_cmp3, prm)
    o_cmp, rank = _cmp_select(q3, ckv, 256, t // SEL_LEN, 0, False)
    o_slc = _sel_prompt(q3, kv_slc3, rank)
    o_win = _win_prompt(q3, kv_win3)
    x2 = _mix(x2, y.reshape(m, SSM_WIDTH), o_cmp.reshape(m, -1), o_slc.reshape(m, -1), o_win.reshape(m, -1),
              gates, prm, PROMPT_TM)
    x2 = _ffn(x2, prm, FFN_TM)
    wbuf = min(WINDOW, PAST_LEN)
    win_state = jnp.pad(kv_win3, ((0, 0), (max(wbuf - t, 0), 0), (0, 0)))[:, -wbuf:]
    return (x2.reshape(b, t, D_MODEL), _kv_out(kv_cmp3, b, t), _kv_out(kv_slc3, b, t),
            _kv_out(win_state, b, wbuf), h_last)


def _sample_layer(x, prm, cache_cmp, cache_slc, layer, page_table, win_buf, h0, slopes):
    b, t, _ = x.shape
    m = b * t
    assert (PAST_LEN + t) // CMP_STRIDE == PAST_LEN // CMP_STRIDE and PAST_LEN % PAGE_SIZE == 0
    x2 = x.reshape(m, D_MODEL)
    u, q, kv_cmp, kv_slc, kv_win, gates = _inproj(x2, prm["norm_mix"], prm["w_in"], prm["q_norm"],
                                                  prm["k_norm_slc"], prm["k_norm_win"], m)
    y, h_last = _s5_mixer(u.reshape(b, t, SSM_WIDTH), h0, prm, SAMPLE_S5_PAD, t)
    q3 = q.reshape(b, t, ATTN_WIDTH)
    kv_cmp3, kv_slc3, kv_win3 = (a.reshape(b, t, 2 * KV_WIDTH) for a in (kv_cmp, kv_slc, kv_win))
    ckv = _compress_sample(cache_cmp, layer, page_table, prm)
    n_blocks = -(-(PAST_LEN + t) // SEL_LEN)
    o_cmp, _, idx = _cmp_select(q3, ckv, t, n_blocks, PAST_LEN, True)
    q5 = q3.reshape(b, t, N_KV_HEADS, GQA, HEAD_DIM)
    slc_pad = jnp.pad(kv_slc3, ((0, 0), (0, SEL_LEN - t), (0, 0)))
    o_slc = _sel_sample(q5.transpose(0, 2, 1, 3, 4), cache_slc, layer, page_table, idx[..., :N_SEL], slc_pad, slopes)
    o_slc = o_slc.transpose(0, 2, 1, 3, 4).reshape(m, ATTN_WIDTH)
    wbuf = win_buf.shape[1]
    win_buf3 = win_buf.reshape(b, wbuf, 2 * KV_WIDTH)
    win_pad = jnp.pad(kv_win3, ((0, 0), (0, 8 - t), (0, 0)))
    o_win = _win_sample(q5.transpose(0, 2, 3, 1, 4), win_buf3, win_pad, slopes)
    o_win = o_win.transpose(0, 3, 1, 2, 4).reshape(m, ATTN_WIDTH)
    x2 = _mix(x2, y.reshape(m, SSM_WIDTH), o_cmp.reshape(m, -1), o_slc, o_win, gates, prm, m)
    x2 = _ffn(x2, prm, m)
    win_state = jnp.concatenate([win_buf3, kv_win3], axis=1)[:, -wbuf:]
    return (x2.reshape(b, t, D_MODEL), _kv_out(kv_cmp3, b, t), _kv_out(kv_slc3, b, t),
            _kv_out(win_state, b, wbuf), h_last)


def kernel(x_prompt, x_sample, cache_cmp_kv, cache_slc_kv, state_win_kv, state_ssm, page_table, norm_mix, w_in, ssm_a_re, ssm_a_im, ssm_log_dt, ssm_b_re, ssm_b_im, ssm_c_re, ssm_c_im, ssm_d, ssm_w_glu, q_norm, k_norm_cmp, k_norm_slc, k_norm_win, cmp_w1_k, cmp_b1_k, cmp_w2_k, cmp_w1_v, cmp_b1_v, cmp_w2_v, norm_ssm_out, norm_attn_out, w_out, norm_ffn, w_ffn_gate, w_ffn_up, w_ffn_down):
    w = dict(norm_mix=norm_mix, w_in=w_in, ssm_a_re=ssm_a_re, ssm_a_im=ssm_a_im, ssm_log_dt=ssm_log_dt,
             ssm_b_re=ssm_b_re, ssm_b_im=ssm_b_im, ssm_c_re=ssm_c_re, ssm_c_im=ssm_c_im, ssm_d=ssm_d,
             ssm_w_glu=ssm_w_glu, q_norm=q_norm, k_norm_cmp=k_norm_cmp, k_norm_slc=k_norm_slc,
             k_norm_win=k_norm_win, cmp_w1_k=cmp_w1_k, cmp_b1_k=cmp_b1_k, cmp_w2_k=cmp_w2_k,
             cmp_w1_v=cmp_w1_v, cmp_b1_v=cmp_b1_v, cmp_w2_v=cmp_w2_v, norm_ssm_out=norm_ssm_out,
             norm_attn_out=norm_attn_out, w_out=w_out, norm_ffn=norm_ffn, w_ffn_gate=w_ffn_gate,
             w_ffn_up=w_ffn_up, w_ffn_down=w_ffn_down)
    heads = jnp.arange(1, N_HEADS + 1, dtype=F32)
    slopes = (2.0 ** (-8.0 * heads / N_HEADS)).reshape(N_KV_HEADS, GQA, 1)
    y_p, y_s = x_prompt, x_sample
    outs_p, outs_s = [], []
    for l in range(DEPTH):
        prm = _layer_params(l, w)
        y_p, *rest = _prompt_layer(y_p, prm)
        outs_p.append(rest)
        y_s, *rest = _sample_layer(y_s, prm, cache_cmp_kv, cache_slc_kv, l, page_table,
                                   state_win_kv[l], state_ssm[l], slopes)
        outs_s.append(rest)
    stack = lambda outs, i: jnp.stack([o[i] for o in outs])
    return (y_p, y_s, stack(outs_p, 0), stack(outs_p, 1), stack(outs_p, 2), stack(outs_p, 3),
            stack(outs_s, 0), stack(outs_s, 1), stack(outs_s, 2), stack(outs_s, 3))
```

```python
import functools
import math

import jax
import jax.numpy as jnp
from jax import lax
from jax.experimental import pallas as pl
from jax.experimental.pallas import tpu as pltpu

F32 = jnp.float32
BF16 = jnp.bfloat16

D_MODEL = 2048
DEPTH = 2
PAST_LEN = 16384
PAGE_SIZE = 128
SSM_WIDTH = 1024
ATTN_WIDTH = 1024
SSM_CH = 16
SSM_GROUPS = 64
SSM_STATE = 64
HEAD_DIM = 128
N_HEADS = 8
N_KV_HEADS = 2
GQA = 4
KV_WIDTH = 256
CMP_LEN = 32
CMP_STRIDE = 16
SEL_LEN = 64
SEL_SHIFT = SEL_LEN.bit_length() - 1
N_SEL = 16
WINDOW = 512
WIN_QBLK = 128
FORCED_SCORE = 1e4
D_FF = 5632
NORM_EPS = 1e-6
SCALE = HEAD_DIM ** -0.5
GATE_PAD = 128
IN_WIDTH_PAD = SSM_WIDTH + ATTN_WIDTH + 6 * KV_WIDTH + GATE_PAD
F32_TINY = float(jnp.finfo(jnp.float32).tiny)
NEG_INF = float("-inf")

V7X_VMEM_BYTES = 64 * 1024 * 1024
V7X_LANES = 128


def _cparams(semantics, vmem_mib):
    assert vmem_mib * 1024 * 1024 < V7X_VMEM_BYTES
    return pltpu.CompilerParams(dimension_semantics=semantics, vmem_limit_bytes=vmem_mib * 1024 * 1024)


def _resident(shape):
    nd = len(shape)
    return pl.BlockSpec(shape, lambda *_: (0,) * nd, pipeline_mode=pl.Buffered(1))


def _rms(x, g):
    return x * lax.rsqrt(jnp.mean(x * x, axis=-1, keepdims=True) + NORM_EPS) * g


def _gelu(x):
    c = math.sqrt(2.0 / math.pi)
    return x * (0.5 * (1.0 + jnp.tanh(c * (x + 0.044715 * (x * x * x)))))


def _bdot(a, b):
    return jnp.dot(a, b, preferred_element_type=F32)


def _bdot_nt(a, b):
    return lax.dot_general(a, b, (((1,), (1,)), ((), ())), preferred_element_type=F32)


def _split3(x):
    hi = x.astype(BF16)
    r1 = x - hi.astype(F32)
    mid = r1.astype(BF16)
    lo = (r1 - mid.astype(F32)).astype(BF16)
    return hi, mid, lo


def _dot_small_int(x, e):
    hi, mid, lo = _split3(x)
    return _bdot(hi, e) + _bdot(mid, e) + _bdot(lo, e)


def _dot_f32(a, b):
    ah, am, al = _split3(a)
    bh, bm, bl = _split3(b)
    return (_bdot(ah, bh) + (_bdot(ah, bm) + _bdot(am, bh))
            + (_bdot(ah, bl) + _bdot(al, bh) + _bdot(am, bm)))


def _softmax_parts(s, valid):
    s = jnp.where(valid, s, NEG_INF)
    m = jnp.max(s, axis=-1, keepdims=True)
    m = jnp.where(m == NEG_INF, 0.0, m)
    e = jnp.exp(s - m)
    return m, e, jnp.sum(e, axis=-1, keepdims=True)


def _online_update(m_ref, l_ref, acc_ref, g, s, valid, v):
    s = jnp.where(valid, s, NEG_INF)
    m_old = m_ref[g]
    m_new = jnp.maximum(m_old, jnp.max(s, axis=-1, keepdims=True))
    m_safe = jnp.where(m_new == NEG_INF, 0.0, m_new)
    alpha = jnp.exp(m_old - m_safe)
    p = jnp.exp(s - m_safe)
    l_ref[g] = alpha * l_ref[g] + jnp.sum(p, axis=-1, keepdims=True)
    acc_ref[g] = alpha * acc_ref[g] + _bdot(p.astype(BF16), v)
    m_ref[g] = m_new


def _alibi_slope(kvh, g):
    return 2.0 ** (-8.0 * (kvh * GQA + g + 1) / N_HEADS)


def _head_slope(kvh, g):
    assert N_KV_HEADS == 2
    return jnp.where(kvh == 0, _alibi_slope(0, g), _alibi_slope(1, g))


def _inproj_kernel(x_ref, gmix_ref, w_ref, qn_ref, ksn_ref, kwn_ref,
                   u_ref, q_ref, cmp_ref, slc_ref, win_ref, gate_ref):
    xn = _rms(x_ref[...], gmix_ref[...]).astype(BF16)

    def proj(c0, c1):
        return _bdot(xn, w_ref[:, c0:c1])

    u_ref[...] = proj(0, SSM_WIDTH)
    q = proj(SSM_WIDTH, D_MODEL)
    for h in range(N_HEADS):
        sl = slice(h * HEAD_DIM, (h + 1) * HEAD_DIM)
        q_ref[:, sl] = _rms(q[:, sl], qn_ref[...]).astype(BF16)
    c0 = D_MODEL
    cmp_ref[...] = proj(c0, c0 + 2 * KV_WIDTH)
    for ref, nref in ((slc_ref, ksn_ref), (win_ref, kwn_ref)):
        c0 += 2 * KV_WIDTH
        kv = proj(c0, c0 + 2 * KV_WIDTH)
        for kvh in range(N_KV_HEADS):
            ks = slice(kvh * 2 * HEAD_DIM, kvh * 2 * HEAD_DIM + HEAD_DIM)
            vs = slice(kvh * 2 * HEAD_DIM + HEAD_DIM, (kvh + 1) * 2 * HEAD_DIM)
            ref[:, ks] = _rms(kv[:, ks], nref[...])
            ref[:, vs] = kv[:, vs]
    c0 += 2 * KV_WIDTH
    gate_ref[...] = jax.nn.sigmoid(proj(c0, c0 + GATE_PAD))


def _inproj(x, gmix, w, qn, ksn, kwn, tm):
    m = x.shape[0]
    row = lambda n: pl.BlockSpec((tm, n), lambda i: (i, 0))
    kvw = 2 * KV_WIDTH
    return pl.pallas_call(
        _inproj_kernel,
        grid=(m // tm,),
        in_specs=[row(D_MODEL), _resident((1, D_MODEL)), _resident((D_MODEL, IN_WIDTH_PAD)),
                  _resident((1, HEAD_DIM)), _resident((1, HEAD_DIM)), _resident((1, HEAD_DIM))],
        out_specs=[row(SSM_WIDTH), row(ATTN_WIDTH), row(kvw), row(kvw), row(kvw), row(GATE_PAD)],
        out_shape=[jax.ShapeDtypeStruct((m, SSM_WIDTH), F32), jax.ShapeDtypeStruct((m, ATTN_WIDTH), BF16),
                   jax.ShapeDtypeStruct((m, kvw), F32), jax.ShapeDtypeStruct((m, kvw), F32),
                   jax.ShapeDtypeStruct((m, kvw), F32), jax.ShapeDtypeStruct((m, GATE_PAD), F32)],
        compiler_params=_cparams(("parallel",), 48),
        name="inproj",
    )(x, gmix, w, qn, ksn, kwn)


def _s5_kernel(u_ref, h0_ref, lrow_ref, lcol_ref, bt_ref, bn_ref, ct_ref, d_ref,
               y_ref, hl_ref, tm_ref, x_ref, hin_ref, *, lp, lreal, nb, nc):
    k = lp * SSM_CH
    n = SSM_STATE

    def lam_bar(a_re, a_im, log_dt):
        dt = jnp.exp(log_dt)
        e = jnp.exp(a_re * dt)
        return e * jnp.cos(a_im * dt), e * jnp.sin(a_im * dt)

    def zoh_coef(l_re, l_im, a_re, a_im):
        den = a_re * a_re + a_im * a_im
        x_re = l_re - 1.0
        return (x_re * a_re + l_im * a_im) / den, (l_im * a_re - x_re * a_im) / den

    lc = lcol_ref[0]
    a_re_c, a_im_c = lc[:, 0:1], lc[:, 1:2]
    l_re_c, l_im_c = lam_bar(a_re_c, a_im_c, lc[:, 2:3])
    lr = lrow_ref[0]
    a_re_r, a_im_r = lr[0:1], lr[1:2]
    l_re_r, l_im_r = lam_bar(a_re_r, a_im_r, lr[2:3])

    def pow_table(e):
        p_re = jnp.ones((n, k), F32)
        p_im = jnp.zeros((n, k), F32)
        s_re, s_im = l_re_c, l_im_c
        for j in range(max(lp.bit_length() - 1, 1)):
            bit = ((e >> j) & 1) == 1
            m_re = jnp.where(bit, s_re, 1.0)
            m_im = jnp.where(bit, s_im, 0.0)
            p_re, p_im = p_re * m_re - p_im * m_im, p_re * m_im + p_im * m_re
            s_re, s_im = s_re * s_re - s_im * s_im, 2.0 * s_re * s_im
        return p_re, p_im

    lane = lax.broadcasted_iota(jnp.int32, (1, k), 1)
    tau = lane >> (SSM_CH.bit_length() - 1)
    ch_onehot = jnp.where((lax.broadcasted_iota(jnp.int32, (SSM_CH, k), 1) & (SSM_CH - 1))
                          == lax.broadcasted_iota(jnp.int32, (SSM_CH, k), 0), 1.0, 0.0).astype(BF16)

    p_re, p_im = pow_table(tau)
    c_re = _dot_small_int(ct_ref[0, 0], ch_onehot)
    c_im = _dot_small_int(ct_ref[0, 1], ch_onehot)
    g_re = c_re * p_re - c_im * p_im
    g_im = c_re * p_im + c_im * p_re

    cf_re_r, cf_im_r = zoh_coef(l_re_r, l_im_r, a_re_r, a_im_r)
    bbt_re = cf_re_r * bt_ref[0, 0] - cf_im_r * bt_ref[0, 1]
    bbt_im = cf_re_r * bt_ref[0, 1] + cf_im_r * bt_ref[0, 0]
    a = _dot_f32(bbt_re, g_re) - _dot_f32(bbt_im, g_im)

    lane16 = lax.broadcasted_iota(jnp.int32, (SSM_CH, k), 1)
    for s in range(lp):
        blk = a if s == 0 else jnp.where(lane16 >= SSM_CH * s, pltpu.roll(a, SSM_CH * s, axis=1), 0.0)
        tm_ref[SSM_CH * s:SSM_CH * (s + 1), :] = blk.astype(BF16)

    cf_re_c, cf_im_c = zoh_coef(l_re_c, l_im_c, a_re_c, a_im_c)
    bbn_re = cf_re_c * bn_ref[0, 0] - cf_im_c * bn_ref[0, 1]
    bbn_im = cf_re_c * bn_ref[0, 1] + cf_im_c * bn_ref[0, 0]
    bb_re = _dot_small_int(bbn_re, ch_onehot)
    bb_im = _dot_small_int(bbn_im, ch_onehot)
    rev = (lreal - 1) - tau
    q_re, q_im = pow_table(jnp.maximum(rev, 0))
    live = rev >= 0
    wx = jnp.concatenate([jnp.where(live, q_re * bb_re - q_im * bb_im, 0.0),
                          jnp.where(live, q_re * bb_im + q_im * bb_re, 0.0)], axis=0).astype(BF16)

    u = u_ref[0]
    ub = u.astype(BF16)
    x_ref[...] = _bdot_nt(ub, wx)

    ll_re, ll_im = l_re_r, l_im_r
    for _ in range(lreal.bit_length() - 1):
        ll_re, ll_im = ll_re * ll_re - ll_im * ll_im, 2.0 * ll_re * ll_im
    h = h0_ref[0]
    for c in range(nc):
        hin_ref[c * nb:(c + 1) * nb, :] = h
        h_re, h_im = h[:, :n], h[:, n:]
        h = jnp.concatenate([ll_re * h_re - ll_im * h_im, ll_re * h_im + ll_im * h_re], axis=1) \
            + x_ref[c * nb:(c + 1) * nb, :]
    hl_ref[0] = h

    g1_re = g_re * l_re_c - g_im * l_im_c
    g1_im = g_re * l_im_c + g_im * l_re_c
    gs = jnp.concatenate([g1_re, -g1_im], axis=0).astype(BF16)

    y_ref[0] = _bdot(ub, tm_ref[...]) + _bdot(hin_ref[...].astype(BF16), gs) + u * d_ref[0]


def _s5(u_g, h0_g, prm, lp, lreal, nb, nc):
    g, m, k = u_g.shape
    blk = lambda *s: pl.BlockSpec((1,) + s, lambda i: (i,) + (0,) * len(s))
    return pl.pallas_call(
        functools.partial(_s5_kernel, lp=lp, lreal=lreal, nb=nb, nc=nc),
        grid=(g,),
        in_specs=[blk(m, k), blk(nb, 2 * SSM_STATE), blk(3, SSM_STATE), blk(SSM_STATE, 3),
                  blk(2, SSM_CH, SSM_STATE), blk(2, SSM_STATE, SSM_CH), blk(2, SSM_STATE, SSM_CH), blk(1, k)],
        out_specs=[blk(m, k), blk(nb, 2 * SSM_STATE)],
        out_shape=[jax.ShapeDtypeStruct((g, m, k), F32), jax.ShapeDtypeStruct((g, nb, 2 * SSM_STATE), F32)],
        scratch_shapes=[pltpu.VMEM((k, k), BF16), pltpu.VMEM((m, 2 * SSM_STATE), F32),
                        pltpu.VMEM((m, 2 * SSM_STATE), F32)],
        compiler_params=_cparams(("parallel",), 32),
        name="s5_group",
    )(u_g, h0_g, prm["lrow"], prm["lcol"], prm["bt"], prm["bn"], prm["ct"], prm["dtile"][lp])


def _s5_mixer(u, h0, prm, lp, lreal):
    b, t, _ = u.shape
    nc = t // lreal
    ug = u.reshape(b, nc, lreal, SSM_GROUPS, SSM_CH)
    if lp != lreal:
        ug = jnp.pad(ug, ((0, 0), (0, 0), (0, lp - lreal), (0, 0), (0, 0)))
    ug = ug.transpose(3, 1, 0, 2, 4).reshape(SSM_GROUPS, nc * b, lp * SSM_CH)
    h0g = h0.transpose(1, 0, 3, 2).reshape(SSM_GROUPS, b, 2 * SSM_STATE)
    yg, hl = _s5(ug, h0g, prm, lp, lreal, b, nc)
    y = yg.reshape(SSM_GROUPS, nc, b, lp, SSM_CH)[:, :, :, :lreal].transpose(2, 1, 3, 0, 4)
    h_last = hl.reshape(SSM_GROUPS, b, 2, SSM_STATE).transpose(1, 0, 3, 2)
    return y.reshape(b, t, SSM_WIDTH), h_last


def _cmp_proj_kernel(*refs, n_in, rowwise):
    refs = refs[len(refs) - n_in - 3:]
    x_refs, (w1k_ref, w1v_ref, o_ref) = refs[:n_in], refs[n_in:]

    def rows_of(j, r):
        if rowwise:
            n = x_refs[0].shape[0] // (4 * CMP_STRIDE)
            parts = [x[pl.ds(4 * r + j, n, stride=4 * CMP_STRIDE), :] for x in x_refs]
            return parts[0] if n_in == 1 else jnp.concatenate(parts, axis=0)
        x = x_refs[j]
        return x[0, pl.ds(r, x.shape[1] // CMP_STRIDE, stride=CMP_STRIDE), :]

    for j in range(4):
        c = jnp.concatenate([rows_of(j, r) for r in range(CMP_STRIDE)], axis=1).astype(BF16)
        w = w1k_ref if j % 2 == 0 else w1v_ref
        o_ref[0, :, j * 2 * HEAD_DIM:(j + 1) * 2 * HEAD_DIM] = _bdot(c, w[...])


def _cmp_combine_kernel(h_ref, b1_ref, w2k_ref, w2v_ref, kn_ref, o_ref, *, nchunk):
    h = h_ref[0]
    keep = lax.broadcasted_iota(jnp.int32, (nchunk, 1), 0) < nchunk - 1
    for j in range(4):
        kvh, is_v = j // 2, j % 2
        lo = h[:, j * 2 * HEAD_DIM: j * 2 * HEAD_DIM + HEAD_DIM]
        hi = h[:, j * 2 * HEAD_DIM + HEAD_DIM:(j + 1) * 2 * HEAD_DIM]
        pre = lo + pltpu.roll(hi, nchunk - 1, axis=0) + b1_ref[is_v:is_v + 1, :]
        out = _bdot(_gelu(pre).astype(BF16), (w2v_ref if is_v else w2k_ref)[...])
        if not is_v:
            out = _rms(out, kn_ref[...])
        c0 = is_v * N_KV_HEADS * HEAD_DIM + kvh * HEAD_DIM
        o_ref[0, :, c0:c0 + HEAD_DIM] = jnp.where(keep, out, 0.0)


def _cmp_combine(h, prm):
    b, nchunk, w = h.shape
    return pl.pallas_call(
        functools.partial(_cmp_combine_kernel, nchunk=nchunk),
        grid=(b,),
        in_specs=[pl.BlockSpec((1, nchunk, w), lambda i: (i, 0, 0)), _resident((2, HEAD_DIM)),
                  _resident((HEAD_DIM, HEAD_DIM)), _resident((HEAD_DIM, HEAD_DIM)), _resident((1, HEAD_DIM))],
        out_specs=pl.BlockSpec((1, nchunk, 4 * HEAD_DIM), lambda i: (i, 0, 0)),
        out_shape=jax.ShapeDtypeStruct((b, nchunk, 4 * HEAD_DIM), F32),
        compiler_params=_cparams(("parallel",), 40),
        name="cmp_combine",
    )(h, prm["cmp_b1"], prm["cmp_w2_k"], prm["cmp_w2_v"], prm["k_norm_cmp"])


def _compress_prompt(kv_cmp, prm):
    b, t, w = kv_cmp.shape
    nchunk = t // CMP_STRIDE
    streams = w // HEAD_DIM
    h = pl.pallas_call(
        functools.partial(_cmp_proj_kernel, n_in=streams, rowwise=False),
        grid=(b,),
        in_specs=[pl.BlockSpec((1, t, HEAD_DIM), functools.partial(lambda i, j: (i, 0, j), j=j))
                  for j in range(streams)] + [
                  _resident((CMP_STRIDE * HEAD_DIM, 2 * HEAD_DIM)), _resident((CMP_STRIDE * HEAD_DIM, 2 * HEAD_DIM))],
        out_specs=pl.BlockSpec((1, nchunk, 8 * HEAD_DIM), lambda i: (i, 0, 0)),
        out_shape=jax.ShapeDtypeStruct((b, nchunk, 8 * HEAD_DIM), F32),
        compiler_params=_cparams(("parallel",), 40),
        name="cmp_proj_prompt",
    )(*([kv_cmp] * streams), prm["cmp_w1_k"], prm["cmp_w1_v"])
    return _cmp_combine(h, prm)


PAGES_PER_STEP = 16


def _rowwise_pages(cache):
    return cache.reshape(cache.shape[0], cache.shape[1], PAGE_SIZE * 2 * N_KV_HEADS, HEAD_DIM)


def _compress_sample(cache_cmp, layer, page_table, prm):
    bsz, n_pages = page_table.shape
    chunks_per_page = PAGE_SIZE // CMP_STRIDE
    pages = _rowwise_pages(cache_cmp)
    steps = n_pages // PAGES_PER_STEP

    def page_spec(p):
        return pl.BlockSpec((None, None, 4 * PAGE_SIZE, HEAD_DIM),
                            lambda b, s, pt: (layer, pt[b * n_pages + s * PAGES_PER_STEP + p], 0, 0))

    rows = PAGES_PER_STEP * chunks_per_page
    h = pl.pallas_call(
        functools.partial(_cmp_proj_kernel, n_in=PAGES_PER_STEP, rowwise=True),
        grid_spec=pltpu.PrefetchScalarGridSpec(
            num_scalar_prefetch=1,
            grid=(bsz, steps),
            in_specs=[page_spec(p) for p in range(PAGES_PER_STEP)] + [
                pl.BlockSpec((CMP_STRIDE * HEAD_DIM, 2 * HEAD_DIM), lambda b, s, pt: (0, 0)),
                pl.BlockSpec((CMP_STRIDE * HEAD_DIM, 2 * HEAD_DIM), lambda b, s, pt: (0, 0))],
            out_specs=pl.BlockSpec((1, rows, 8 * HEAD_DIM), lambda b, s, pt: (b, s, 0)),
        ),
        out_shape=jax.ShapeDtypeStruct((bsz, n_pages * chunks_per_page, 8 * HEAD_DIM), F32),
        compiler_params=_cparams(("parallel", "parallel"), 40),
        name="cmp_proj_sample",
    )(page_table.reshape(-1), *([pages] * PAGES_PER_STEP), prm["cmp_w1_k"], prm["cmp_w1_v"])
    return _cmp_combine(h, prm)


def _cmp_select_kernel(q_ref, ck_ref, cv_ref, o_ref, rank_ref, *idx_ref, tq, nc, nbp, nblk, qpos0):
    kvh = pl.program_id(1)
    qpos = qpos0 + pl.program_id(2) * tq + lax.broadcasted_iota(jnp.int32, (tq, 1), 0)
    start = lax.broadcasted_iota(jnp.int32, (1, nc), 1) * CMP_STRIDE
    valid = (start + (CMP_LEN - 1)) <= qpos
    dist = (qpos - start).astype(F32) - (CMP_LEN - 1) / 2
    ck = ck_ref[0].astype(BF16)
    cv = cv_ref[0].astype(BF16)
    psum = jnp.zeros((tq, nc), F32)
    for g in range(GQA):
        sl = slice(g * HEAD_DIM, (g + 1) * HEAD_DIM)
        s = _bdot_nt(q_ref[0, :, sl], ck) * SCALE - _head_slope(kvh, g) * dist
        _, e, den = _softmax_parts(s, valid)
        p = e / jnp.maximum(den, F32_TINY)
        o_ref[0, :, sl] = _bdot(p.astype(BF16), cv)
        psum = psum + p

    per_shift = (SEL_LEN // CMP_STRIDE).bit_length() - 1
    ci = lax.broadcasted_iota(jnp.int32, (nc, nbp), 0)
    bj = lax.broadcasted_iota(jnp.int32, (nc, nbp), 1)
    overlap = (jnp.where(ci >> per_shift == bj, 1.0, 0.0)
               + jnp.where((ci + 1) >> per_shift == bj, 1.0, 0.0)).astype(BF16)
    score = _dot_small_int(psum, overlap)

    j = lax.broadcasted_iota(jnp.int32, (1, nbp), 1)
    qb = qpos >> SEL_SHIFT
    forced = (j == 0) | (j == qb) | (j == qb - 1)
    causal = (j * SEL_LEN) <= qpos
    score = jnp.where(forced, FORCED_SCORE, jnp.where(causal, score, -1.0))
    score = jnp.where(j < nblk, score, -2.0)

    rank = jnp.zeros((tq, nbp), F32)
    for i in range(nblk):
        si = score[:, i:i + 1]
        rank = rank + jnp.where(j > i, jnp.where(si >= score, 1.0, 0.0), jnp.where(si > score, 1.0, 0.0))
    rank_ref[0, 0] = rank

    if idx_ref:
        jf = j.astype(F32)
        lane = lax.broadcasted_iota(jnp.int32, (tq, V7X_LANES), 1)
        idx = jnp.zeros((tq, V7X_LANES), F32)
        for r in range(N_SEL):
            col = jnp.sum(jnp.where(rank == float(r), jf, 0.0), axis=-1, keepdims=True)
            idx = jnp.where(lane == r, col, idx)
        idx_ref[0][0, 0] = idx.astype(jnp.int32)


def _cmp_select(q, ckv, tq, nblk, qpos0, want_idx):
    b, t, _ = q.shape
    nc = ckv.shape[1]
    nbp = -(-nblk // V7X_LANES) * V7X_LANES
    hw = GQA * HEAD_DIM
    out_specs = [pl.BlockSpec((1, tq, hw), lambda bi, k, i: (bi, i, k)),
                 pl.BlockSpec((1, 1, tq, nbp), lambda bi, k, i: (bi, k, i, 0))]
    out_shape = [jax.ShapeDtypeStruct((b, t, ATTN_WIDTH), F32),
                 jax.ShapeDtypeStruct((b, N_KV_HEADS, t, nbp), F32)]
    if want_idx:
        out_specs.append(pl.BlockSpec((1, 1, tq, V7X_LANES), lambda bi, k, i: (bi, k, i, 0)))
        out_shape.append(jax.ShapeDtypeStruct((b, N_KV_HEADS, t, V7X_LANES), jnp.int32))
    return pl.pallas_call(
        functools.partial(_cmp_select_kernel, tq=tq, nc=nc, nbp=nbp, nblk=nblk, qpos0=qpos0),
        grid=(b, N_KV_HEADS, t // tq),
        in_specs=[pl.BlockSpec((1, tq, hw), lambda bi, k, i: (bi, i, k)),
                  pl.BlockSpec((1, nc, HEAD_DIM), lambda bi, k, i: (bi, 0, k)),
                  pl.BlockSpec((1, nc, HEAD_DIM), lambda bi, k, i: (bi, 0, N_KV_HEADS + k))],
        out_specs=out_specs,
        out_shape=out_shape,
        compiler_params=_cparams(("parallel", "parallel", "parallel"), 40),
        name="cmp_select",
    )(q, ckv, ckv)


SEL_TQ = 128
SEL_TK = 512


def _sel_prompt_kernel(q_ref, k_ref, v_ref, rank_ref, o_ref, m_ref, l_ref, acc_ref):
    kvh = pl.program_id(1)
    q0 = pl.program_id(2) * SEL_TQ
    qpos = q0 + lax.broadcasted_iota(jnp.int32, (SEL_TQ, 1), 0)
    chosen = jnp.where(rank_ref[0, 0] < float(N_SEL), 1.0, 0.0).astype(BF16)
    nbp = chosen.shape[1]
    m_ref[...] = jnp.full(m_ref.shape, NEG_INF, F32)
    l_ref[...] = jnp.zeros(l_ref.shape, F32)
    acc_ref[...] = jnp.zeros(acc_ref.shape, F32)

    def tile(jt, carry):
        k0 = pl.multiple_of(jt * SEL_TK, SEL_TK)
        kb = k_ref[0, pl.ds(k0, SEL_TK), :].astype(BF16)
        vb = v_ref[0, pl.ds(k0, SEL_TK), :].astype(BF16)
        kpos = k0 + lax.broadcasted_iota(jnp.int32, (1, SEL_TK), 1)
        blk_of_key = (k0 + lax.broadcasted_iota(jnp.int32, (nbp, SEL_TK), 1)) >> SEL_SHIFT
        expand = jnp.where(blk_of_key == lax.broadcasted_iota(jnp.int32, (nbp, SEL_TK), 0), 1.0, 0.0).astype(BF16)
        valid = (_bdot(chosen, expand) > 0.5) & (kpos <= qpos)
        dist = (qpos - kpos).astype(F32)
        for g in range(GQA):
            s = _bdot_nt(q_ref[0, :, g * HEAD_DIM:(g + 1) * HEAD_DIM], kb) * SCALE - _head_slope(kvh, g) * dist
            _online_update(m_ref, l_ref, acc_ref, g, s, valid, vb)
        return carry

    lax.fori_loop(0, (q0 + SEL_TQ + SEL_TK - 1) // SEL_TK, tile, 0)
    for g in range(GQA):
        o_ref[0, :, g * HEAD_DIM:(g + 1) * HEAD_DIM] = acc_ref[g] / jnp.maximum(l_ref[g], F32_TINY)


def _sel_prompt(q, kv_slc, rank):
    b, t, _ = q.shape
    nbp = rank.shape[-1]
    hw = GQA * HEAD_DIM
    return pl.pallas_call(
        _sel_prompt_kernel,
        grid=(b, N_KV_HEADS, t // SEL_TQ),
        in_specs=[pl.BlockSpec((1, SEL_TQ, hw), lambda bi, k, i: (bi, i, k)),
                  pl.BlockSpec((1, t, HEAD_DIM), lambda bi, k, i: (bi, 0, 2 * k)),
                  pl.BlockSpec((1, t, HEAD_DIM), lambda bi, k, i: (bi, 0, 2 * k + 1)),
                  pl.BlockSpec((1, 1, SEL_TQ, nbp), lambda bi, k, i: (bi, k, i, 0))],
        out_specs=pl.BlockSpec((1, SEL_TQ, hw), lambda bi, k, i: (bi, i, k)),
        out_shape=jax.ShapeDtypeStruct((b, t, ATTN_WIDTH), F32),
        scratch_shapes=[pltpu.VMEM((GQA, SEL_TQ, 1), F32), pltpu.VMEM((GQA, SEL_TQ, 1), F32),
                        pltpu.VMEM((GQA, SEL_TQ, HEAD_DIM), F32)],
        compiler_params=_cparams(("parallel", "parallel", "parallel"), 32),
        name="sel_prompt",
    )(q, kv_slc, kv_slc, rank)


def _win_prompt_kernel(q_ref, k_ref, v_ref, o_ref):
    kvh = pl.program_id(1)
    i = pl.program_id(2)
    span = WINDOW + WIN_QBLK
    qpos = i * WIN_QBLK + lax.broadcasted_iota(jnp.int32, (WIN_QBLK, 1), 0)
    k0 = pl.multiple_of(jnp.maximum(i * WIN_QBLK - WINDOW, 0), WIN_QBLK)
    kb = k_ref[0, pl.ds(k0, span), :].astype(BF16)
    vb = v_ref[0, pl.ds(k0, span), :].astype(BF16)
    d = qpos - (k0 + lax.broadcasted_iota(jnp.int32, (1, span), 1))
    valid = (d >= 0) & (d < WINDOW)
    dist = d.astype(F32)
    for g in range(GQA):
        sl = slice(g * HEAD_DIM, (g + 1) * HEAD_DIM)
        s = _bdot_nt(q_ref[0, :, sl], kb) * SCALE - _head_slope(kvh, g) * dist
        _, e, den = _softmax_parts(s, valid)
        o_ref[0, :, sl] = _bdot(e.astype(BF16), vb) / jnp.maximum(den, F32_TINY)


def _win_prompt(q, kv_win):
    b, t, _ = q.shape
    hw = GQA * HEAD_DIM
    return pl.pallas_call(
        _win_prompt_kernel,
        grid=(b, N_KV_HEADS, t // WIN_QBLK),
        in_specs=[pl.BlockSpec((1, WIN_QBLK, hw), lambda bi, k, i: (bi, i, k)),
                  pl.BlockSpec((1, t, HEAD_DIM), lambda bi, k, i: (bi, 0, 2 * k)),
                  pl.BlockSpec((1, t, HEAD_DIM), lambda bi, k, i: (bi, 0, 2 * k + 1))],
        out_specs=pl.BlockSpec((1, WIN_QBLK, hw), lambda bi, k, i: (bi, i, k)),
        out_shape=jax.ShapeDtypeStruct((b, t, ATTN_WIDTH), F32),
        compiler_params=_cparams(("parallel", "parallel", "parallel"), 32),
        name="win_prompt",
    )(q, kv_win, kv_win)


def _slope_col(kvh):
    g = lax.broadcasted_iota(jnp.int32, (GQA, 1), 0)
    col = jnp.zeros((GQA, 1), F32)
    for i in range(GQA):
        col = jnp.where(g == i, _alibi_slope(kvh, i), col)
    return col


def _stream_rows(ref, lead, stream, n):
    return ref[lead + (pl.ds(stream, n, stride=2 * N_KV_HEADS), slice(None))]


def _sel_sample_kernel(idx_ref, pt_ref, q_ref, *refs, t_len):
    past, new_ref, o_ref = refs[:N_KV_HEADS * N_SEL], refs[N_KV_HEADS * N_SEL], refs[N_KV_HEADS * N_SEL + 1]
    b, t = pl.program_id(0), pl.program_id(1)
    past_blocks = PAST_LEN // SEL_LEN
    lane = lax.broadcasted_iota(jnp.int32, (1, N_SEL * SEL_LEN), 1)
    for kvh in range(N_KV_HEADS):
        ks, vs = [], []
        base = jnp.zeros((1, N_SEL * SEL_LEN), jnp.int32)
        for r in range(N_SEL):
            blk = idx_ref[((b * N_KV_HEADS + kvh) * t_len + t) * N_SEL + r]
            is_new = blk >= past_blocks
            ref = past[kvh * N_SEL + r]
            ks.append(jnp.where(is_new, _stream_rows(new_ref, (0,), 2 * kvh, SEL_LEN),
                                _stream_rows(ref, (), 2 * kvh, SEL_LEN)).astype(BF16))
            vs.append(jnp.where(is_new, _stream_rows(new_ref, (0,), 2 * kvh + 1, SEL_LEN),
                                _stream_rows(ref, (), 2 * kvh + 1, SEL_LEN)).astype(BF16))
            base = jnp.where((lane >> SEL_SHIFT) == r, blk * SEL_LEN, base)
        d = (PAST_LEN + t) - (base + (lane & (SEL_LEN - 1)))
        rows = slice(kvh * GQA, (kvh + 1) * GQA)
        s = _bdot_nt(q_ref[0, 0, rows, :], jnp.concatenate(ks, axis=0)) * SCALE - _slope_col(kvh) * d.astype(F32)
        _, e, z = _softmax_parts(s, d >= 0)
        o_ref[0, 0, rows, :] = _bdot(e.astype(BF16), jnp.concatenate(vs, axis=0)) / jnp.maximum(z, F32_TINY)


def _sel_sample(q_h, cache_slc, layer, page_table, idx, kv_new_rows):
    bsz, t_len = q_h.shape[:2]
    n_pages = page_table.shape[1]
    per_page = PAGE_SIZE // SEL_LEN
    past_blocks = PAST_LEN // SEL_LEN
    pages = _rowwise_pages(cache_slc)
    blk_rows = SEL_LEN * 2 * N_KV_HEADS

    def past_spec(kvh, r):
        def imap(b, t, idx_ref, pt_ref):
            blk = jnp.minimum(idx_ref[((b * N_KV_HEADS + kvh) * t_len + t) * N_SEL + r], past_blocks - 1)
            return (layer, pt_ref[b * n_pages + blk // per_page], blk % per_page, 0)
        return pl.BlockSpec((None, None, blk_rows, HEAD_DIM), imap)

    qspec = pl.BlockSpec((1, 1, N_HEADS, HEAD_DIM), lambda b, t, i_, p_: (b, t, 0, 0))
    return pl.pallas_call(
        functools.partial(_sel_sample_kernel, t_len=t_len),
        grid_spec=pltpu.PrefetchScalarGridSpec(
            num_scalar_prefetch=2,
            grid=(bsz, t_len),
            in_specs=[qspec] + [past_spec(kvh, r) for kvh in range(N_KV_HEADS) for r in range(N_SEL)]
            + [pl.BlockSpec((1, blk_rows, HEAD_DIM), lambda b, t, i_, p_: (b, 0, 0))],
            out_specs=qspec,
        ),
        out_shape=jax.ShapeDtypeStruct(q_h.shape, F32),
        compiler_params=_cparams(("parallel", "parallel"), 32),
        name="sel_sample",
    )(idx.reshape(-1), page_table.reshape(-1), q_h, *([pages] * (N_KV_HEADS * N_SEL)), kv_new_rows)


def _win_sample_kernel(q_ref, past_ref, new_ref, o_ref, *, t_len, t_pad, wbuf):
    t = lax.broadcasted_iota(jnp.int32, (t_len, 1), 0)
    d_past = wbuf + t - lax.broadcasted_iota(jnp.int32, (1, wbuf), 1)
    row_new = lax.broadcasted_iota(jnp.int32, (1, t_pad), 1)
    d_new = t - row_new
    valid_past = (d_past >= 0) & (d_past < WINDOW)
    valid_new = (d_new >= 0) & (d_new < WINDOW) & (row_new < t_len)
    for kvh in range(N_KV_HEADS):
        kp = _stream_rows(past_ref, (0,), 2 * kvh, wbuf).astype(BF16)
        vp = _stream_rows(past_ref, (0,), 2 * kvh + 1, wbuf).astype(BF16)
        kn = _stream_rows(new_ref, (0,), 2 * kvh, t_pad).astype(BF16)
        vn = _stream_rows(new_ref, (0,), 2 * kvh + 1, t_pad).astype(BF16)
        for g in range(GQA):
            q = q_ref[0, kvh, g]
            slope = _alibi_slope(kvh, g)
            m1, e1, z1 = _softmax_parts(_bdot_nt(q, kp) * SCALE - slope * d_past.astype(F32), valid_past)
            m2, e2, z2 = _softmax_parts(_bdot_nt(q, kn) * SCALE - slope * d_new.astype(F32), valid_new)
            m = jnp.maximum(m1, m2)
            a1 = jnp.where(z1 > 0.0, jnp.exp(m1 - m), 0.0)
            a2 = jnp.where(z2 > 0.0, jnp.exp(m2 - m), 0.0)
            num = a1 * _bdot(e1.astype(BF16), vp) + a2 * _bdot(e2.astype(BF16), vn)
            o_ref[0, kvh, g] = num / jnp.maximum(a1 * z1 + a2 * z2, F32_TINY)


def _win_sample(q_g, win_rows, new_rows):
    bsz, _, _, t_len, _ = q_g.shape
    streams = 2 * N_KV_HEADS
    wbuf, t_pad = win_rows.shape[1] // streams, new_rows.shape[1] // streams
    qspec = pl.BlockSpec((1, N_KV_HEADS, GQA, t_len, HEAD_DIM), lambda b: (b, 0, 0, 0, 0))
    return pl.pallas_call(
        functools.partial(_win_sample_kernel, t_len=t_len, t_pad=t_pad, wbuf=wbuf),
        grid=(bsz,),
        in_specs=[qspec, pl.BlockSpec((1, wbuf * streams, HEAD_DIM), lambda b: (b, 0, 0)),
                  pl.BlockSpec((1, t_pad * streams, HEAD_DIM), lambda b: (b, 0, 0))],
        out_specs=qspec,
        out_shape=jax.ShapeDtypeStruct(q_g.shape, F32),
        compiler_params=_cparams(("parallel",), 16),
        name="win_sample",
    )(q_g, win_rows, new_rows)


def _mix_kernel(x_ref, y_ref, oc_ref, os_ref, ow_ref, gate_ref, wglu_ref, gs_ref, ga_ref, wout_ref, o_ref):
    z = _gelu(y_ref[...])
    z = z * jax.nn.sigmoid(_bdot(z.astype(BF16), wglu_ref[...]))
    gate = gate_ref[...]
    cols = []
    for h in range(N_HEADS):
        sl = slice(h * HEAD_DIM, (h + 1) * HEAD_DIM)
        cols.append(gate[:, h:h + 1] * oc_ref[:, sl] + gate[:, N_HEADS + h:N_HEADS + h + 1] * os_ref[:, sl]
                    + gate[:, 2 * N_HEADS + h:2 * N_HEADS + h + 1] * ow_ref[:, sl])
    attn = jnp.concatenate(cols, axis=1)
    hs = _rms(z, gs_ref[...]).astype(BF16)
    ha = _rms(attn, ga_ref[...]).astype(BF16)
    o_ref[...] = x_ref[...] + (_bdot(hs, wout_ref[:SSM_WIDTH, :]) + _bdot(ha, wout_ref[SSM_WIDTH:, :]))


def _mix(x, y, oc, osl, ow, gates, prm, tm):
    m = x.shape[0]
    row = lambda n: pl.BlockSpec((tm, n), lambda i: (i, 0))
    return pl.pallas_call(
        _mix_kernel,
        grid=(m // tm,),
        in_specs=[row(D_MODEL), row(SSM_WIDTH), row(ATTN_WIDTH), row(ATTN_WIDTH), row(ATTN_WIDTH), row(GATE_PAD),
                  _resident((SSM_WIDTH, SSM_WIDTH)), _resident((1, SSM_WIDTH)), _resident((1, ATTN_WIDTH)),
                  _resident((D_MODEL, D_MODEL))],
        out_specs=row(D_MODEL),
        out_shape=jax.ShapeDtypeStruct((m, D_MODEL), F32),
        compiler_params=_cparams(("parallel",), 48),
        name="mix_outproj",
    )(x, y, oc, osl, ow, gates, prm["ssm_w_glu"], prm["norm_ssm_out"], prm["norm_attn_out"], prm["w_out"])


FFN_TF = 512


def _ffn_kernel(x_ref, g_ref, wg_ref, wu_ref, wd_ref, o_ref, xn_ref, acc_ref):
    j = pl.program_id(1)

    @pl.when(j == 0)
    def _():
        xn_ref[...] = _rms(x_ref[...], g_ref[...]).astype(BF16)
        acc_ref[...] = jnp.zeros(acc_ref.shape, F32)

    xn = xn_ref[...]
    a = _bdot(xn, wg_ref[...])
    h = (a * jax.nn.sigmoid(a)) * _bdot(xn, wu_ref[...])
    acc_ref[...] += _bdot(h.astype(BF16), wd_ref[...])

    @pl.when(j == pl.num_programs(1) - 1)
    def _():
        o_ref[...] = x_ref[...] + acc_ref[...]


def _ffn(x, prm, tm):
    m = x.shape[0]
    return pl.pallas_call(
        _ffn_kernel,
        grid=(m // tm, D_FF // FFN_TF),
        in_specs=[pl.BlockSpec((tm, D_MODEL), lambda i, j: (i, 0)),
                  pl.BlockSpec((1, D_MODEL), lambda i, j: (0, 0)),
                  pl.BlockSpec((D_MODEL, FFN_TF), lambda i, j: (0, j)),
                  pl.BlockSpec((D_MODEL, FFN_TF), lambda i, j: (0, j)),
                  pl.BlockSpec((FFN_TF, D_MODEL), lambda i, j: (j, 0))],
        out_specs=pl.BlockSpec((tm, D_MODEL), lambda i, j: (i, 0)),
        out_shape=jax.ShapeDtypeStruct((m, D_MODEL), F32),
        scratch_shapes=[pltpu.VMEM((tm, D_MODEL), BF16), pltpu.VMEM((tm, D_MODEL), F32)],
        compiler_params=_cparams(("parallel", "arbitrary"), 48),
        name="ffn",
    )(x, prm["norm_ffn"], prm["w_ffn_gate"], prm["w_ffn_up"], prm["w_ffn_down"])


S5_CHUNK = 64
SAMPLE_S5_PAD = 8


def _layer_params(l, w):
    kvh = jnp.arange(N_KV_HEADS)[:, None] * HEAD_DIM + jnp.arange(HEAD_DIM)[None, :]
    base = D_MODEL
    cols = [jnp.arange(D_MODEL)]
    for br in range(3):
        k0, v0 = base + 2 * br * KV_WIDTH, base + (2 * br + 1) * KV_WIDTH
        cols.append(jnp.stack([k0 + kvh, v0 + kvh], axis=1).reshape(-1))
    g0 = base + 6 * KV_WIDTH
    cols.append((g0 + jnp.arange(N_HEADS)[None, :] * 3 + jnp.arange(3)[:, None]).reshape(-1))
    w_in = jnp.take(w["w_in"][l], jnp.concatenate(cols), axis=1)
    w_in = jnp.pad(w_in, ((0, 0), (0, IN_WIDTH_PAD - w_in.shape[1]))).astype(BF16)

    def w1cat(w1):
        half = CMP_STRIDE * HEAD_DIM
        return jnp.concatenate([w1[:half], w1[half:]], axis=1).astype(BF16)

    row = lambda v: v.reshape(1, -1)
    a_re, a_im = w["ssm_a_re"][l], w["ssm_a_im"][l]
    log_dt = jnp.broadcast_to(w["ssm_log_dt"][l][:, None], a_re.shape)
    lrow = jnp.stack([a_re, a_im, log_dt], axis=1)
    d = w["ssm_d"][l]
    return {
        "w_in": w_in, "norm_mix": row(w["norm_mix"][l]),
        "q_norm": row(w["q_norm"][l]), "k_norm_slc": row(w["k_norm_slc"][l]),
        "k_norm_win": row(w["k_norm_win"][l]), "k_norm_cmp": row(w["k_norm_cmp"][l]),
        "lrow": lrow, "lcol": lrow.transpose(0, 2, 1),
        "bt": jnp.stack([w["ssm_b_re"][l], w["ssm_b_im"][l]], axis=1).transpose(0, 1, 3, 2),
        "bn": jnp.stack([w["ssm_b_re"][l], w["ssm_b_im"][l]], axis=1),
        "ct": jnp.stack([w["ssm_c_re"][l], w["ssm_c_im"][l]], axis=1).transpose(0, 1, 3, 2),
        "dtile": {lp: jnp.tile(d, (1, lp))[:, None, :] for lp in (S5_CHUNK, SAMPLE_S5_PAD)},
        "ssm_w_glu": w["ssm_w_glu"][l].astype(BF16),
        "cmp_w1_k": w1cat(w["cmp_w1_k"][l]), "cmp_w1_v": w1cat(w["cmp_w1_v"][l]),
        "cmp_b1": jnp.stack([w["cmp_b1_k"][l], w["cmp_b1_v"][l]]),
        "cmp_w2_k": w["cmp_w2_k"][l].astype(BF16), "cmp_w2_v": w["cmp_w2_v"][l].astype(BF16),
        "norm_ssm_out": row(w["norm_ssm_out"][l]), "norm_attn_out": row(w["norm_attn_out"][l]),
        "w_out": w["w_out"][l].astype(BF16), "norm_ffn": row(w["norm_ffn"][l]),
        "w_ffn_gate": w["w_ffn_gate"][l].astype(BF16), "w_ffn_up": w["w_ffn_up"][l].astype(BF16),
        "w_ffn_down": w["w_ffn_down"][l].astype(BF16),
    }


def _kv_out(kv, b, t):
    return kv.reshape(b, t, N_KV_HEADS, 2, HEAD_DIM)


PROMPT_TM = 256
FFN_TM = 512


def _prompt_layer(x, prm):
    b, t, _ = x.shape
    m = b * t
    x2 = x.reshape(m, D_MODEL)
    u, q, kv_cmp, kv_slc, kv_win, gates = _inproj(x2, prm["norm_mix"], prm["w_in"], prm["q_norm"],
                                                  prm["k_norm_slc"], prm["k_norm_win"], PROMPT_TM)
    h0 = jnp.zeros((b, SSM_GROUPS, SSM_STATE, 2), F32)
    y, h_last = _s5_mixer(u.reshape(b, t, SSM_WIDTH), h0, prm, S5_CHUNK, S5_CHUNK)
    q3 = q.reshape(b, t, ATTN_WIDTH)
    kv_cmp3, kv_slc3, kv_win3 = (a.reshape(b, t, 2 * KV_WIDTH) for a in (kv_cmp, kv_slc, kv_win))
    ckv = _compress_prompt(kv_cmp3, prm)
    o_cmp, rank = _cmp_select(q3, ckv, 256, t // SEL_LEN, 0, False)
    o_slc = _sel_prompt(q3, kv_slc3, rank)
    o_win = _win_prompt(q3, kv_win3)
    x2 = _mix(x2, y.reshape(m, SSM_WIDTH), o_cmp.reshape(m, -1), o_slc.reshape(m, -1), o_win.reshape(m, -1),
              gates, prm, PROMPT_TM)
    x2 = _ffn(x2, prm, FFN_TM)
    wbuf = min(WINDOW, PAST_LEN)
    win_state = jnp.pad(kv_win3, ((0, 0), (max(wbuf - t, 0), 0), (0, 0)))[:, -wbuf:]
    return (x2.reshape(b, t, D_MODEL), _kv_out(kv_cmp3, b, t), _kv_out(kv_slc3, b, t),
            _kv_out(win_state, b, wbuf), h_last)


def _sample_layer(x, prm, cache_cmp, cache_slc, layer, page_table, win_buf, h0):
    b, t, _ = x.shape
    m = b * t
    assert (PAST_LEN + t) // CMP_STRIDE == PAST_LEN // CMP_STRIDE and PAST_LEN % PAGE_SIZE == 0
    x2 = x.reshape(m, D_MODEL)
    u, q, kv_cmp, kv_slc, kv_win, gates = _inproj(x2, prm["norm_mix"], prm["w_in"], prm["q_norm"],
                                                  prm["k_norm_slc"], prm["k_norm_win"], m)
    y, h_last = _s5_mixer(u.reshape(b, t, SSM_WIDTH), h0, prm, SAMPLE_S5_PAD, t)
    q3 = q.reshape(b, t, ATTN_WIDTH)
    kv_cmp3, kv_slc3, kv_win3 = (a.reshape(b, t, 2 * KV_WIDTH) for a in (kv_cmp, kv_slc, kv_win))
    ckv = _compress_sample(cache_cmp, layer, page_table, prm)
    n_blocks = -(-(PAST_LEN + t) // SEL_LEN)
    o_cmp, _, idx = _cmp_select(q3, ckv, t, n_blocks, PAST_LEN, True)
    streams = 2 * N_KV_HEADS

    def stream_rows(kv3, t_pad):
        return jnp.pad(kv3.reshape(b, t * streams, HEAD_DIM), ((0, 0), (0, (t_pad - t) * streams), (0, 0)))

    o_slc = _sel_sample(q3.reshape(b, t, N_HEADS, HEAD_DIM), cache_slc, layer, page_table, idx[..., :N_SEL],
                        stream_rows(kv_slc3, SEL_LEN)).reshape(m, ATTN_WIDTH)
    wbuf = win_buf.shape[1]
    q5 = q3.reshape(b, t, N_KV_HEADS, GQA, HEAD_DIM)
    o_win = _win_sample(q5.transpose(0, 2, 3, 1, 4), win_buf.reshape(b, wbuf * streams, HEAD_DIM),
                        stream_rows(kv_win3, 8))
    o_win = o_win.transpose(0, 3, 1, 2, 4).reshape(m, ATTN_WIDTH)
    x2 = _mix(x2, y.reshape(m, SSM_WIDTH), o_cmp.reshape(m, -1), o_slc, o_win, gates, prm, m)
    x2 = _ffn(x2, prm, m)
    win_state = jnp.concatenate([win_buf, _kv_out(kv_win3, b, t)], axis=1)[:, -wbuf:]
    return (x2.reshape(b, t, D_MODEL), _kv_out(kv_cmp3, b, t), _kv_out(kv_slc3, b, t), win_state, h_last)


def kernel(x_prompt, x_sample, cache_cmp_kv, cache_slc_kv, state_win_kv, state_ssm, page_table, norm_mix, w_in, ssm_a_re, ssm_a_im, ssm_log_dt, ssm_b_re, ssm_b_im, ssm_c_re, ssm_c_im, ssm_d, ssm_w_glu, q_norm, k_norm_cmp, k_norm_slc, k_norm_win, cmp_w1_k, cmp_b1_k, cmp_w2_k, cmp_w1_v, cmp_b1_v, cmp_w2_v, norm_ssm_out, norm_attn_out, w_out, norm_ffn, w_ffn_gate, w_ffn_up, w_ffn_down):
    w = dict(norm_mix=norm_mix, w_in=w_in, ssm_a_re=ssm_a_re, ssm_a_im=ssm_a_im, ssm_log_dt=ssm_log_dt,
             ssm_b_re=ssm_b_re, ssm_b_im=ssm_b_im, ssm_c_re=ssm_c_re, ssm_c_im=ssm_c_im, ssm_d=ssm_d,
             ssm_w_glu=ssm_w_glu, q_norm=q_norm, k_norm_cmp=k_norm_cmp, k_norm_slc=k_norm_slc,
             k_norm_win=k_norm_win, cmp_w1_k=cmp_w1_k, cmp_b1_k=cmp_b1_k, cmp_w2_k=cmp_w2_k,
             cmp_w1_v=cmp_w1_v, cmp_b1_v=cmp_b1_v, cmp_w2_v=cmp_w2_v, norm_ssm_out=norm_ssm_out,
             norm_attn_out=norm_attn_out, w_out=w_out, norm_ffn=norm_ffn, w_ffn_gate=w_ffn_gate,
             w_ffn_up=w_ffn_up, w_ffn_down=w_ffn_down)
    y_p, y_s = x_prompt, x_sample
    outs_p, outs_s = [], []
    for l in range(DEPTH):
        prm = _layer_params(l, w)
        y_p, *rest = _prompt_layer(y_p, prm)
        outs_p.append(rest)
        y_s, *rest = _sample_layer(y_s, prm, cache_cmp_kv, cache_slc_kv, l, page_table,
                                   state_win_kv[l], state_ssm[l])
        outs_s.append(rest)
    stack = lambda outs, i: jnp.stack([o[i] for o in outs])
    return (y_p, y_s, stack(outs_p, 0), stack(outs_p, 1), stack(outs_p, 2), stack(outs_p, 3),
            stack(outs_s, 0), stack(outs_s, 1), stack(outs_s, 2), stack(outs_s, 3))
```

```python
import functools
import math

import jax
import jax.numpy as jnp
import numpy as np
from jax import lax
from jax.experimental import pallas as pl
from jax.experimental.pallas import tpu as pltpu

F32 = jnp.float32
BF16 = jnp.bfloat16

D_MODEL = 2048
DEPTH = 2
PAST_LEN = 16384
PAGE_SIZE = 128
SSM_WIDTH = 1024
ATTN_WIDTH = 1024
SSM_CH = 16
SSM_GROUPS = 64
SSM_STATE = 64
HEAD_DIM = 128
N_HEADS = 8
N_KV_HEADS = 2
GQA = 4
KV_WIDTH = 256
CMP_LEN = 32
CMP_STRIDE = 16
SEL_LEN = 64
SEL_SHIFT = SEL_LEN.bit_length() - 1
N_SEL = 16
WINDOW = 512
WIN_QBLK = 128
FORCED_SCORE = 1e4
D_FF = 5632
NORM_EPS = 1e-6
LOG2E = math.log2(math.e)
QSCALE = HEAD_DIM ** -0.5 * LOG2E
GATE_PAD = 128
IN_WIDTH_PAD = SSM_WIDTH + ATTN_WIDTH + 6 * KV_WIDTH + GATE_PAD
F32_TINY = float(jnp.finfo(jnp.float32).tiny)
NEG_INF = float("-inf")

V7X_VMEM_BYTES = 64 * 1024 * 1024
V7X_LANES = 128


def _cparams(semantics, vmem_mib):
    assert vmem_mib * 1024 * 1024 < V7X_VMEM_BYTES
    return pltpu.CompilerParams(dimension_semantics=semantics, vmem_limit_bytes=vmem_mib * 1024 * 1024)


def _resident(shape):
    nd = len(shape)
    return pl.BlockSpec(shape, lambda *_: (0,) * nd, pipeline_mode=pl.Buffered(1))


def _rms(x, g):
    return x * lax.rsqrt(jnp.mean(x * x, axis=-1, keepdims=True) + NORM_EPS) * g


def _gelu(x):
    c = math.sqrt(2.0 / math.pi)
    return x * (0.5 * (1.0 + jnp.tanh(c * (x + 0.044715 * (x * x * x)))))


def _bdot(a, b):
    return jnp.dot(a, b, preferred_element_type=F32)


def _bdot_nt(a, b):
    return lax.dot_general(a, b, (((1,), (1,)), ((), ())), preferred_element_type=F32)


def _split3(x):
    hi = x.astype(BF16)
    r1 = x - hi.astype(F32)
    mid = r1.astype(BF16)
    lo = (r1 - mid.astype(F32)).astype(BF16)
    return hi, mid, lo


def _dot_small_int(x, e):
    hi, mid, lo = _split3(x)
    return _bdot(hi, e) + _bdot(mid, e) + _bdot(lo, e)


def _dot_f32(a, b):
    ah, am, al = _split3(a)
    bh, bm, bl = _split3(b)
    return (_bdot(ah, bh) + (_bdot(ah, bm) + _bdot(am, bh))
            + (_bdot(ah, bl) + _bdot(al, bh) + _bdot(am, bm)))


def _softmax_parts(s, valid):
    s = jnp.where(valid, s, NEG_INF)
    m = jnp.max(s, axis=-1, keepdims=True)
    m = jnp.where(m == NEG_INF, 0.0, m)
    e = jnp.exp2(s - m)
    return m, e, jnp.sum(e, axis=-1, keepdims=True)


def _flash_step(m_ref, acc_ref, g, s, v_ones):
    m_old = m_ref[g]
    m_new = jnp.maximum(m_old, jnp.max(s, axis=-1, keepdims=True))
    m_safe = jnp.where(m_new == NEG_INF, 0.0, m_new)
    acc_ref[g] = jnp.exp2(m_old - m_safe) * acc_ref[g] + _bdot(jnp.exp2(s - m_safe).astype(BF16), v_ones)
    m_ref[g] = m_new


def _alibi_slope(kvh, g):
    return LOG2E * 2.0 ** (-8.0 * (kvh * GQA + g + 1) / N_HEADS)


def _bf16_terms(x, n=3):
    out = []
    for _ in range(n):
        hi = float(np.asarray(x, dtype=np.float32).astype(BF16))
        out.append(hi)
        x -= hi
    return out


def _head_slope(kvh, g):
    assert N_KV_HEADS == 2
    return jnp.where(kvh == 0, _alibi_slope(0, g), _alibi_slope(1, g))


def _inproj_kernel(x_ref, gmix_ref, w_ref, qn_ref, ksn_ref, kwn_ref,
                   u_ref, q_ref, cmp_ref, slc_ref, win_ref, gate_ref):
    xn = _rms(x_ref[...], gmix_ref[...]).astype(BF16)

    def proj(c0, c1):
        return _bdot(xn, w_ref[:, c0:c1])

    u_ref[...] = proj(0, SSM_WIDTH)
    q = proj(SSM_WIDTH, D_MODEL)
    for h in range(N_HEADS):
        sl = slice(h * HEAD_DIM, (h + 1) * HEAD_DIM)
        q_ref[:, sl] = (_rms(q[:, sl], qn_ref[...]) * QSCALE).astype(BF16)
    c0 = D_MODEL
    cmp_ref[...] = proj(c0, c0 + 2 * KV_WIDTH)
    for ref, nref in ((slc_ref, ksn_ref), (win_ref, kwn_ref)):
        c0 += 2 * KV_WIDTH
        kv = proj(c0, c0 + 2 * KV_WIDTH)
        for kvh in range(N_KV_HEADS):
            ks = slice(kvh * 2 * HEAD_DIM, kvh * 2 * HEAD_DIM + HEAD_DIM)
            vs = slice(kvh * 2 * HEAD_DIM + HEAD_DIM, (kvh + 1) * 2 * HEAD_DIM)
            ref[:, ks] = _rms(kv[:, ks], nref[...])
            ref[:, vs] = kv[:, vs]
    c0 += 2 * KV_WIDTH
    gate_ref[...] = jax.nn.sigmoid(proj(c0, c0 + GATE_PAD))


def _inproj(x, gmix, w, qn, ksn, kwn, tm):
    m = x.shape[0]
    row = lambda n: pl.BlockSpec((tm, n), lambda i: (i, 0))
    kvw = 2 * KV_WIDTH
    return pl.pallas_call(
        _inproj_kernel,
        grid=(m // tm,),
        in_specs=[row(D_MODEL), _resident((1, D_MODEL)), _resident((D_MODEL, IN_WIDTH_PAD)),
                  _resident((1, HEAD_DIM)), _resident((1, HEAD_DIM)), _resident((1, HEAD_DIM))],
        out_specs=[row(SSM_WIDTH), row(ATTN_WIDTH), row(kvw), row(kvw), row(kvw), row(GATE_PAD)],
        out_shape=[jax.ShapeDtypeStruct((m, SSM_WIDTH), F32), jax.ShapeDtypeStruct((m, ATTN_WIDTH), BF16),
                   jax.ShapeDtypeStruct((m, kvw), F32), jax.ShapeDtypeStruct((m, kvw), F32),
                   jax.ShapeDtypeStruct((m, kvw), F32), jax.ShapeDtypeStruct((m, GATE_PAD), F32)],
        compiler_params=_cparams(("parallel",), 48),
        name="inproj",
    )(x, gmix, w, qn, ksn, kwn)


def _s5_group_body(gi, u_ref, h0_ref, lrow_ref, lcol_ref, bt_ref, bn_ref, ct_ref, d_ref,
                   y_ref, hl_ref, tm_ref, x_ref, hin_ref, *, lp, lreal, nb, nc):
    k = lp * SSM_CH
    n = SSM_STATE

    def lam_bar(a_re, a_im, log_dt):
        dt = jnp.exp(log_dt)
        e = jnp.exp(a_re * dt)
        return e * jnp.cos(a_im * dt), e * jnp.sin(a_im * dt)

    def zoh_coef(l_re, l_im, a_re, a_im):
        den = a_re * a_re + a_im * a_im
        x_re = l_re - 1.0
        return (x_re * a_re + l_im * a_im) / den, (l_im * a_re - x_re * a_im) / den

    lc = lcol_ref[gi]
    a_re_c, a_im_c = lc[:, 0:1], lc[:, 1:2]
    l_re_c, l_im_c = lam_bar(a_re_c, a_im_c, lc[:, 2:3])
    lr = lrow_ref[gi]
    a_re_r, a_im_r = lr[0:1], lr[1:2]
    l_re_r, l_im_r = lam_bar(a_re_r, a_im_r, lr[2:3])

    def pow_table(e):
        p_re = jnp.ones((n, k), F32)
        p_im = jnp.zeros((n, k), F32)
        s_re, s_im = l_re_c, l_im_c
        for j in range(max(lp.bit_length() - 1, 1)):
            bit = ((e >> j) & 1) == 1
            m_re = jnp.where(bit, s_re, 1.0)
            m_im = jnp.where(bit, s_im, 0.0)
            p_re, p_im = p_re * m_re - p_im * m_im, p_re * m_im + p_im * m_re
            s_re, s_im = s_re * s_re - s_im * s_im, 2.0 * s_re * s_im
        return p_re, p_im

    lane = lax.broadcasted_iota(jnp.int32, (1, k), 1)
    tau = lane >> (SSM_CH.bit_length() - 1)
    ch_onehot = jnp.where((lax.broadcasted_iota(jnp.int32, (SSM_CH, k), 1) & (SSM_CH - 1))
                          == lax.broadcasted_iota(jnp.int32, (SSM_CH, k), 0), 1.0, 0.0).astype(BF16)

    p_re, p_im = pow_table(tau)
    c_re = _dot_small_int(ct_ref[gi, 0], ch_onehot)
    c_im = _dot_small_int(ct_ref[gi, 1], ch_onehot)
    g_re = c_re * p_re - c_im * p_im
    g_im = c_re * p_im + c_im * p_re

    cf_re_r, cf_im_r = zoh_coef(l_re_r, l_im_r, a_re_r, a_im_r)
    bbt_re = cf_re_r * bt_ref[gi, 0] - cf_im_r * bt_ref[gi, 1]
    bbt_im = cf_re_r * bt_ref[gi, 1] + cf_im_r * bt_ref[gi, 0]
    a = _dot_f32(bbt_re, g_re) - _dot_f32(bbt_im, g_im)

    lane16 = lax.broadcasted_iota(jnp.int32, (SSM_CH, k), 1)
    for s in range(lp):
        blk = a if s == 0 else jnp.where(lane16 >= SSM_CH * s, pltpu.roll(a, SSM_CH * s, axis=1), 0.0)
        tm_ref[gi, SSM_CH * s:SSM_CH * (s + 1), :] = blk.astype(BF16)

    cf_re_c, cf_im_c = zoh_coef(l_re_c, l_im_c, a_re_c, a_im_c)
    bbn_re = cf_re_c * bn_ref[gi, 0] - cf_im_c * bn_ref[gi, 1]
    bbn_im = cf_re_c * bn_ref[gi, 1] + cf_im_c * bn_ref[gi, 0]
    bb_re = _dot_small_int(bbn_re, ch_onehot)
    bb_im = _dot_small_int(bbn_im, ch_onehot)
    rev = (lreal - 1) - tau
    q_re, q_im = pow_table(jnp.maximum(rev, 0))
    live = rev >= 0
    wx_re = jnp.where(live, q_re * bb_re - q_im * bb_im, 0.0).astype(BF16)
    wx_im = jnp.where(live, q_re * bb_im + q_im * bb_re, 0.0).astype(BF16)

    u = u_ref[gi]
    ub = u.astype(BF16)
    x_ref[gi, 0] = _bdot_nt(ub, wx_re)
    x_ref[gi, 1] = _bdot_nt(ub, wx_im)

    ll_re, ll_im = l_re_r, l_im_r
    for _ in range(lreal.bit_length() - 1):
        ll_re, ll_im = ll_re * ll_re - ll_im * ll_im, 2.0 * ll_re * ll_im
    h0 = h0_ref[gi]
    h_re, h_im = h0[:, :n], h0[:, n:]
    for c in range(nc):
        rows = slice(c * nb, (c + 1) * nb)
        hin_ref[gi, 0, rows, :] = h_re
        hin_ref[gi, 1, rows, :] = h_im
        h_re, h_im = (ll_re * h_re - ll_im * h_im + x_ref[gi, 0, rows, :],
                      ll_re * h_im + ll_im * h_re + x_ref[gi, 1, rows, :])
    hl_ref[gi] = jnp.concatenate([h_re, h_im], axis=1)

    g1_re = (g_re * l_re_c - g_im * l_im_c).astype(BF16)
    g1_im = (g_re * l_im_c + g_im * l_re_c).astype(BF16)
    y_state = _bdot(hin_ref[gi, 0].astype(BF16), g1_re) - _bdot(hin_ref[gi, 1].astype(BF16), g1_im)

    y_ref[gi] = _bdot(ub, tm_ref[gi]) + y_state + u * d_ref[gi]


def _s5_kernel(*refs, gb, **kw):
    for gi in range(gb):
        _s5_group_body(gi, *refs, **kw)


def _s5(u_g, h0_g, prm, lp, lreal, nb, nc):
    g, m, k = u_g.shape
    gb = S5_GROUPS_PER_STEP
    blk = lambda *s: pl.BlockSpec((gb,) + s, lambda i: (i,) + (0,) * len(s))
    return pl.pallas_call(
        functools.partial(_s5_kernel, gb=gb, lp=lp, lreal=lreal, nb=nb, nc=nc),
        grid=(g // gb,),
        in_specs=[blk(m, k), blk(nb, 2 * SSM_STATE), blk(3, SSM_STATE), blk(SSM_STATE, 3),
                  blk(2, SSM_CH, SSM_STATE), blk(2, SSM_STATE, SSM_CH), blk(2, SSM_STATE, SSM_CH), blk(1, k)],
        out_specs=[blk(m, k), blk(nb, 2 * SSM_STATE)],
        out_shape=[jax.ShapeDtypeStruct((g, m, k), F32), jax.ShapeDtypeStruct((g, nb, 2 * SSM_STATE), F32)],
        scratch_shapes=[pltpu.VMEM((gb, k, k), BF16), pltpu.VMEM((gb, 2, m, SSM_STATE), F32),
                        pltpu.VMEM((gb, 2, m, SSM_STATE), F32)],
        compiler_params=_cparams(("parallel",), 32),
        name="s5_group",
    )(u_g, h0_g, prm["lrow"], prm["lcol"], prm["bt"], prm["bn"], prm["ct"], prm["dtile"][lp])


def _s5_mixer(u, h0, prm, lp, lreal):
    b, t, _ = u.shape
    nc = t // lreal
    ug = u.reshape(b, nc, lreal, SSM_GROUPS, SSM_CH)
    if lp != lreal:
        ug = jnp.pad(ug, ((0, 0), (0, 0), (0, lp - lreal), (0, 0), (0, 0)))
    ug = ug.transpose(3, 1, 0, 2, 4).reshape(SSM_GROUPS, nc * b, lp * SSM_CH)
    h0g = h0.transpose(1, 0, 3, 2).reshape(SSM_GROUPS, b, 2 * SSM_STATE)
    yg, hl = _s5(ug, h0g, prm, lp, lreal, b, nc)
    y = yg.reshape(SSM_GROUPS, nc, b, lp, SSM_CH)[:, :, :, :lreal].transpose(2, 1, 3, 0, 4)
    h_last = hl.reshape(SSM_GROUPS, b, 2, SSM_STATE).transpose(1, 0, 3, 2)
    return y.reshape(b, t, SSM_WIDTH), h_last


def _cmp_proj_kernel(*refs, n_in, rowwise):
    refs = refs[len(refs) - n_in - 3:]
    x_refs, (w1k_ref, w1v_ref, o_ref) = refs[:n_in], refs[n_in:]

    def rows_of(j, r):
        if rowwise:
            n = x_refs[0].shape[0] // (4 * CMP_STRIDE)
            parts = [x[pl.ds(4 * r + j, n, stride=4 * CMP_STRIDE), :] for x in x_refs]
            return parts[0] if n_in == 1 else jnp.concatenate(parts, axis=0)
        x = x_refs[j]
        return x[0, pl.ds(r, x.shape[1] // CMP_STRIDE, stride=CMP_STRIDE), :]

    for j in range(4):
        c = jnp.concatenate([rows_of(j, r) for r in range(CMP_STRIDE)], axis=1).astype(BF16)
        w = w1k_ref if j % 2 == 0 else w1v_ref
        o_ref[0, :, j * 2 * HEAD_DIM:(j + 1) * 2 * HEAD_DIM] = _bdot(c, w[...])


def _cmp_combine_kernel(h_ref, b1_ref, w2k_ref, w2v_ref, kn_ref, o_ref, *, nchunk):
    h = h_ref[0]
    keep = lax.broadcasted_iota(jnp.int32, (nchunk, 1), 0) < nchunk - 1
    for j in range(4):
        kvh, is_v = j // 2, j % 2
        lo = h[:, j * 2 * HEAD_DIM: j * 2 * HEAD_DIM + HEAD_DIM]
        hi = h[:, j * 2 * HEAD_DIM + HEAD_DIM:(j + 1) * 2 * HEAD_DIM]
        pre = lo + pltpu.roll(hi, nchunk - 1, axis=0) + b1_ref[is_v:is_v + 1, :]
        out = _bdot(_gelu(pre).astype(BF16), (w2v_ref if is_v else w2k_ref)[...])
        if not is_v:
            out = _rms(out, kn_ref[...])
        c0 = is_v * N_KV_HEADS * HEAD_DIM + kvh * HEAD_DIM
        o_ref[0, :, c0:c0 + HEAD_DIM] = jnp.where(keep, out, 0.0)


def _cmp_combine(h, prm):
    b, nchunk, w = h.shape
    return pl.pallas_call(
        functools.partial(_cmp_combine_kernel, nchunk=nchunk),
        grid=(b,),
        in_specs=[pl.BlockSpec((1, nchunk, w), lambda i: (i, 0, 0)), _resident((2, HEAD_DIM)),
                  _resident((HEAD_DIM, HEAD_DIM)), _resident((HEAD_DIM, HEAD_DIM)), _resident((1, HEAD_DIM))],
        out_specs=pl.BlockSpec((1, nchunk, 4 * HEAD_DIM), lambda i: (i, 0, 0)),
        out_shape=jax.ShapeDtypeStruct((b, nchunk, 4 * HEAD_DIM), F32),
        compiler_params=_cparams(("parallel",), 40),
        name="cmp_combine",
    )(h, prm["cmp_b1"], prm["cmp_w2_k"], prm["cmp_w2_v"], prm["k_norm_cmp"])


def _compress_prompt(kv_cmp, prm):
    b, t, w = kv_cmp.shape
    nchunk = t // CMP_STRIDE
    streams = w // HEAD_DIM
    h = pl.pallas_call(
        functools.partial(_cmp_proj_kernel, n_in=streams, rowwise=False),
        grid=(b,),
        in_specs=[pl.BlockSpec((1, t, HEAD_DIM), functools.partial(lambda i, j: (i, 0, j), j=j))
                  for j in range(streams)] + [
                  _resident((CMP_STRIDE * HEAD_DIM, 2 * HEAD_DIM)), _resident((CMP_STRIDE * HEAD_DIM, 2 * HEAD_DIM))],
        out_specs=pl.BlockSpec((1, nchunk, 8 * HEAD_DIM), lambda i: (i, 0, 0)),
        out_shape=jax.ShapeDtypeStruct((b, nchunk, 8 * HEAD_DIM), F32),
        compiler_params=_cparams(("parallel",), 40),
        name="cmp_proj_prompt",
    )(*([kv_cmp] * streams), prm["cmp_w1_k"], prm["cmp_w1_v"])
    return _cmp_combine(h, prm)


PAGES_PER_STEP = 16


def _rowwise_pages(cache):
    return cache.reshape(cache.shape[0], cache.shape[1], PAGE_SIZE * 2 * N_KV_HEADS, HEAD_DIM)


def _compress_sample(cache_cmp, layer, page_table, prm):
    bsz, n_pages = page_table.shape
    chunks_per_page = PAGE_SIZE // CMP_STRIDE
    pages = _rowwise_pages(cache_cmp)
    steps = n_pages // PAGES_PER_STEP

    def page_spec(p):
        return pl.BlockSpec((None, None, 4 * PAGE_SIZE, HEAD_DIM),
                            lambda b, s, pt: (layer, pt[b * n_pages + s * PAGES_PER_STEP + p], 0, 0))

    rows = PAGES_PER_STEP * chunks_per_page
    h = pl.pallas_call(
        functools.partial(_cmp_proj_kernel, n_in=PAGES_PER_STEP, rowwise=True),
        grid_spec=pltpu.PrefetchScalarGridSpec(
            num_scalar_prefetch=1,
            grid=(bsz, steps),
            in_specs=[page_spec(p) for p in range(PAGES_PER_STEP)] + [
                pl.BlockSpec((CMP_STRIDE * HEAD_DIM, 2 * HEAD_DIM), lambda b, s, pt: (0, 0)),
                pl.BlockSpec((CMP_STRIDE * HEAD_DIM, 2 * HEAD_DIM), lambda b, s, pt: (0, 0))],
            out_specs=pl.BlockSpec((1, rows, 8 * HEAD_DIM), lambda b, s, pt: (b, s, 0)),
        ),
        out_shape=jax.ShapeDtypeStruct((bsz, n_pages * chunks_per_page, 8 * HEAD_DIM), F32),
        compiler_params=_cparams(("parallel", "parallel"), 40),
        name="cmp_proj_sample",
    )(page_table.reshape(-1), *([pages] * PAGES_PER_STEP), prm["cmp_w1_k"], prm["cmp_w1_v"])
    return _cmp_combine(h, prm)


def _cmp_select_kernel(q_ref, ck_ref, cv_ref, o_ref, rank_ref, *idx_ref, tq, nc, nbp, nblk, qpos0):
    kvh = pl.program_id(1)
    qpos = qpos0 + pl.program_id(2) * tq + lax.broadcasted_iota(jnp.int32, (tq, 1), 0)
    start = lax.broadcasted_iota(jnp.int32, (1, nc), 1) * CMP_STRIDE
    valid = (start + (CMP_LEN - 1)) <= qpos
    dist = (qpos - start).astype(F32) - (CMP_LEN - 1) / 2
    ck = ck_ref[0].astype(BF16)
    cv = cv_ref[0].astype(BF16)
    psum = jnp.zeros((tq, nc), F32)
    for g in range(GQA):
        sl = slice(g * HEAD_DIM, (g + 1) * HEAD_DIM)
        s = _bdot_nt(q_ref[0, :, sl], ck) - _head_slope(kvh, g) * dist
        _, e, den = _softmax_parts(s, valid)
        p = e / jnp.maximum(den, F32_TINY)
        o_ref[0, :, sl] = _bdot(p.astype(BF16), cv)
        psum = psum + p

    per_shift = (SEL_LEN // CMP_STRIDE).bit_length() - 1
    ci = lax.broadcasted_iota(jnp.int32, (nc, nbp), 0)
    bj = lax.broadcasted_iota(jnp.int32, (nc, nbp), 1)
    overlap = (jnp.where(ci >> per_shift == bj, 1.0, 0.0)
               + jnp.where((ci + 1) >> per_shift == bj, 1.0, 0.0)).astype(BF16)
    score = _dot_small_int(psum, overlap)

    j = lax.broadcasted_iota(jnp.int32, (1, nbp), 1)
    qb = qpos >> SEL_SHIFT
    forced = (j == 0) | (j == qb) | (j == qb - 1)
    causal = (j * SEL_LEN) <= qpos
    score = jnp.where(forced, FORCED_SCORE, jnp.where(causal, score, -1.0))
    score = jnp.where(j < nblk, score, -2.0)

    rank = jnp.zeros((tq, nbp), F32)
    for i in range(nblk):
        si = score[:, i:i + 1]
        rank = rank + jnp.where(j > i, jnp.where(si >= score, 1.0, 0.0), jnp.where(si > score, 1.0, 0.0))
    rank_ref[0, 0] = rank

    if idx_ref:
        jf = j.astype(F32)
        lane = lax.broadcasted_iota(jnp.int32, (tq, V7X_LANES), 1)
        idx = jnp.zeros((tq, V7X_LANES), F32)
        for r in range(N_SEL):
            col = jnp.sum(jnp.where(rank == float(r), jf, 0.0), axis=-1, keepdims=True)
            idx = jnp.where(lane == r, col, idx)
        idx_ref[0][0, 0] = idx.astype(jnp.int32)


def _cmp_select(q, ckv, tq, nblk, qpos0, want_idx):
    b, t, _ = q.shape
    nc = ckv.shape[1]
    nbp = -(-nblk // V7X_LANES) * V7X_LANES
    hw = GQA * HEAD_DIM
    out_specs = [pl.BlockSpec((1, tq, hw), lambda bi, k, i: (bi, i, k)),
                 pl.BlockSpec((1, 1, tq, nbp), lambda bi, k, i: (bi, k, i, 0))]
    out_shape = [jax.ShapeDtypeStruct((b, t, ATTN_WIDTH), F32),
                 jax.ShapeDtypeStruct((b, N_KV_HEADS, t, nbp), F32)]
    if want_idx:
        out_specs.append(pl.BlockSpec((1, 1, tq, V7X_LANES), lambda bi, k, i: (bi, k, i, 0)))
        out_shape.append(jax.ShapeDtypeStruct((b, N_KV_HEADS, t, V7X_LANES), jnp.int32))
    return pl.pallas_call(
        functools.partial(_cmp_select_kernel, tq=tq, nc=nc, nbp=nbp, nblk=nblk, qpos0=qpos0),
        grid=(b, N_KV_HEADS, t // tq),
        in_specs=[pl.BlockSpec((1, tq, hw), lambda bi, k, i: (bi, i, k)),
                  pl.BlockSpec((1, nc, HEAD_DIM), lambda bi, k, i: (bi, 0, k)),
                  pl.BlockSpec((1, nc, HEAD_DIM), lambda bi, k, i: (bi, 0, N_KV_HEADS + k))],
        out_specs=out_specs,
        out_shape=out_shape,
        compiler_params=_cparams(("parallel", "parallel", "parallel"), 40),
        name="cmp_select",
    )(q, ckv, ckv)


ATT_TQ = 128
SEL_TK = 256
FILL_ROWS = 256
POS_RADIX = 256
RATE_TERMS = 3
MASK_BIAS = 2.0 ** 100


def _key_features(kidx, nblk):
    lane = lax.broadcasted_iota(jnp.int32, (1, V7X_LANES), 1)
    radix_shift = POS_RADIX.bit_length() - 1
    digits = jnp.where(((lane - nblk) & 1) == 0, (kidx >> radix_shift).astype(F32),
                       (kidx & (POS_RADIX - 1)).astype(F32))
    feats = jnp.where(lane < nblk + 2 * RATE_TERMS, digits, 0.0)
    if nblk:
        feats = jnp.where(lane < nblk, jnp.where((kidx >> SEL_SHIFT) == lane, 1.0, 0.0), feats)
    return feats


def _query_features(kvh, g, rank, nblk, rows):
    lane = lax.broadcasted_iota(jnp.int32, (rows, V7X_LANES), 1)
    feats = jnp.zeros((rows, V7X_LANES), F32)
    if nblk:
        feats = jnp.where((lane < nblk) & (rank >= float(N_SEL)), -MASK_BIAS, 0.0)
    for i, (c0, c1) in enumerate(zip(_bf16_terms(_alibi_slope(0, g), RATE_TERMS),
                                     _bf16_terms(_alibi_slope(1, g), RATE_TERMS))):
        c = jnp.where(kvh == 0, c0, c1)
        feats = jnp.where(lane == nblk + 2 * i, c * POS_RADIX, feats)
        feats = jnp.where(lane == nblk + 2 * i + 1, c, feats)
    return feats


def _fill_key_value_scratch(k_ref, v_ref, kf_ref, vo_ref, nblk):
    def chunk(c, carry):
        r0 = pl.multiple_of(c * FILL_ROWS, FILL_ROWS)
        rows = pl.ds(r0, FILL_ROWS)
        kidx = r0 + lax.broadcasted_iota(jnp.int32, (FILL_ROWS, 1), 0)
        kf_ref[rows, :HEAD_DIM] = k_ref[0, rows, :].astype(BF16)
        kf_ref[rows, HEAD_DIM:] = _key_features(kidx, nblk).astype(BF16)
        vo_ref[rows, :HEAD_DIM] = v_ref[0, rows, :].astype(BF16)
        vo_ref[rows, HEAD_DIM:] = jnp.ones((FILL_ROWS, V7X_LANES), BF16)
        return carry

    lax.fori_loop(0, k_ref.shape[1] // FILL_ROWS, chunk, 0)


def _attn_out(acc):
    return acc[:, :HEAD_DIM] / jnp.maximum(acc[:, HEAD_DIM:HEAD_DIM + 1], F32_TINY)


def _sel_prompt_kernel(q_ref, k_ref, v_ref, rank_ref, o_ref, kf_ref, vo_ref, qf_ref, m_ref, acc_ref, *, nblk):
    kvh, i = pl.program_id(1), pl.program_id(2)

    @pl.when(i == 0)
    def _():
        _fill_key_value_scratch(k_ref, v_ref, kf_ref, vo_ref, nblk)

    q0 = i * ATT_TQ
    qpos = q0 + lax.broadcasted_iota(jnp.int32, (ATT_TQ, 1), 0)
    rank = rank_ref[0, 0]
    for g in range(GQA):
        qf_ref[g, :, :HEAD_DIM] = q_ref[0, :, g * HEAD_DIM:(g + 1) * HEAD_DIM]
        qf_ref[g, :, HEAD_DIM:] = _query_features(kvh, g, rank, nblk, ATT_TQ).astype(BF16)
    m_ref[...] = jnp.full(m_ref.shape, NEG_INF, F32)
    acc_ref[...] = jnp.zeros(acc_ref.shape, F32)

    def tile(jt, carry, diagonal):
        k0 = pl.multiple_of(jt * SEL_TK, SEL_TK)
        kf = kf_ref[pl.ds(k0, SEL_TK), :]
        vo = vo_ref[pl.ds(k0, SEL_TK), :]
        if diagonal:
            causal = (k0 + lax.broadcasted_iota(jnp.int32, (1, SEL_TK), 1)) <= qpos
        for g in range(GQA):
            s = _bdot_nt(qf_ref[g], kf)
            if diagonal:
                s = jnp.where(causal, s, NEG_INF)
            _flash_step(m_ref, acc_ref, g, s, vo)
        return carry

    n_full = q0 // SEL_TK
    lax.fori_loop(0, n_full, functools.partial(tile, diagonal=False), 0)
    tile(n_full, 0, True)
    for g in range(GQA):
        o_ref[0, :, g * HEAD_DIM:(g + 1) * HEAD_DIM] = _attn_out(acc_ref[g])


def _sel_prompt(q, kv_slc, rank, nblk):
    b, t, _ = q.shape
    nbp = rank.shape[-1]
    hw = GQA * HEAD_DIM
    assert nbp == V7X_LANES and nblk + 2 * RATE_TERMS <= V7X_LANES and t <= POS_RADIX * POS_RADIX
    return pl.pallas_call(
        functools.partial(_sel_prompt_kernel, nblk=nblk),
        grid=(b, N_KV_HEADS, t // ATT_TQ),
        in_specs=[pl.BlockSpec((1, ATT_TQ, hw), lambda bi, k, i: (bi, i, k)),
                  pl.BlockSpec((1, t, HEAD_DIM), lambda bi, k, i: (bi, 0, 2 * k)),
                  pl.BlockSpec((1, t, HEAD_DIM), lambda bi, k, i: (bi, 0, 2 * k + 1)),
                  pl.BlockSpec((1, 1, ATT_TQ, nbp), lambda bi, k, i: (bi, k, i, 0))],
        out_specs=pl.BlockSpec((1, ATT_TQ, hw), lambda bi, k, i: (bi, i, k)),
        out_shape=jax.ShapeDtypeStruct((b, t, ATTN_WIDTH), F32),
        scratch_shapes=[pltpu.VMEM((t, 2 * HEAD_DIM), BF16), pltpu.VMEM((t, 2 * HEAD_DIM), BF16),
                        pltpu.VMEM((GQA, ATT_TQ, 2 * HEAD_DIM), BF16),
                        pltpu.VMEM((GQA, ATT_TQ, 1), F32), pltpu.VMEM((GQA, ATT_TQ, 2 * HEAD_DIM), F32)],
        compiler_params=_cparams(("parallel", "parallel", "arbitrary"), 32),
        name="sel_prompt",
    )(q, kv_slc, kv_slc, rank)


def _win_prompt_kernel(q_ref, k_ref, v_ref, o_ref, kf_ref, vo_ref):
    kvh, i = pl.program_id(1), pl.program_id(2)

    @pl.when(i == 0)
    def _():
        _fill_key_value_scratch(k_ref, v_ref, kf_ref, vo_ref, 0)

    span = WINDOW + WIN_QBLK
    qpos = i * WIN_QBLK + lax.broadcasted_iota(jnp.int32, (WIN_QBLK, 1), 0)
    k0 = pl.multiple_of(jnp.maximum(i * WIN_QBLK - WINDOW, 0), WIN_QBLK)
    kf = kf_ref[pl.ds(k0, span), :]
    vo = vo_ref[pl.ds(k0, span), :]
    d = qpos - (k0 + lax.broadcasted_iota(jnp.int32, (1, span), 1))
    valid = (d >= 0) & (d < WINDOW)
    for g in range(GQA):
        sl = slice(g * HEAD_DIM, (g + 1) * HEAD_DIM)
        qf = jnp.concatenate([q_ref[0, :, sl], _query_features(kvh, g, None, 0, WIN_QBLK).astype(BF16)], axis=1)
        s = jnp.where(valid, _bdot_nt(qf, kf), NEG_INF)
        e = jnp.exp2(s - jnp.max(s, axis=-1, keepdims=True))
        o_ref[0, :, sl] = _attn_out(_bdot(e.astype(BF16), vo))


def _win_prompt(q, kv_win):
    b, t, _ = q.shape
    hw = GQA * HEAD_DIM
    assert t <= POS_RADIX * POS_RADIX
    return pl.pallas_call(
        _win_prompt_kernel,
        grid=(b, N_KV_HEADS, t // WIN_QBLK),
        in_specs=[pl.BlockSpec((1, WIN_QBLK, hw), lambda bi, k, i: (bi, i, k)),
                  pl.BlockSpec((1, t, HEAD_DIM), lambda bi, k, i: (bi, 0, 2 * k)),
                  pl.BlockSpec((1, t, HEAD_DIM), lambda bi, k, i: (bi, 0, 2 * k + 1))],
        out_specs=pl.BlockSpec((1, WIN_QBLK, hw), lambda bi, k, i: (bi, i, k)),
        out_shape=jax.ShapeDtypeStruct((b, t, ATTN_WIDTH), F32),
        scratch_shapes=[pltpu.VMEM((t, 2 * HEAD_DIM), BF16), pltpu.VMEM((t, 2 * HEAD_DIM), BF16)],
        compiler_params=_cparams(("parallel", "parallel", "arbitrary"), 32),
        name="win_prompt",
    )(q, kv_win, kv_win)


def _slope_col(kvh):
    g = lax.broadcasted_iota(jnp.int32, (GQA, 1), 0)
    col = jnp.zeros((GQA, 1), F32)
    for i in range(GQA):
        col = jnp.where(g == i, _alibi_slope(kvh, i), col)
    return col


def _stream_rows(ref, lead, stream, n):
    return ref[lead + (pl.ds(stream, n, stride=2 * N_KV_HEADS), slice(None))]


def _sel_sample_kernel(idx_ref, pt_ref, q_ref, *refs, t_len):
    past, new_ref, o_ref = refs[:N_KV_HEADS * N_SEL], refs[N_KV_HEADS * N_SEL], refs[N_KV_HEADS * N_SEL + 1]
    b, t = pl.program_id(0), pl.program_id(1)
    past_blocks = PAST_LEN // SEL_LEN
    lane = lax.broadcasted_iota(jnp.int32, (1, N_SEL * SEL_LEN), 1)
    for kvh in range(N_KV_HEADS):
        ks, vs = [], []
        base = jnp.zeros((1, N_SEL * SEL_LEN), jnp.int32)
        for r in range(N_SEL):
            blk = idx_ref[((b * N_KV_HEADS + kvh) * t_len + t) * N_SEL + r]
            is_new = blk >= past_blocks
            ref = past[kvh * N_SEL + r]
            ks.append(jnp.where(is_new, _stream_rows(new_ref, (0,), 2 * kvh, SEL_LEN),
                                _stream_rows(ref, (), 2 * kvh, SEL_LEN)).astype(BF16))
            vs.append(jnp.where(is_new, _stream_rows(new_ref, (0,), 2 * kvh + 1, SEL_LEN),
                                _stream_rows(ref, (), 2 * kvh + 1, SEL_LEN)).astype(BF16))
            base = jnp.where((lane >> SEL_SHIFT) == r, blk * SEL_LEN, base)
        d = (PAST_LEN + t) - (base + (lane & (SEL_LEN - 1)))
        rows = slice(kvh * GQA, (kvh + 1) * GQA)
        s = _bdot_nt(q_ref[0, 0, rows, :], jnp.concatenate(ks, axis=0)) - _slope_col(kvh) * d.astype(F32)
        _, e, z = _softmax_parts(s, d >= 0)
        o_ref[0, 0, rows, :] = _bdot(e.astype(BF16), jnp.concatenate(vs, axis=0)) / jnp.maximum(z, F32_TINY)


def _sel_sample(q_h, cache_slc, layer, page_table, idx, kv_new_rows):
    bsz, t_len = q_h.shape[:2]
    n_pages = page_table.shape[1]
    per_page = PAGE_SIZE // SEL_LEN
    past_blocks = PAST_LEN // SEL_LEN
    pages = _rowwise_pages(cache_slc)
    blk_rows = SEL_LEN * 2 * N_KV_HEADS

    def past_spec(kvh, r):
        def imap(b, t, idx_ref, pt_ref):
            blk = jnp.minimum(idx_ref[((b * N_KV_HEADS + kvh) * t_len + t) * N_SEL + r], past_blocks - 1)
            return (layer, pt_ref[b * n_pages + blk // per_page], blk % per_page, 0)
        return pl.BlockSpec((None, None, blk_rows, HEAD_DIM), imap)

    qspec = pl.BlockSpec((1, 1, N_HEADS, HEAD_DIM), lambda b, t, i_, p_: (b, t, 0, 0))
    return pl.pallas_call(
        functools.partial(_sel_sample_kernel, t_len=t_len),
        grid_spec=pltpu.PrefetchScalarGridSpec(
            num_scalar_prefetch=2,
            grid=(bsz, t_len),
            in_specs=[qspec] + [past_spec(kvh, r) for kvh in range(N_KV_HEADS) for r in range(N_SEL)]
            + [pl.BlockSpec((1, blk_rows, HEAD_DIM), lambda b, t, i_, p_: (b, 0, 0))],
            out_specs=qspec,
        ),
        out_shape=jax.ShapeDtypeStruct(q_h.shape, F32),
        compiler_params=_cparams(("parallel", "parallel"), 32),
        name="sel_sample",
    )(idx.reshape(-1), page_table.reshape(-1), q_h, *([pages] * (N_KV_HEADS * N_SEL)), kv_new_rows)


def _win_sample_kernel(q_ref, past_ref, new_ref, o_ref, *, t_len, t_pad, wbuf):
    t = lax.broadcasted_iota(jnp.int32, (t_len, 1), 0)
    d_past = wbuf + t - lax.broadcasted_iota(jnp.int32, (1, wbuf), 1)
    row_new = lax.broadcasted_iota(jnp.int32, (1, t_pad), 1)
    d_new = t - row_new
    valid_past = (d_past >= 0) & (d_past < WINDOW)
    valid_new = (d_new >= 0) & (d_new < WINDOW) & (row_new < t_len)
    for kvh in range(N_KV_HEADS):
        kp = _stream_rows(past_ref, (0,), 2 * kvh, wbuf).astype(BF16)
        vp = _stream_rows(past_ref, (0,), 2 * kvh + 1, wbuf).astype(BF16)
        kn = _stream_rows(new_ref, (0,), 2 * kvh, t_pad).astype(BF16)
        vn = _stream_rows(new_ref, (0,), 2 * kvh + 1, t_pad).astype(BF16)
        for g in range(GQA):
            q = q_ref[0, kvh, g]
            slope = _alibi_slope(kvh, g)
            m1, e1, z1 = _softmax_parts(_bdot_nt(q, kp) - slope * d_past.astype(F32), valid_past)
            m2, e2, z2 = _softmax_parts(_bdot_nt(q, kn) - slope * d_new.astype(F32), valid_new)
            m = jnp.maximum(m1, m2)
            a1 = jnp.where(z1 > 0.0, jnp.exp2(m1 - m), 0.0)
            a2 = jnp.where(z2 > 0.0, jnp.exp2(m2 - m), 0.0)
            num = a1 * _bdot(e1.astype(BF16), vp) + a2 * _bdot(e2.astype(BF16), vn)
            o_ref[0, kvh, g] = num / jnp.maximum(a1 * z1 + a2 * z2, F32_TINY)


def _win_sample(q_g, win_rows, new_rows):
    bsz, _, _, t_len, _ = q_g.shape
    streams = 2 * N_KV_HEADS
    wbuf, t_pad = win_rows.shape[1] // streams, new_rows.shape[1] // streams
    qspec = pl.BlockSpec((1, N_KV_HEADS, GQA, t_len, HEAD_DIM), lambda b: (b, 0, 0, 0, 0))
    return pl.pallas_call(
        functools.partial(_win_sample_kernel, t_len=t_len, t_pad=t_pad, wbuf=wbuf),
        grid=(bsz,),
        in_specs=[qspec, pl.BlockSpec((1, wbuf * streams, HEAD_DIM), lambda b: (b, 0, 0)),
                  pl.BlockSpec((1, t_pad * streams, HEAD_DIM), lambda b: (b, 0, 0))],
        out_specs=qspec,
        out_shape=jax.ShapeDtypeStruct(q_g.shape, F32),
        compiler_params=_cparams(("parallel",), 16),
        name="win_sample",
    )(q_g, win_rows, new_rows)


def _mix_kernel(x_ref, y_ref, oc_ref, os_ref, ow_ref, gate_ref, wglu_ref, gs_ref, ga_ref, wout_ref, o_ref):
    z = _gelu(y_ref[...])
    z = z * jax.nn.sigmoid(_bdot(z.astype(BF16), wglu_ref[...]))
    gate = gate_ref[...]
    cols = []
    for h in range(N_HEADS):
        sl = slice(h * HEAD_DIM, (h + 1) * HEAD_DIM)
        cols.append(gate[:, h:h + 1] * oc_ref[:, sl] + gate[:, N_HEADS + h:N_HEADS + h + 1] * os_ref[:, sl]
                    + gate[:, 2 * N_HEADS + h:2 * N_HEADS + h + 1] * ow_ref[:, sl])
    attn = jnp.concatenate(cols, axis=1)
    hs = _rms(z, gs_ref[...]).astype(BF16)
    ha = _rms(attn, ga_ref[...]).astype(BF16)
    o_ref[...] = x_ref[...] + (_bdot(hs, wout_ref[:SSM_WIDTH, :]) + _bdot(ha, wout_ref[SSM_WIDTH:, :]))


def _mix(x, y, oc, osl, ow, gates, prm, tm):
    m = x.shape[0]
    row = lambda n: pl.BlockSpec((tm, n), lambda i: (i, 0))
    return pl.pallas_call(
        _mix_kernel,
        grid=(m // tm,),
        in_specs=[row(D_MODEL), row(SSM_WIDTH), row(ATTN_WIDTH), row(ATTN_WIDTH), row(ATTN_WIDTH), row(GATE_PAD),
                  _resident((SSM_WIDTH, SSM_WIDTH)), _resident((1, SSM_WIDTH)), _resident((1, ATTN_WIDTH)),
                  _resident((D_MODEL, D_MODEL))],
        out_specs=row(D_MODEL),
        out_shape=jax.ShapeDtypeStruct((m, D_MODEL), F32),
        compiler_params=_cparams(("parallel",), 48),
        name="mix_outproj",
    )(x, y, oc, osl, ow, gates, prm["ssm_w_glu"], prm["norm_ssm_out"], prm["norm_attn_out"], prm["w_out"])


FFN_TF = 512


def _ffn_kernel(x_ref, g_ref, wg_ref, wu_ref, wd_ref, o_ref, xn_ref, acc_ref):
    j = pl.program_id(1)

    @pl.when(j == 0)
    def _():
        xn_ref[...] = _rms(x_ref[...], g_ref[...]).astype(BF16)
        acc_ref[...] = jnp.zeros(acc_ref.shape, F32)

    xn = xn_ref[...]
    a = _bdot(xn, wg_ref[...])
    h = (a * jax.nn.sigmoid(a)) * _bdot(xn, wu_ref[...])
    acc_ref[...] += _bdot(h.astype(BF16), wd_ref[...])

    @pl.when(j == pl.num_programs(1) - 1)
    def _():
        o_ref[...] = x_ref[...] + acc_ref[...]


def _ffn(x, prm, tm):
    m = x.shape[0]
    return pl.pallas_call(
        _ffn_kernel,
        grid=(m // tm, D_FF // FFN_TF),
        in_specs=[pl.BlockSpec((tm, D_MODEL), lambda i, j: (i, 0)),
                  pl.BlockSpec((1, D_MODEL), lambda i, j: (0, 0)),
                  pl.BlockSpec((D_MODEL, FFN_TF), lambda i, j: (0, j)),
                  pl.BlockSpec((D_MODEL, FFN_TF), lambda i, j: (0, j)),
                  pl.BlockSpec((FFN_TF, D_MODEL), lambda i, j: (j, 0))],
        out_specs=pl.BlockSpec((tm, D_MODEL), lambda i, j: (i, 0)),
        out_shape=jax.ShapeDtypeStruct((m, D_MODEL), F32),
        scratch_shapes=[pltpu.VMEM((tm, D_MODEL), BF16), pltpu.VMEM((tm, D_MODEL), F32)],
        compiler_params=_cparams(("parallel", "arbitrary"), 48),
        name="ffn",
    )(x, prm["norm_ffn"], prm["w_ffn_gate"], prm["w_ffn_up"], prm["w_ffn_down"])


S5_CHUNK = 64
S5_GROUPS_PER_STEP = 2
SAMPLE_S5_PAD = 8


def _layer_params(l, w):
    kvh = jnp.arange(N_KV_HEADS)[:, None] * HEAD_DIM + jnp.arange(HEAD_DIM)[None, :]
    base = D_MODEL
    cols = [jnp.arange(D_MODEL)]
    for br in range(3):
        k0, v0 = base + 2 * br * KV_WIDTH, base + (2 * br + 1) * KV_WIDTH
        cols.append(jnp.stack([k0 + kvh, v0 + kvh], axis=1).reshape(-1))
    g0 = base + 6 * KV_WIDTH
    cols.append((g0 + jnp.arange(N_HEADS)[None, :] * 3 + jnp.arange(3)[:, None]).reshape(-1))
    w_in = jnp.take(w["w_in"][l], jnp.concatenate(cols), axis=1)
    w_in = jnp.pad(w_in, ((0, 0), (0, IN_WIDTH_PAD - w_in.shape[1]))).astype(BF16)

    def w1cat(w1):
        half = CMP_STRIDE * HEAD_DIM
        return jnp.concatenate([w1[:half], w1[half:]], axis=1).astype(BF16)

    row = lambda v: v.reshape(1, -1)
    a_re, a_im = w["ssm_a_re"][l], w["ssm_a_im"][l]
    log_dt = jnp.broadcast_to(w["ssm_log_dt"][l][:, None], a_re.shape)
    lrow = jnp.stack([a_re, a_im, log_dt], axis=1)
    d = w["ssm_d"][l]
    return {
        "w_in": w_in, "norm_mix": row(w["norm_mix"][l]),
        "q_norm": row(w["q_norm"][l]), "k_norm_slc": row(w["k_norm_slc"][l]),
        "k_norm_win": row(w["k_norm_win"][l]), "k_norm_cmp": row(w["k_norm_cmp"][l]),
        "lrow": lrow, "lcol": lrow.transpose(0, 2, 1),
        "bt": jnp.stack([w["ssm_b_re"][l], w["ssm_b_im"][l]], axis=1).transpose(0, 1, 3, 2),
        "bn": jnp.stack([w["ssm_b_re"][l], w["ssm_b_im"][l]], axis=1),
        "ct": jnp.stack([w["ssm_c_re"][l], w["ssm_c_im"][l]], axis=1).transpose(0, 1, 3, 2),
        "dtile": {lp: jnp.tile(d, (1, lp))[:, None, :] for lp in (S5_CHUNK, SAMPLE_S5_PAD)},
        "ssm_w_glu": w["ssm_w_glu"][l].astype(BF16),
        "cmp_w1_k": w1cat(w["cmp_w1_k"][l]), "cmp_w1_v": w1cat(w["cmp_w1_v"][l]),
        "cmp_b1": jnp.stack([w["cmp_b1_k"][l], w["cmp_b1_v"][l]]),
        "cmp_w2_k": w["cmp_w2_k"][l].astype(BF16), "cmp_w2_v": w["cmp_w2_v"][l].astype(BF16),
        "norm_ssm_out": row(w["norm_ssm_out"][l]), "norm_attn_out": row(w["norm_attn_out"][l]),
        "w_out": w["w_out"][l].astype(BF16), "norm_ffn": row(w["norm_ffn"][l]),
        "w_ffn_gate": w["w_ffn_gate"][l].astype(BF16), "w_ffn_up": w["w_ffn_up"][l].astype(BF16),
        "w_ffn_down": w["w_ffn_down"][l].astype(BF16),
    }


def _kv_out(kv, b, t):
    return kv.reshape(b, t, N_KV_HEADS, 2, HEAD_DIM)


PROMPT_TM = 256
FFN_TM = 512


def _prompt_layer(x, prm):
    b, t, _ = x.shape
    m = b * t
    x2 = x.reshape(m, D_MODEL)
    u, q, kv_cmp, kv_slc, kv_win, gates = _inproj(x2, prm["norm_mix"], prm["w_in"], prm["q_norm"],
                                                  prm["k_norm_slc"], prm["k_norm_win"], PROMPT_TM)
    h0 = jnp.zeros((b, SSM_GROUPS, SSM_STATE, 2), F32)
    y, h_last = _s5_mixer(u.reshape(b, t, SSM_WIDTH), h0, prm, S5_CHUNK, S5_CHUNK)
    q3 = q.reshape(b, t, ATTN_WIDTH)
    kv_cmp3, kv_slc3, kv_win3 = (a.reshape(b, t, 2 * KV_WIDTH) for a in (kv_cmp, kv_slc, kv_win))
    ckv = _compress_prompt(kv_cmp3, prm)
    o_cmp, rank = _cmp_select(q3, ckv, 256, t // SEL_LEN, 0, False)
    o_slc = _sel_prompt(q3, kv_slc3, rank, t // SEL_LEN)
    o_win = _win_prompt(q3, kv_win3)
    x2 = _mix(x2, y.reshape(m, SSM_WIDTH), o_cmp.reshape(m, -1), o_slc.reshape(m, -1), o_win.reshape(m, -1),
              gates, prm, PROMPT_TM)
    x2 = _ffn(x2, prm, FFN_TM)
    wbuf = min(WINDOW, PAST_LEN)
    win_state = jnp.pad(kv_win3, ((0, 0), (max(wbuf - t, 0), 0), (0, 0)))[:, -wbuf:]
    return (x2.reshape(b, t, D_MODEL), _kv_out(kv_cmp3, b, t), _kv_out(kv_slc3, b, t),
            _kv_out(win_state, b, wbuf), h_last)


def _sample_layer(x, prm, cache_cmp, cache_slc, layer, page_table, win_buf, h0):
    b, t, _ = x.shape
    m = b * t
    assert (PAST_LEN + t) // CMP_STRIDE == PAST_LEN // CMP_STRIDE and PAST_LEN % PAGE_SIZE == 0
    x2 = x.reshape(m, D_MODEL)
    u, q, kv_cmp, kv_slc, kv_win, gates = _inproj(x2, prm["norm_mix"], prm["w_in"], prm["q_norm"],
                                                  prm["k_norm_slc"], prm["k_norm_win"], m)
    y, h_last = _s5_mixer(u.reshape(b, t, SSM_WIDTH), h0, prm, SAMPLE_S5_PAD, t)
    q3 = q.reshape(b, t, ATTN_WIDTH)
    kv_cmp3, kv_slc3, kv_win3 = (a.reshape(b, t, 2 * KV_WIDTH) for a in (kv_cmp, kv_slc, kv_win))
    ckv = _compress_sample(cache_cmp, layer, page_table, prm)
    n_blocks = -(-(PAST_LEN + t) // SEL_LEN)
    o_cmp, _, idx = _cmp_select(q3, ckv, t, n_blocks, PAST_LEN, True)
    streams = 2 * N_KV_HEADS

    def stream_rows(kv3, t_pad):
        return jnp.pad(kv3.reshape(b, t * streams, HEAD_DIM), ((0, 0), (0, (t_pad - t) * streams), (0, 0)))

    o_slc = _sel_sample(q3.reshape(b, t, N_HEADS, HEAD_DIM), cache_slc, layer, page_table, idx[..., :N_SEL],
                        stream_rows(kv_slc3, SEL_LEN)).reshape(m, ATTN_WIDTH)
    wbuf = win_buf.shape[1]
    q5 = q3.reshape(b, t, N_KV_HEADS, GQA, HEAD_DIM)
    o_win = _win_sample(q5.transpose(0, 2, 3, 1, 4), win_buf.reshape(b, wbuf * streams, HEAD_DIM),
                        stream_rows(kv_win3, 8))
    o_win = o_win.transpose(0, 3, 1, 2, 4).reshape(m, ATTN_WIDTH)
    x2 = _mix(x2, y.reshape(m, SSM_WIDTH), o_cmp.reshape(m, -1), o_slc, o_win, gates, prm, m)
    x2 = _ffn(x2, prm, m)
    win_state = jnp.concatenate([win_buf, _kv_out(kv_win3, b, t)], axis=1)[:, -wbuf:]
    return (x2.reshape(b, t, D_MODEL), _kv_out(kv_cmp3, b, t), _kv_out(kv_slc3, b, t), win_state, h_last)


def kernel(x_prompt, x_sample, cache_cmp_kv, cache_slc_kv, state_win_kv, state_ssm, page_table, norm_mix, w_in, ssm_a_re, ssm_a_im, ssm_log_dt, ssm_b_re, ssm_b_im, ssm_c_re, ssm_c_im, ssm_d, ssm_w_glu, q_norm, k_norm_cmp, k_norm_slc, k_norm_win, cmp_w1_k, cmp_b1_k, cmp_w2_k, cmp_w1_v, cmp_b1_v, cmp_w2_v, norm_ssm_out, norm_attn_out, w_out, norm_ffn, w_ffn_gate, w_ffn_up, w_ffn_down):
    w = dict(norm_mix=norm_mix, w_in=w_in, ssm_a_re=ssm_a_re, ssm_a_im=ssm_a_im, ssm_log_dt=ssm_log_dt,
             ssm_b_re=ssm_b_re, ssm_b_im=ssm_b_im, ssm_c_re=ssm_c_re, ssm_c_im=ssm_c_im, ssm_d=ssm_d,
             ssm_w_glu=ssm_w_glu, q_norm=q_norm, k_norm_cmp=k_norm_cmp, k_norm_slc=k_norm_slc,
             k_norm_win=k_norm_win, cmp_w1_k=cmp_w1_k, cmp_b1_k=cmp_b1_k, cmp_w2_k=cmp_w2_k,
             cmp_w1_v=cmp_w1_v, cmp_b1_v=cmp_b1_v, cmp_w2_v=cmp_w2_v, norm_ssm_out=norm_ssm_out,
             norm_attn_out=norm_attn_out, w_out=w_out, norm_ffn=norm_ffn, w_ffn_gate=w_ffn_gate,
             w_ffn_up=w_ffn_up, w_ffn_down=w_ffn_down)
    y_p, y_s = x_prompt, x_sample
    outs_p, outs_s = [], []
    for l in range(DEPTH):
        prm = _layer_params(l, w)
        y_p, *rest = _prompt_layer(y_p, prm)
        outs_p.append(rest)
        y_s, *rest = _sample_layer(y_s, prm, cache_cmp_kv, cache_slc_kv, l, page_table,
                                   state_win_kv[l], state_ssm[l])
        outs_s.append(rest)
    stack = lambda outs, i: jnp.stack([o[i] for o in outs])
    return (y_p, y_s, stack(outs_p, 0), stack(outs_p, 1), stack(outs_p, 2), stack(outs_p, 3),
            stack(outs_s, 0), stack(outs_s, 1), stack(outs_s, 2), stack(outs_s, 3))
```

```python
import functools
import math

import jax
import jax.numpy as jnp
import numpy as np
from jax import lax
from jax.experimental import pallas as pl
from jax.experimental.pallas import tpu as pltpu

F32 = jnp.float32
BF16 = jnp.bfloat16

D_MODEL = 2048
DEPTH = 2
PAST_LEN = 16384
PAGE_SIZE = 128
SSM_WIDTH = 1024
ATTN_WIDTH = 1024
SSM_CH = 16
SSM_GROUPS = 64
SSM_STATE = 64
HEAD_DIM = 128
N_HEADS = 8
N_KV_HEADS = 2
GQA = 4
KV_WIDTH = 256
CMP_LEN = 32
CMP_STRIDE = 16
SEL_LEN = 64
SEL_SHIFT = SEL_LEN.bit_length() - 1
N_SEL = 16
WINDOW = 512
WIN_QBLK = 128
FORCED_SCORE = 1e4
D_FF = 5632
NORM_EPS = 1e-6
LOG2E = math.log2(math.e)
QSCALE = HEAD_DIM ** -0.5 * LOG2E
GATE_PAD = 128
IN_WIDTH_PAD = SSM_WIDTH + ATTN_WIDTH + 6 * KV_WIDTH + GATE_PAD
F32_TINY = float(jnp.finfo(jnp.float32).tiny)
NEG_INF = float("-inf")

V7X_VMEM_BYTES = 64 * 1024 * 1024
V7X_LANES = 128


def _cparams(semantics, vmem_mib):
    assert vmem_mib * 1024 * 1024 < V7X_VMEM_BYTES
    return pltpu.CompilerParams(dimension_semantics=semantics, vmem_limit_bytes=vmem_mib * 1024 * 1024)


def _resident(shape):
    nd = len(shape)
    return pl.BlockSpec(shape, lambda *_: (0,) * nd, pipeline_mode=pl.Buffered(1))


def _rms(x, g):
    return x * lax.rsqrt(jnp.mean(x * x, axis=-1, keepdims=True) + NORM_EPS) * g


def _gelu(x):
    c = math.sqrt(2.0 / math.pi)
    return x * (0.5 * (1.0 + jnp.tanh(c * (x + 0.044715 * (x * x * x)))))


def _bdot(a, b):
    return jnp.dot(a, b, preferred_element_type=F32)


def _bdot_nt(a, b):
    return lax.dot_general(a, b, (((1,), (1,)), ((), ())), preferred_element_type=F32)


def _split3(x):
    hi = x.astype(BF16)
    r1 = x - hi.astype(F32)
    mid = r1.astype(BF16)
    lo = (r1 - mid.astype(F32)).astype(BF16)
    return hi, mid, lo


def _dot_small_int(x, e):
    hi, mid, lo = _split3(x)
    return _bdot(hi, e) + _bdot(mid, e) + _bdot(lo, e)


def _dot_f32(a, b):
    ah, am, al = _split3(a)
    bh, bm, bl = _split3(b)
    return (_bdot(ah, bh) + (_bdot(ah, bm) + _bdot(am, bh))
            + (_bdot(ah, bl) + _bdot(al, bh) + _bdot(am, bm)))


def _softmax_parts(s, valid):
    s = jnp.where(valid, s, NEG_INF)
    m = jnp.max(s, axis=-1, keepdims=True)
    m = jnp.where(m == NEG_INF, 0.0, m)
    e = jnp.exp2(s - m)
    return m, e, jnp.sum(e, axis=-1, keepdims=True)


def _flash_step(m_ref, acc_ref, g, s, v_ones):
    m_old = m_ref[g]
    m_new = jnp.maximum(m_old, jnp.max(s, axis=-1, keepdims=True))
    m_safe = jnp.where(m_new == NEG_INF, 0.0, m_new)
    acc_ref[g] = jnp.exp2(m_old - m_safe) * acc_ref[g] + _bdot(jnp.exp2(s - m_safe).astype(BF16), v_ones)
    m_ref[g] = m_new


def _alibi_slope(kvh, g):
    return LOG2E * 2.0 ** (-8.0 * (kvh * GQA + g + 1) / N_HEADS)


def _bf16_terms(x, n=3):
    out = []
    for _ in range(n):
        hi = float(np.asarray(x, dtype=np.float32).astype(BF16))
        out.append(hi)
        x -= hi
    return out


def _head_slope(kvh, g):
    assert N_KV_HEADS == 2
    return jnp.where(kvh == 0, _alibi_slope(0, g), _alibi_slope(1, g))


def _inproj_kernel(x_ref, gmix_ref, w_ref, qn_ref, ksn_ref, kwn_ref,
                   u_ref, q_ref, cmp_ref, slc_ref, win_ref, gate_ref):
    xn = _rms(x_ref[...], gmix_ref[...]).astype(BF16)

    def proj(c0, c1):
        return _bdot(xn, w_ref[:, c0:c1])

    u_ref[...] = proj(0, SSM_WIDTH)
    q = proj(SSM_WIDTH, D_MODEL)
    for h in range(N_HEADS):
        sl = slice(h * HEAD_DIM, (h + 1) * HEAD_DIM)
        q_ref[:, sl] = (_rms(q[:, sl], qn_ref[...]) * QSCALE).astype(BF16)
    c0 = D_MODEL
    cmp_ref[...] = proj(c0, c0 + 2 * KV_WIDTH)
    for ref, nref in ((slc_ref, ksn_ref), (win_ref, kwn_ref)):
        c0 += 2 * KV_WIDTH
        kv = proj(c0, c0 + 2 * KV_WIDTH)
        for kvh in range(N_KV_HEADS):
            ks = slice(kvh * 2 * HEAD_DIM, kvh * 2 * HEAD_DIM + HEAD_DIM)
            vs = slice(kvh * 2 * HEAD_DIM + HEAD_DIM, (kvh + 1) * 2 * HEAD_DIM)
            ref[:, ks] = _rms(kv[:, ks], nref[...])
            ref[:, vs] = kv[:, vs]
    c0 += 2 * KV_WIDTH
    gate_ref[...] = jax.nn.sigmoid(proj(c0, c0 + GATE_PAD))


def _inproj(x, gmix, w, qn, ksn, kwn, tm):
    m = x.shape[0]
    row = lambda n: pl.BlockSpec((tm, n), lambda i: (i, 0))
    kvw = 2 * KV_WIDTH
    return pl.pallas_call(
        _inproj_kernel,
        grid=(m // tm,),
        in_specs=[row(D_MODEL), _resident((1, D_MODEL)), _resident((D_MODEL, IN_WIDTH_PAD)),
                  _resident((1, HEAD_DIM)), _resident((1, HEAD_DIM)), _resident((1, HEAD_DIM))],
        out_specs=[row(SSM_WIDTH), row(ATTN_WIDTH), row(kvw), row(kvw), row(kvw), row(GATE_PAD)],
        out_shape=[jax.ShapeDtypeStruct((m, SSM_WIDTH), F32), jax.ShapeDtypeStruct((m, ATTN_WIDTH), BF16),
                   jax.ShapeDtypeStruct((m, kvw), F32), jax.ShapeDtypeStruct((m, kvw), F32),
                   jax.ShapeDtypeStruct((m, kvw), F32), jax.ShapeDtypeStruct((m, GATE_PAD), F32)],
        compiler_params=_cparams(("parallel",), 48),
        name="inproj",
    )(x, gmix, w, qn, ksn, kwn)


def _s5_group_body(gi, u_ref, h0_ref, lrow_ref, lcol_ref, bt_ref, bn_ref, ct_ref, d_ref,
                   y_ref, hl_ref, tm_ref, x_ref, hin_ref, *, lp, lreal, nb, nc):
    k = lp * SSM_CH
    n = SSM_STATE

    def lam_bar(a_re, a_im, log_dt):
        dt = jnp.exp(log_dt)
        e = jnp.exp(a_re * dt)
        return e * jnp.cos(a_im * dt), e * jnp.sin(a_im * dt)

    def zoh_coef(l_re, l_im, a_re, a_im):
        den = a_re * a_re + a_im * a_im
        x_re = l_re - 1.0
        return (x_re * a_re + l_im * a_im) / den, (l_im * a_re - x_re * a_im) / den

    lc = lcol_ref[gi]
    a_re_c, a_im_c = lc[:, 0:1], lc[:, 1:2]
    l_re_c, l_im_c = lam_bar(a_re_c, a_im_c, lc[:, 2:3])
    lr = lrow_ref[gi]
    a_re_r, a_im_r = lr[0:1], lr[1:2]
    l_re_r, l_im_r = lam_bar(a_re_r, a_im_r, lr[2:3])

    def pow_table(e):
        p_re = jnp.ones((n, k), F32)
        p_im = jnp.zeros((n, k), F32)
        s_re, s_im = l_re_c, l_im_c
        for j in range(max(lp.bit_length() - 1, 1)):
            bit = ((e >> j) & 1) == 1
            m_re = jnp.where(bit, s_re, 1.0)
            m_im = jnp.where(bit, s_im, 0.0)
            p_re, p_im = p_re * m_re - p_im * m_im, p_re * m_im + p_im * m_re
            s_re, s_im = s_re * s_re - s_im * s_im, 2.0 * s_re * s_im
        return p_re, p_im

    lane = lax.broadcasted_iota(jnp.int32, (1, k), 1)
    tau = lane >> (SSM_CH.bit_length() - 1)
    ch_onehot = jnp.where((lax.broadcasted_iota(jnp.int32, (SSM_CH, k), 1) & (SSM_CH - 1))
                          == lax.broadcasted_iota(jnp.int32, (SSM_CH, k), 0), 1.0, 0.0).astype(BF16)

    p_re, p_im = pow_table(tau)
    c_re = _dot_small_int(ct_ref[gi, 0], ch_onehot)
    c_im = _dot_small_int(ct_ref[gi, 1], ch_onehot)
    g_re = c_re * p_re - c_im * p_im
    g_im = c_re * p_im + c_im * p_re

    cf_re_r, cf_im_r = zoh_coef(l_re_r, l_im_r, a_re_r, a_im_r)
    bbt_re = cf_re_r * bt_ref[gi, 0] - cf_im_r * bt_ref[gi, 1]
    bbt_im = cf_re_r * bt_ref[gi, 1] + cf_im_r * bt_ref[gi, 0]
    a = _dot_f32(bbt_re, g_re) - _dot_f32(bbt_im, g_im)

    lane16 = lax.broadcasted_iota(jnp.int32, (SSM_CH, k), 1)
    for s in range(lp):
        blk = a if s == 0 else jnp.where(lane16 >= SSM_CH * s, pltpu.roll(a, SSM_CH * s, axis=1), 0.0)
        tm_ref[gi, SSM_CH * s:SSM_CH * (s + 1), :] = blk.astype(BF16)

    cf_re_c, cf_im_c = zoh_coef(l_re_c, l_im_c, a_re_c, a_im_c)
    bbn_re = cf_re_c * bn_ref[gi, 0] - cf_im_c * bn_ref[gi, 1]
    bbn_im = cf_re_c * bn_ref[gi, 1] + cf_im_c * bn_ref[gi, 0]
    bb_re = _dot_small_int(bbn_re, ch_onehot)
    bb_im = _dot_small_int(bbn_im, ch_onehot)
    rev = (lreal - 1) - tau
    q_re, q_im = pow_table(jnp.maximum(rev, 0))
    live = rev >= 0
    wx_re = jnp.where(live, q_re * bb_re - q_im * bb_im, 0.0).astype(BF16)
    wx_im = jnp.where(live, q_re * bb_im + q_im * bb_re, 0.0).astype(BF16)

    u = u_ref[gi]
    ub = u.astype(BF16)
    x_ref[gi, 0] = _bdot_nt(ub, wx_re)
    x_ref[gi, 1] = _bdot_nt(ub, wx_im)

    ll_re, ll_im = l_re_r, l_im_r
    for _ in range(lreal.bit_length() - 1):
        ll_re, ll_im = ll_re * ll_re - ll_im * ll_im, 2.0 * ll_re * ll_im
    h0 = h0_ref[gi]
    h_re, h_im = h0[:, :n], h0[:, n:]
    for c in range(nc):
        rows = slice(c * nb, (c + 1) * nb)
        hin_ref[gi, 0, rows, :] = h_re
        hin_ref[gi, 1, rows, :] = h_im
        h_re, h_im = (ll_re * h_re - ll_im * h_im + x_ref[gi, 0, rows, :],
                      ll_re * h_im + ll_im * h_re + x_ref[gi, 1, rows, :])
    hl_ref[gi] = jnp.concatenate([h_re, h_im], axis=1)

    g1_re = (g_re * l_re_c - g_im * l_im_c).astype(BF16)
    g1_im = (g_re * l_im_c + g_im * l_re_c).astype(BF16)
    y_state = _bdot(hin_ref[gi, 0].astype(BF16), g1_re) - _bdot(hin_ref[gi, 1].astype(BF16), g1_im)

    y_ref[gi] = _bdot(ub, tm_ref[gi]) + y_state + u * d_ref[gi]


def _s5_kernel(*refs, gb, **kw):
    for gi in range(gb):
        _s5_group_body(gi, *refs, **kw)


def _s5(u_g, h0_g, prm, lp, lreal, nb, nc):
    g, m, k = u_g.shape
    gb = S5_GROUPS_PER_STEP
    blk = lambda *s: pl.BlockSpec((gb,) + s, lambda i: (i,) + (0,) * len(s))
    return pl.pallas_call(
        functools.partial(_s5_kernel, gb=gb, lp=lp, lreal=lreal, nb=nb, nc=nc),
        grid=(g // gb,),
        in_specs=[blk(m, k), blk(nb, 2 * SSM_STATE), blk(3, SSM_STATE), blk(SSM_STATE, 3),
                  blk(2, SSM_CH, SSM_STATE), blk(2, SSM_STATE, SSM_CH), blk(2, SSM_STATE, SSM_CH), blk(1, k)],
        out_specs=[blk(m, k), blk(nb, 2 * SSM_STATE)],
        out_shape=[jax.ShapeDtypeStruct((g, m, k), F32), jax.ShapeDtypeStruct((g, nb, 2 * SSM_STATE), F32)],
        scratch_shapes=[pltpu.VMEM((gb, k, k), BF16), pltpu.VMEM((gb, 2, m, SSM_STATE), F32),
                        pltpu.VMEM((gb, 2, m, SSM_STATE), F32)],
        compiler_params=_cparams(("parallel",), 32),
        name="s5_group",
    )(u_g, h0_g, prm["lrow"], prm["lcol"], prm["bt"], prm["bn"], prm["ct"], prm["dtile"][lp])


def _s5_mixer(u, h0, prm, lp, lreal):
    b, t, _ = u.shape
    nc = t // lreal
    ug = u.reshape(b, nc, lreal, SSM_GROUPS, SSM_CH)
    if lp != lreal:
        ug = jnp.pad(ug, ((0, 0), (0, 0), (0, lp - lreal), (0, 0), (0, 0)))
    ug = ug.transpose(3, 1, 0, 2, 4).reshape(SSM_GROUPS, nc * b, lp * SSM_CH)
    h0g = h0.transpose(1, 0, 3, 2).reshape(SSM_GROUPS, b, 2 * SSM_STATE)
    yg, hl = _s5(ug, h0g, prm, lp, lreal, b, nc)
    y = yg.reshape(SSM_GROUPS, nc, b, lp, SSM_CH)[:, :, :, :lreal].transpose(2, 1, 3, 0, 4)
    h_last = hl.reshape(SSM_GROUPS, b, 2, SSM_STATE).transpose(1, 0, 3, 2)
    return y.reshape(b, t, SSM_WIDTH), h_last


def _cmp_proj_kernel(*refs, n_in, rowwise):
    refs = refs[len(refs) - n_in - 3:]
    x_refs, (w1k_ref, w1v_ref, o_ref) = refs[:n_in], refs[n_in:]

    def rows_of(j, r):
        if rowwise:
            n = x_refs[0].shape[0] // (4 * CMP_STRIDE)
            parts = [x[pl.ds(4 * r + j, n, stride=4 * CMP_STRIDE), :] for x in x_refs]
            return parts[0] if n_in == 1 else jnp.concatenate(parts, axis=0)
        x = x_refs[j]
        return x[0, pl.ds(r, x.shape[1] // CMP_STRIDE, stride=CMP_STRIDE), :]

    for j in range(4):
        c = jnp.concatenate([rows_of(j, r) for r in range(CMP_STRIDE)], axis=1).astype(BF16)
        w = w1k_ref if j % 2 == 0 else w1v_ref
        o_ref[0, :, j * 2 * HEAD_DIM:(j + 1) * 2 * HEAD_DIM] = _bdot(c, w[...])


def _cmp_combine_kernel(h_ref, b1_ref, w2k_ref, w2v_ref, kn_ref, o_ref, *, nchunk):
    h = h_ref[0]
    keep = lax.broadcasted_iota(jnp.int32, (nchunk, 1), 0) < nchunk - 1
    for j in range(4):
        kvh, is_v = j // 2, j % 2
        lo = h[:, j * 2 * HEAD_DIM: j * 2 * HEAD_DIM + HEAD_DIM]
        hi = h[:, j * 2 * HEAD_DIM + HEAD_DIM:(j + 1) * 2 * HEAD_DIM]
        pre = lo + pltpu.roll(hi, nchunk - 1, axis=0) + b1_ref[is_v:is_v + 1, :]
        out = _bdot(_gelu(pre).astype(BF16), (w2v_ref if is_v else w2k_ref)[...])
        if not is_v:
            out = _rms(out, kn_ref[...])
        c0 = is_v * N_KV_HEADS * HEAD_DIM + kvh * HEAD_DIM
        o_ref[0, :, c0:c0 + HEAD_DIM] = jnp.where(keep, out, 0.0)


def _cmp_combine(h, prm):
    b, nchunk, w = h.shape
    return pl.pallas_call(
        functools.partial(_cmp_combine_kernel, nchunk=nchunk),
        grid=(b,),
        in_specs=[pl.BlockSpec((1, nchunk, w), lambda i: (i, 0, 0)), _resident((2, HEAD_DIM)),
                  _resident((HEAD_DIM, HEAD_DIM)), _resident((HEAD_DIM, HEAD_DIM)), _resident((1, HEAD_DIM))],
        out_specs=pl.BlockSpec((1, nchunk, 4 * HEAD_DIM), lambda i: (i, 0, 0)),
        out_shape=jax.ShapeDtypeStruct((b, nchunk, 4 * HEAD_DIM), F32),
        compiler_params=_cparams(("parallel",), 40),
        name="cmp_combine",
    )(h, prm["cmp_b1"], prm["cmp_w2_k"], prm["cmp_w2_v"], prm["k_norm_cmp"])


def _compress_prompt(kv_cmp, prm):
    b, t, w = kv_cmp.shape
    nchunk = t // CMP_STRIDE
    streams = w // HEAD_DIM
    h = pl.pallas_call(
        functools.partial(_cmp_proj_kernel, n_in=streams, rowwise=False),
        grid=(b,),
        in_specs=[pl.BlockSpec((1, t, HEAD_DIM), functools.partial(lambda i, j: (i, 0, j), j=j))
                  for j in range(streams)] + [
                  _resident((CMP_STRIDE * HEAD_DIM, 2 * HEAD_DIM)), _resident((CMP_STRIDE * HEAD_DIM, 2 * HEAD_DIM))],
        out_specs=pl.BlockSpec((1, nchunk, 8 * HEAD_DIM), lambda i: (i, 0, 0)),
        out_shape=jax.ShapeDtypeStruct((b, nchunk, 8 * HEAD_DIM), F32),
        compiler_params=_cparams(("parallel",), 40),
        name="cmp_proj_prompt",
    )(*([kv_cmp] * streams), prm["cmp_w1_k"], prm["cmp_w1_v"])
    return _cmp_combine(h, prm)


PAGES_PER_STEP = 16


def _rowwise_pages(cache):
    return cache.reshape(cache.shape[0], cache.shape[1], PAGE_SIZE * 2 * N_KV_HEADS, HEAD_DIM)


def _compress_sample(cache_cmp, layer, page_table, prm):
    bsz, n_pages = page_table.shape
    chunks_per_page = PAGE_SIZE // CMP_STRIDE
    pages = _rowwise_pages(cache_cmp)
    steps = n_pages // PAGES_PER_STEP

    def page_spec(p):
        return pl.BlockSpec((None, None, 4 * PAGE_SIZE, HEAD_DIM),
                            lambda b, s, pt: (layer, pt[b * n_pages + s * PAGES_PER_STEP + p], 0, 0))

    rows = PAGES_PER_STEP * chunks_per_page
    h = pl.pallas_call(
        functools.partial(_cmp_proj_kernel, n_in=PAGES_PER_STEP, rowwise=True),
        grid_spec=pltpu.PrefetchScalarGridSpec(
            num_scalar_prefetch=1,
            grid=(bsz, steps),
            in_specs=[page_spec(p) for p in range(PAGES_PER_STEP)] + [
                pl.BlockSpec((CMP_STRIDE * HEAD_DIM, 2 * HEAD_DIM), lambda b, s, pt: (0, 0)),
                pl.BlockSpec((CMP_STRIDE * HEAD_DIM, 2 * HEAD_DIM), lambda b, s, pt: (0, 0))],
            out_specs=pl.BlockSpec((1, rows, 8 * HEAD_DIM), lambda b, s, pt: (b, s, 0)),
        ),
        out_shape=jax.ShapeDtypeStruct((bsz, n_pages * chunks_per_page, 8 * HEAD_DIM), F32),
        compiler_params=_cparams(("parallel", "parallel"), 40),
        name="cmp_proj_sample",
    )(page_table.reshape(-1), *([pages] * PAGES_PER_STEP), prm["cmp_w1_k"], prm["cmp_w1_v"])
    return _cmp_combine(h, prm)


def _cmp_select_kernel(q_ref, ck_ref, cv_ref, o_ref, rank_ref, *idx_ref, tq, nc, nbp, nblk, qpos0):
    kvh = pl.program_id(1)
    qpos = qpos0 + pl.program_id(2) * tq + lax.broadcasted_iota(jnp.int32, (tq, 1), 0)
    start = lax.broadcasted_iota(jnp.int32, (1, nc), 1) * CMP_STRIDE
    valid = (start + (CMP_LEN - 1)) <= qpos
    dist = (qpos - start).astype(F32) - (CMP_LEN - 1) / 2
    ck = ck_ref[0].astype(BF16)
    cv = cv_ref[0].astype(BF16)
    psum = jnp.zeros((tq, nc), F32)
    for g in range(GQA):
        sl = slice(g * HEAD_DIM, (g + 1) * HEAD_DIM)
        s = _bdot_nt(q_ref[0, :, sl], ck) - _head_slope(kvh, g) * dist
        _, e, den = _softmax_parts(s, valid)
        p = e / jnp.maximum(den, F32_TINY)
        o_ref[0, :, sl] = _bdot(p.astype(BF16), cv)
        psum = psum + p

    per_shift = (SEL_LEN // CMP_STRIDE).bit_length() - 1
    ci = lax.broadcasted_iota(jnp.int32, (nc, nbp), 0)
    bj = lax.broadcasted_iota(jnp.int32, (nc, nbp), 1)
    overlap = (jnp.where(ci >> per_shift == bj, 1.0, 0.0)
               + jnp.where((ci + 1) >> per_shift == bj, 1.0, 0.0)).astype(BF16)
    score = _dot_small_int(psum, overlap)

    j = lax.broadcasted_iota(jnp.int32, (1, nbp), 1)
    qb = qpos >> SEL_SHIFT
    forced = (j == 0) | (j == qb) | (j == qb - 1)
    causal = (j * SEL_LEN) <= qpos
    score = jnp.where(forced, FORCED_SCORE, jnp.where(causal, score, -1.0))
    score = jnp.where(j < nblk, score, -2.0)

    rank = jnp.zeros((tq, nbp), F32)
    for i in range(nblk):
        si = score[:, i:i + 1]
        rank = rank + jnp.where(j > i, jnp.where(si >= score, 1.0, 0.0), jnp.where(si > score, 1.0, 0.0))
    rank_ref[0, 0] = rank

    if idx_ref:
        jf = j.astype(F32)
        lane = lax.broadcasted_iota(jnp.int32, (tq, V7X_LANES), 1)
        idx = jnp.zeros((tq, V7X_LANES), F32)
        for r in range(N_SEL):
            col = jnp.sum(jnp.where(rank == float(r), jf, 0.0), axis=-1, keepdims=True)
            idx = jnp.where(lane == r, col, idx)
        idx_ref[0][0, 0] = idx.astype(jnp.int32)


def _cmp_select(q, ckv, tq, nblk, qpos0, want_idx):
    b, t, _ = q.shape
    nc = ckv.shape[1]
    nbp = -(-nblk // V7X_LANES) * V7X_LANES
    hw = GQA * HEAD_DIM
    out_specs = [pl.BlockSpec((1, tq, hw), lambda bi, k, i: (bi, i, k)),
                 pl.BlockSpec((1, 1, tq, nbp), lambda bi, k, i: (bi, k, i, 0))]
    out_shape = [jax.ShapeDtypeStruct((b, t, ATTN_WIDTH), F32),
                 jax.ShapeDtypeStruct((b, N_KV_HEADS, t, nbp), F32)]
    if want_idx:
        out_specs.append(pl.BlockSpec((1, 1, tq, V7X_LANES), lambda bi, k, i: (bi, k, i, 0)))
        out_shape.append(jax.ShapeDtypeStruct((b, N_KV_HEADS, t, V7X_LANES), jnp.int32))
    return pl.pallas_call(
        functools.partial(_cmp_select_kernel, tq=tq, nc=nc, nbp=nbp, nblk=nblk, qpos0=qpos0),
        grid=(b, N_KV_HEADS, t // tq),
        in_specs=[pl.BlockSpec((1, tq, hw), lambda bi, k, i: (bi, i, k)),
                  pl.BlockSpec((1, nc, HEAD_DIM), lambda bi, k, i: (bi, 0, k)),
                  pl.BlockSpec((1, nc, HEAD_DIM), lambda bi, k, i: (bi, 0, N_KV_HEADS + k))],
        out_specs=out_specs,
        out_shape=out_shape,
        compiler_params=_cparams(("parallel", "parallel", "parallel"), 40),
        name="cmp_select",
    )(q, ckv, ckv)


ATT_TQ = 128
SEL_TK = 512
FILL_ROWS = 256
POS_RADIX = 256
RATE_TERMS = 3
MASK_BIAS = 2.0 ** 100


def _key_features(kidx, nblk):
    lane = lax.broadcasted_iota(jnp.int32, (1, V7X_LANES), 1)
    radix_shift = POS_RADIX.bit_length() - 1
    digits = jnp.where(((lane - nblk) & 1) == 0, (kidx >> radix_shift).astype(F32),
                       (kidx & (POS_RADIX - 1)).astype(F32))
    feats = jnp.where(lane < nblk + 2 * RATE_TERMS, digits, 0.0)
    if nblk:
        feats = jnp.where(lane < nblk, jnp.where((kidx >> SEL_SHIFT) == lane, 1.0, 0.0), feats)
    return feats


def _query_features(kvh, g, rank, nblk, rows):
    lane = lax.broadcasted_iota(jnp.int32, (rows, V7X_LANES), 1)
    feats = jnp.zeros((rows, V7X_LANES), F32)
    if nblk:
        feats = jnp.where((lane < nblk) & (rank >= float(N_SEL)), -MASK_BIAS, 0.0)
    for i, (c0, c1) in enumerate(zip(_bf16_terms(_alibi_slope(0, g), RATE_TERMS),
                                     _bf16_terms(_alibi_slope(1, g), RATE_TERMS))):
        c = jnp.where(kvh == 0, c0, c1)
        feats = jnp.where(lane == nblk + 2 * i, c * POS_RADIX, feats)
        feats = jnp.where(lane == nblk + 2 * i + 1, c, feats)
    return feats


def _fill_key_value_scratch(k_ref, v_ref, kf_ref, vo_ref, nblk):
    def chunk(c, carry):
        r0 = pl.multiple_of(c * FILL_ROWS, FILL_ROWS)
        rows = pl.ds(r0, FILL_ROWS)
        kidx = r0 + lax.broadcasted_iota(jnp.int32, (FILL_ROWS, 1), 0)
        kf_ref[rows, :HEAD_DIM] = k_ref[0, rows, :].astype(BF16)
        kf_ref[rows, HEAD_DIM:] = _key_features(kidx, nblk).astype(BF16)
        vo_ref[rows, :HEAD_DIM] = v_ref[0, rows, :].astype(BF16)
        vo_ref[rows, HEAD_DIM:] = jnp.ones((FILL_ROWS, V7X_LANES), BF16)
        return carry

    lax.fori_loop(0, k_ref.shape[1] // FILL_ROWS, chunk, 0)


def _attn_out(acc):
    return acc[:, :HEAD_DIM] / jnp.maximum(acc[:, HEAD_DIM:HEAD_DIM + 1], F32_TINY)


def _sel_prompt_kernel(q_ref, k_ref, v_ref, rank_ref, o_ref, kf_ref, vo_ref, qf_ref, m_ref, acc_ref, *, nblk):
    kvh, i = pl.program_id(1), pl.program_id(2)

    @pl.when(i == 0)
    def _():
        _fill_key_value_scratch(k_ref, v_ref, kf_ref, vo_ref, nblk)

    q0 = i * ATT_TQ
    qpos = q0 + (lax.broadcasted_iota(jnp.int32, (GQA * ATT_TQ, 1), 0) & (ATT_TQ - 1))
    rank = rank_ref[0, 0]
    for g in range(GQA):
        rows = slice(g * ATT_TQ, (g + 1) * ATT_TQ)
        qf_ref[rows, :HEAD_DIM] = q_ref[0, :, g * HEAD_DIM:(g + 1) * HEAD_DIM]
        qf_ref[rows, HEAD_DIM:] = _query_features(kvh, g, rank, nblk, ATT_TQ).astype(BF16)
    m_ref[...] = jnp.full(m_ref.shape, NEG_INF, F32)
    acc_ref[...] = jnp.zeros(acc_ref.shape, F32)

    def tile(jt, carry):
        k0 = pl.multiple_of(jt * SEL_TK, SEL_TK)
        causal = (k0 + lax.broadcasted_iota(jnp.int32, (1, SEL_TK), 1)) <= qpos
        s = jnp.where(causal, _bdot_nt(qf_ref[...], kf_ref[pl.ds(k0, SEL_TK), :]), NEG_INF)
        _flash_step(m_ref, acc_ref, 0, s, vo_ref[pl.ds(k0, SEL_TK), :])
        return carry

    lax.fori_loop(0, (q0 + ATT_TQ + SEL_TK - 1) // SEL_TK, tile, 0)
    out = _attn_out(acc_ref[0])
    for g in range(GQA):
        o_ref[0, :, g * HEAD_DIM:(g + 1) * HEAD_DIM] = out[g * ATT_TQ:(g + 1) * ATT_TQ]


def _sel_prompt(q, kv_slc, rank, nblk):
    b, t, _ = q.shape
    nbp = rank.shape[-1]
    hw = GQA * HEAD_DIM
    assert nbp == V7X_LANES and nblk + 2 * RATE_TERMS <= V7X_LANES and t <= POS_RADIX * POS_RADIX
    return pl.pallas_call(
        functools.partial(_sel_prompt_kernel, nblk=nblk),
        grid=(b, N_KV_HEADS, t // ATT_TQ),
        in_specs=[pl.BlockSpec((1, ATT_TQ, hw), lambda bi, k, i: (bi, i, k)),
                  pl.BlockSpec((1, t, HEAD_DIM), lambda bi, k, i: (bi, 0, 2 * k)),
                  pl.BlockSpec((1, t, HEAD_DIM), lambda bi, k, i: (bi, 0, 2 * k + 1)),
                  pl.BlockSpec((1, 1, ATT_TQ, nbp), lambda bi, k, i: (bi, k, i, 0))],
        out_specs=pl.BlockSpec((1, ATT_TQ, hw), lambda bi, k, i: (bi, i, k)),
        out_shape=jax.ShapeDtypeStruct((b, t, ATTN_WIDTH), F32),
        scratch_shapes=[pltpu.VMEM((t, 2 * HEAD_DIM), BF16), pltpu.VMEM((t, 2 * HEAD_DIM), BF16),
                        pltpu.VMEM((GQA * ATT_TQ, 2 * HEAD_DIM), BF16),
                        pltpu.VMEM((1, GQA * ATT_TQ, 1), F32), pltpu.VMEM((1, GQA * ATT_TQ, 2 * HEAD_DIM), F32)],
        compiler_params=_cparams(("parallel", "parallel", "arbitrary"), 32),
        name="sel_prompt",
    )(q, kv_slc, kv_slc, rank)


def _win_prompt_kernel(q_ref, k_ref, v_ref, o_ref, kf_ref, vo_ref):
    kvh, i = pl.program_id(1), pl.program_id(2)

    @pl.when(i == 0)
    def _():
        _fill_key_value_scratch(k_ref, v_ref, kf_ref, vo_ref, 0)

    span = WINDOW + WIN_QBLK
    qpos = i * WIN_QBLK + (lax.broadcasted_iota(jnp.int32, (GQA * WIN_QBLK, 1), 0) & (WIN_QBLK - 1))
    k0 = pl.multiple_of(jnp.maximum(i * WIN_QBLK - WINDOW, 0), WIN_QBLK)
    d = qpos - (k0 + lax.broadcasted_iota(jnp.int32, (1, span), 1))
    qf = jnp.concatenate(
        [jnp.concatenate([q_ref[0, :, g * HEAD_DIM:(g + 1) * HEAD_DIM],
                          _query_features(kvh, g, None, 0, WIN_QBLK).astype(BF16)], axis=1) for g in range(GQA)],
        axis=0)
    s = jnp.where((d >= 0) & (d < WINDOW), _bdot_nt(qf, kf_ref[pl.ds(k0, span), :]), NEG_INF)
    e = jnp.exp2(s - jnp.max(s, axis=-1, keepdims=True))
    out = _attn_out(_bdot(e.astype(BF16), vo_ref[pl.ds(k0, span), :]))
    for g in range(GQA):
        o_ref[0, :, g * HEAD_DIM:(g + 1) * HEAD_DIM] = out[g * WIN_QBLK:(g + 1) * WIN_QBLK]


def _win_prompt(q, kv_win):
    b, t, _ = q.shape
    hw = GQA * HEAD_DIM
    assert t <= POS_RADIX * POS_RADIX
    return pl.pallas_call(
        _win_prompt_kernel,
        grid=(b, N_KV_HEADS, t // WIN_QBLK),
        in_specs=[pl.BlockSpec((1, WIN_QBLK, hw), lambda bi, k, i: (bi, i, k)),
                  pl.BlockSpec((1, t, HEAD_DIM), lambda bi, k, i: (bi, 0, 2 * k)),
                  pl.BlockSpec((1, t, HEAD_DIM), lambda bi, k, i: (bi, 0, 2 * k + 1))],
        out_specs=pl.BlockSpec((1, WIN_QBLK, hw), lambda bi, k, i: (bi, i, k)),
        out_shape=jax.ShapeDtypeStruct((b, t, ATTN_WIDTH), F32),
        scratch_shapes=[pltpu.VMEM((t, 2 * HEAD_DIM), BF16), pltpu.VMEM((t, 2 * HEAD_DIM), BF16)],
        compiler_params=_cparams(("parallel", "parallel", "arbitrary"), 32),
        name="win_prompt",
    )(q, kv_win, kv_win)


def _slope_col(kvh):
    g = lax.broadcasted_iota(jnp.int32, (GQA, 1), 0)
    col = jnp.zeros((GQA, 1), F32)
    for i in range(GQA):
        col = jnp.where(g == i, _alibi_slope(kvh, i), col)
    return col


def _stream_rows(ref, lead, stream, n):
    return ref[lead + (pl.ds(stream, n, stride=2 * N_KV_HEADS), slice(None))]


def _sel_sample_kernel(idx_ref, pt_ref, q_ref, *refs, t_len):
    past, new_ref, o_ref = refs[:N_KV_HEADS * N_SEL], refs[N_KV_HEADS * N_SEL], refs[N_KV_HEADS * N_SEL + 1]
    b, t = pl.program_id(0), pl.program_id(1)
    past_blocks = PAST_LEN // SEL_LEN
    lane = lax.broadcasted_iota(jnp.int32, (1, N_SEL * SEL_LEN), 1)
    for kvh in range(N_KV_HEADS):
        ks, vs = [], []
        base = jnp.zeros((1, N_SEL * SEL_LEN), jnp.int32)
        for r in range(N_SEL):
            blk = idx_ref[((b * N_KV_HEADS + kvh) * t_len + t) * N_SEL + r]
            is_new = blk >= past_blocks
            ref = past[kvh * N_SEL + r]
            ks.append(jnp.where(is_new, _stream_rows(new_ref, (0,), 2 * kvh, SEL_LEN),
                                _stream_rows(ref, (), 2 * kvh, SEL_LEN)).astype(BF16))
            vs.append(jnp.where(is_new, _stream_rows(new_ref, (0,), 2 * kvh + 1, SEL_LEN),
                                _stream_rows(ref, (), 2 * kvh + 1, SEL_LEN)).astype(BF16))
            base = jnp.where((lane >> SEL_SHIFT) == r, blk * SEL_LEN, base)
        d = (PAST_LEN + t) - (base + (lane & (SEL_LEN - 1)))
        rows = slice(kvh * GQA, (kvh + 1) * GQA)
        s = _bdot_nt(q_ref[0, 0, rows, :], jnp.concatenate(ks, axis=0)) - _slope_col(kvh) * d.astype(F32)
        _, e, z = _softmax_parts(s, d >= 0)
        o_ref[0, 0, rows, :] = _bdot(e.astype(BF16), jnp.concatenate(vs, axis=0)) / jnp.maximum(z, F32_TINY)


def _sel_sample(q_h, cache_slc, layer, page_table, idx, kv_new_rows):
    bsz, t_len = q_h.shape[:2]
    n_pages = page_table.shape[1]
    per_page = PAGE_SIZE // SEL_LEN
    past_blocks = PAST_LEN // SEL_LEN
    pages = _rowwise_pages(cache_slc)
    blk_rows = SEL_LEN * 2 * N_KV_HEADS

    def past_spec(kvh, r):
        def imap(b, t, idx_ref, pt_ref):
            blk = jnp.minimum(idx_ref[((b * N_KV_HEADS + kvh) * t_len + t) * N_SEL + r], past_blocks - 1)
            return (layer, pt_ref[b * n_pages + blk // per_page], blk % per_page, 0)
        return pl.BlockSpec((None, None, blk_rows, HEAD_DIM), imap)

    qspec = pl.BlockSpec((1, 1, N_HEADS, HEAD_DIM), lambda b, t, i_, p_: (b, t, 0, 0))
    return pl.pallas_call(
        functools.partial(_sel_sample_kernel, t_len=t_len),
        grid_spec=pltpu.PrefetchScalarGridSpec(
            num_scalar_prefetch=2,
            grid=(bsz, t_len),
            in_specs=[qspec] + [past_spec(kvh, r) for kvh in range(N_KV_HEADS) for r in range(N_SEL)]
            + [pl.BlockSpec((1, blk_rows, HEAD_DIM), lambda b, t, i_, p_: (b, 0, 0))],
            out_specs=qspec,
        ),
        out_shape=jax.ShapeDtypeStruct(q_h.shape, F32),
        compiler_params=_cparams(("parallel", "parallel"), 32),
        name="sel_sample",
    )(idx.reshape(-1), page_table.reshape(-1), q_h, *([pages] * (N_KV_HEADS * N_SEL)), kv_new_rows)


def _win_sample_kernel(q_ref, past_ref, new_ref, o_ref, *, t_len, t_pad, wbuf):
    t = lax.broadcasted_iota(jnp.int32, (t_len, 1), 0)
    d_past = wbuf + t - lax.broadcasted_iota(jnp.int32, (1, wbuf), 1)
    row_new = lax.broadcasted_iota(jnp.int32, (1, t_pad), 1)
    d_new = t - row_new
    valid_past = (d_past >= 0) & (d_past < WINDOW)
    valid_new = (d_new >= 0) & (d_new < WINDOW) & (row_new < t_len)
    for kvh in range(N_KV_HEADS):
        kp = _stream_rows(past_ref, (0,), 2 * kvh, wbuf).astype(BF16)
        vp = _stream_rows(past_ref, (0,), 2 * kvh + 1, wbuf).astype(BF16)
        kn = _stream_rows(new_ref, (0,), 2 * kvh, t_pad).astype(BF16)
        vn = _stream_rows(new_ref, (0,), 2 * kvh + 1, t_pad).astype(BF16)
        for g in range(GQA):
            q = q_ref[0, kvh, g]
            slope = _alibi_slope(kvh, g)
            m1, e1, z1 = _softmax_parts(_bdot_nt(q, kp) - slope * d_past.astype(F32), valid_past)
            m2, e2, z2 = _softmax_parts(_bdot_nt(q, kn) - slope * d_new.astype(F32), valid_new)
            m = jnp.maximum(m1, m2)
            a1 = jnp.where(z1 > 0.0, jnp.exp2(m1 - m), 0.0)
            a2 = jnp.where(z2 > 0.0, jnp.exp2(m2 - m), 0.0)
            num = a1 * _bdot(e1.astype(BF16), vp) + a2 * _bdot(e2.astype(BF16), vn)
            o_ref[0, kvh, g] = num / jnp.maximum(a1 * z1 + a2 * z2, F32_TINY)


def _win_sample(q_g, win_rows, new_rows):
    bsz, _, _, t_len, _ = q_g.shape
    streams = 2 * N_KV_HEADS
    wbuf, t_pad = win_rows.shape[1] // streams, new_rows.shape[1] // streams
    qspec = pl.BlockSpec((1, N_KV_HEADS, GQA, t_len, HEAD_DIM), lambda b: (b, 0, 0, 0, 0))
    return pl.pallas_call(
        functools.partial(_win_sample_kernel, t_len=t_len, t_pad=t_pad, wbuf=wbuf),
        grid=(bsz,),
        in_specs=[qspec, pl.BlockSpec((1, wbuf * streams, HEAD_DIM), lambda b: (b, 0, 0)),
                  pl.BlockSpec((1, t_pad * streams, HEAD_DIM), lambda b: (b, 0, 0))],
        out_specs=qspec,
        out_shape=jax.ShapeDtypeStruct(q_g.shape, F32),
        compiler_params=_cparams(("parallel",), 16),
        name="win_sample",
    )(q_g, win_rows, new_rows)


def _mix_kernel(x_ref, y_ref, oc_ref, os_ref, ow_ref, gate_ref, wglu_ref, gs_ref, ga_ref, wout_ref, o_ref):
    z = _gelu(y_ref[...])
    z = z * jax.nn.sigmoid(_bdot(z.astype(BF16), wglu_ref[...]))
    gate = gate_ref[...]
    cols = []
    for h in range(N_HEADS):
        sl = slice(h * HEAD_DIM, (h + 1) * HEAD_DIM)
        cols.append(gate[:, h:h + 1] * oc_ref[:, sl] + gate[:, N_HEADS + h:N_HEADS + h + 1] * os_ref[:, sl]
                    + gate[:, 2 * N_HEADS + h:2 * N_HEADS + h + 1] * ow_ref[:, sl])
    attn = jnp.concatenate(cols, axis=1)
    hs = _rms(z, gs_ref[...]).astype(BF16)
    ha = _rms(attn, ga_ref[...]).astype(BF16)
    o_ref[...] = x_ref[...] + (_bdot(hs, wout_ref[:SSM_WIDTH, :]) + _bdot(ha, wout_ref[SSM_WIDTH:, :]))


def _mix(x, y, oc, osl, ow, gates, prm, tm):
    m = x.shape[0]
    row = lambda n: pl.BlockSpec((tm, n), lambda i: (i, 0))
    return pl.pallas_call(
        _mix_kernel,
        grid=(m // tm,),
        in_specs=[row(D_MODEL), row(SSM_WIDTH), row(ATTN_WIDTH), row(ATTN_WIDTH), row(ATTN_WIDTH), row(GATE_PAD),
                  _resident((SSM_WIDTH, SSM_WIDTH)), _resident((1, SSM_WIDTH)), _resident((1, ATTN_WIDTH)),
                  _resident((D_MODEL, D_MODEL))],
        out_specs=row(D_MODEL),
        out_shape=jax.ShapeDtypeStruct((m, D_MODEL), F32),
        compiler_params=_cparams(("parallel",), 48),
        name="mix_outproj",
    )(x, y, oc, osl, ow, gates, prm["ssm_w_glu"], prm["norm_ssm_out"], prm["norm_attn_out"], prm["w_out"])


FFN_TF = 512


def _ffn_kernel(x_ref, g_ref, wg_ref, wu_ref, wd_ref, o_ref, xn_ref, acc_ref):
    j = pl.program_id(1)

    @pl.when(j == 0)
    def _():
        xn_ref[...] = _rms(x_ref[...], g_ref[...]).astype(BF16)
        acc_ref[...] = jnp.zeros(acc_ref.shape, F32)

    xn = xn_ref[...]
    a = _bdot(xn, wg_ref[...])
    h = (a * jax.nn.sigmoid(a)) * _bdot(xn, wu_ref[...])
    acc_ref[...] += _bdot(h.astype(BF16), wd_ref[...])

    @pl.when(j == pl.num_programs(1) - 1)
    def _():
        o_ref[...] = x_ref[...] + acc_ref[...]


def _ffn(x, prm, tm):
    m = x.shape[0]
    return pl.pallas_call(
        _ffn_kernel,
        grid=(m // tm, D_FF // FFN_TF),
        in_specs=[pl.BlockSpec((tm, D_MODEL), lambda i, j: (i, 0)),
                  pl.BlockSpec((1, D_MODEL), lambda i, j: (0, 0)),
                  pl.BlockSpec((D_MODEL, FFN_TF), lambda i, j: (0, j)),
                  pl.BlockSpec((D_MODEL, FFN_TF), lambda i, j: (0, j)),
                  pl.BlockSpec((FFN_TF, D_MODEL), lambda i, j: (j, 0))],
        out_specs=pl.BlockSpec((tm, D_MODEL), lambda i, j: (i, 0)),
        out_shape=jax.ShapeDtypeStruct((m, D_MODEL), F32),
        scratch_shapes=[pltpu.VMEM((tm, D_MODEL), BF16), pltpu.VMEM((tm, D_MODEL), F32)],
        compiler_params=_cparams(("parallel", "arbitrary"), 48),
        name="ffn",
    )(x, prm["norm_ffn"], prm["w_ffn_gate"], prm["w_ffn_up"], prm["w_ffn_down"])


S5_CHUNK = 64
S5_GROUPS_PER_STEP = 2
SAMPLE_S5_PAD = 8


def _layer_params(l, w):
    kvh = jnp.arange(N_KV_HEADS)[:, None] * HEAD_DIM + jnp.arange(HEAD_DIM)[None, :]
    base = D_MODEL
    cols = [jnp.arange(D_MODEL)]
    for br in range(3):
        k0, v0 = base + 2 * br * KV_WIDTH, base + (2 * br + 1) * KV_WIDTH
        cols.append(jnp.stack([k0 + kvh, v0 + kvh], axis=1).reshape(-1))
    g0 = base + 6 * KV_WIDTH
    cols.append((g0 + jnp.arange(N_HEADS)[None, :] * 3 + jnp.arange(3)[:, None]).reshape(-1))
    w_in = jnp.take(w["w_in"][l], jnp.concatenate(cols), axis=1)
    w_in = jnp.pad(w_in, ((0, 0), (0, IN_WIDTH_PAD - w_in.shape[1]))).astype(BF16)

    def w1cat(w1):
        half = CMP_STRIDE * HEAD_DIM
        return jnp.concatenate([w1[:half], w1[half:]], axis=1).astype(BF16)

    row = lambda v: v.reshape(1, -1)
    a_re, a_im = w["ssm_a_re"][l], w["ssm_a_im"][l]
    log_dt = jnp.broadcast_to(w["ssm_log_dt"][l][:, None], a_re.shape)
    lrow = jnp.stack([a_re, a_im, log_dt], axis=1)
    d = w["ssm_d"][l]
    return {
        "w_in": w_in, "norm_mix": row(w["norm_mix"][l]),
        "q_norm": row(w["q_norm"][l]), "k_norm_slc": row(w["k_norm_slc"][l]),
        "k_norm_win": row(w["k_norm_win"][l]), "k_norm_cmp": row(w["k_norm_cmp"][l]),
        "lrow": lrow, "lcol": lrow.transpose(0, 2, 1),
        "bt": jnp.stack([w["ssm_b_re"][l], w["ssm_b_im"][l]], axis=1).transpose(0, 1, 3, 2),
        "bn": jnp.stack([w["ssm_b_re"][l], w["ssm_b_im"][l]], axis=1),
        "ct": jnp.stack([w["ssm_c_re"][l], w["ssm_c_im"][l]], axis=1).transpose(0, 1, 3, 2),
        "dtile": {lp: jnp.tile(d, (1, lp))[:, None, :] for lp in (S5_CHUNK, SAMPLE_S5_PAD)},
        "ssm_w_glu": w["ssm_w_glu"][l].astype(BF16),
        "cmp_w1_k": w1cat(w["cmp_w1_k"][l]), "cmp_w1_v": w1cat(w["cmp_w1_v"][l]),
        "cmp_b1": jnp.stack([w["cmp_b1_k"][l], w["cmp_b1_v"][l]]),
        "cmp_w2_k": w["cmp_w2_k"][l].astype(BF16), "cmp_w2_v": w["cmp_w2_v"][l].astype(BF16),
        "norm_ssm_out": row(w["norm_ssm_out"][l]), "norm_attn_out": row(w["norm_attn_out"][l]),
        "w_out": w["w_out"][l].astype(BF16), "norm_ffn": row(w["norm_ffn"][l]),
        "w_ffn_gate": w["w_ffn_gate"][l].astype(BF16), "w_ffn_up": w["w_ffn_up"][l].astype(BF16),
        "w_ffn_down": w["w_ffn_down"][l].astype(BF16),
    }


def _kv_out(kv, b, t):
    return kv.reshape(b, t, N_KV_HEADS, 2, HEAD_DIM)


PROMPT_TM = 256
FFN_TM = 512


def _prompt_layer(x, prm):
    b, t, _ = x.shape
    m = b * t
    x2 = x.reshape(m, D_MODEL)
    u, q, kv_cmp, kv_slc, kv_win, gates = _inproj(x2, prm["norm_mix"], prm["w_in"], prm["q_norm"],
                                                  prm["k_norm_slc"], prm["k_norm_win"], PROMPT_TM)
    h0 = jnp.zeros((b, SSM_GROUPS, SSM_STATE, 2), F32)
    y, h_last = _s5_mixer(u.reshape(b, t, SSM_WIDTH), h0, prm, S5_CHUNK, S5_CHUNK)
    q3 = q.reshape(b, t, ATTN_WIDTH)
    kv_cmp3, kv_slc3, kv_win3 = (a.reshape(b, t, 2 * KV_WIDTH) for a in (kv_cmp, kv_slc, kv_win))
    ckv = _compress_prompt(kv_cmp3, prm)
    o_cmp, rank = _cmp_select(q3, ckv, 256, t // SEL_LEN, 0, False)
    o_slc = _sel_prompt(q3, kv_slc3, rank, t // SEL_LEN)
    o_win = _win_prompt(q3, kv_win3)
    x2 = _mix(x2, y.reshape(m, SSM_WIDTH), o_cmp.reshape(m, -1), o_slc.reshape(m, -1), o_win.reshape(m, -1),
              gates, prm, PROMPT_TM)
    x2 = _ffn(x2, prm, FFN_TM)
    wbuf = min(WINDOW, PAST_LEN)
    win_state = jnp.pad(kv_win3, ((0, 0), (max(wbuf - t, 0), 0), (0, 0)))[:, -wbuf:]
    return (x2.reshape(b, t, D_MODEL), _kv_out(kv_cmp3, b, t), _kv_out(kv_slc3, b, t),
            _kv_out(win_state, b, wbuf), h_last)


def _sample_layer(x, prm, cache_cmp, cache_slc, layer, page_table, win_buf, h0):
    b, t, _ = x.shape
    m = b * t
    assert (PAST_LEN + t) // CMP_STRIDE == PAST_LEN // CMP_STRIDE and PAST_LEN % PAGE_SIZE == 0
    x2 = x.reshape(m, D_MODEL)
    u, q, kv_cmp, kv_slc, kv_win, gates = _inproj(x2, prm["norm_mix"], prm["w_in"], prm["q_norm"],
                                                  prm["k_norm_slc"], prm["k_norm_win"], m)
    y, h_last = _s5_mixer(u.reshape(b, t, SSM_WIDTH), h0, prm, SAMPLE_S5_PAD, t)
    q3 = q.reshape(b, t, ATTN_WIDTH)
    kv_cmp3, kv_slc3, kv_win3 = (a.reshape(b, t, 2 * KV_WIDTH) for a in (kv_cmp, kv_slc, kv_win))
    ckv = _compress_sample(cache_cmp, layer, page_table, prm)
    n_blocks = -(-(PAST_LEN + t) // SEL_LEN)
    o_cmp, _, idx = _cmp_select(q3, ckv, t, n_blocks, PAST_LEN, True)
    streams = 2 * N_KV_HEADS

    def stream_rows(kv3, t_pad):
        return jnp.pad(kv3.reshape(b, t * streams, HEAD_DIM), ((0, 0), (0, (t_pad - t) * streams), (0, 0)))

    o_slc = _sel_sample(q3.reshape(b, t, N_HEADS, HEAD_DIM), cache_slc, layer, page_table, idx[..., :N_SEL],
                        stream_rows(kv_slc3, SEL_LEN)).reshape(m, ATTN_WIDTH)
    wbuf = win_buf.shape[1]
    q5 = q3.reshape(b, t, N_KV_HEADS, GQA, HEAD_DIM)
    o_win = _win_sample(q5.transpose(0, 2, 3, 1, 4), win_buf.reshape(b, wbuf * streams, HEAD_DIM),
                        stream_rows(kv_win3, 8))
    o_win = o_win.transpose(0, 3, 1, 2, 4).reshape(m, ATTN_WIDTH)
    x2 = _mix(x2, y.reshape(m, SSM_WIDTH), o_cmp.reshape(m, -1), o_slc, o_win, gates, prm, m)
    x2 = _ffn(x2, prm, m)
    win_state = jnp.concatenate([win_buf, _kv_out(kv_win3, b, t)], axis=1)[:, -wbuf:]
    return (x2.reshape(b, t, D_MODEL), _kv_out(kv_cmp3, b, t), _kv_out(kv_slc3, b, t), win_state, h_last)


def kernel(x_prompt, x_sample, cache_cmp_kv, cache_slc_kv, state_win_kv, state_ssm, page_table, norm_mix, w_in, ssm_a_re, ssm_a_im, ssm_log_dt, ssm_b_re, ssm_b_im, ssm_c_re, ssm_c_im, ssm_d, ssm_w_glu, q_norm, k_norm_cmp, k_norm_slc, k_norm_win, cmp_w1_k, cmp_b1_k, cmp_w2_k, cmp_w1_v, cmp_b1_v, cmp_w2_v, norm_ssm_out, norm_attn_out, w_out, norm_ffn, w_ffn_gate, w_ffn_up, w_ffn_down):
    w = dict(norm_mix=norm_mix, w_in=w_in, ssm_a_re=ssm_a_re, ssm_a_im=ssm_a_im, ssm_log_dt=ssm_log_dt,
             ssm_b_re=ssm_b_re, ssm_b_im=ssm_b_im, ssm_c_re=ssm_c_re, ssm_c_im=ssm_c_im, ssm_d=ssm_d,
             ssm_w_glu=ssm_w_glu, q_norm=q_norm, k_norm_cmp=k_norm_cmp, k_norm_slc=k_norm_slc,
             k_norm_win=k_norm_win, cmp_w1_k=cmp_w1_k, cmp_b1_k=cmp_b1_k, cmp_w2_k=cmp_w2_k,
             cmp_w1_v=cmp_w1_v, cmp_b1_v=cmp_b1_v, cmp_w2_v=cmp_w2_v, norm_ssm_out=norm_ssm_out,
             norm_attn_out=norm_attn_out, w_out=w_out, norm_ffn=norm_ffn, w_ffn_gate=w_ffn_gate,
             w_ffn_up=w_ffn_up, w_ffn_down=w_ffn_down)
    y_p, y_s = x_prompt, x_sample
    outs_p, outs_s = [], []
    for l in range(DEPTH):
        prm = _layer_params(l, w)
        y_p, *rest = _prompt_layer(y_p, prm)
        outs_p.append(rest)
        y_s, *rest = _sample_layer(y_s, prm, cache_cmp_kv, cache_slc_kv, l, page_table,
                                   state_win_kv[l], state_ssm[l])
        outs_s.append(rest)
    stack = lambda outs, i: jnp.stack([o[i] for o in outs])
    return (y_p, y_s, stack(outs_p, 0), stack(outs_p, 1), stack(outs_p, 2), stack(outs_p, 3),
            stack(outs_s, 0), stack(outs_s, 1), stack(outs_s, 2), stack(outs_s, 3))
```

```python
import functools
import math

import jax
import jax.numpy as jnp
import numpy as np
from jax import lax
from jax.experimental import pallas as pl
from jax.experimental.pallas import tpu as pltpu

F32 = jnp.float32
BF16 = jnp.bfloat16

D_MODEL = 2048
DEPTH = 2
PAST_LEN = 16384
PAGE_SIZE = 128
SSM_WIDTH = 1024
ATTN_WIDTH = 1024
SSM_CH = 16
SSM_GROUPS = 64
SSM_STATE = 64
HEAD_DIM = 128
N_HEADS = 8
N_KV_HEADS = 2
GQA = 4
KV_WIDTH = 256
CMP_LEN = 32
CMP_STRIDE = 16
SEL_LEN = 64
SEL_SHIFT = SEL_LEN.bit_length() - 1
N_SEL = 16
WINDOW = 512
WIN_QBLK = 128
FORCED_SCORE = 1e4
D_FF = 5632
NORM_EPS = 1e-6
LOG2E = math.log2(math.e)
QSCALE = HEAD_DIM ** -0.5 * LOG2E
N_GATES = N_HEADS * 3
IN_WIDTH = SSM_WIDTH + ATTN_WIDTH + 6 * KV_WIDTH + N_GATES
F32_TINY = float(jnp.finfo(jnp.float32).tiny)
NEG_INF = float("-inf")

V7X_VMEM_BYTES = 64 * 1024 * 1024
V7X_LANES = 128


def _cparams(semantics, vmem_mib):
    assert vmem_mib * 1024 * 1024 < V7X_VMEM_BYTES
    return pltpu.CompilerParams(dimension_semantics=semantics, vmem_limit_bytes=vmem_mib * 1024 * 1024)


def _resident(shape):
    nd = len(shape)
    return pl.BlockSpec(shape, lambda *_: (0,) * nd, pipeline_mode=pl.Buffered(1))


def _resident_layer(shape, layer):
    nd = len(shape)
    return pl.BlockSpec((None,) + tuple(shape), lambda *_: (layer,) + (0,) * nd, pipeline_mode=pl.Buffered(1))


def _rms(x, g):
    return x * lax.rsqrt(jnp.mean(x * x, axis=-1, keepdims=True) + NORM_EPS) * g


def _gelu(x):
    c = math.sqrt(2.0 / math.pi)
    return x * (0.5 * (1.0 + jnp.tanh(c * (x + 0.044715 * (x * x * x)))))


def _bdot(a, b):
    return jnp.dot(a, b, preferred_element_type=F32)


def _bdot_nt(a, b):
    return lax.dot_general(a, b, (((1,), (1,)), ((), ())), preferred_element_type=F32)


def _split3(x):
    hi = x.astype(BF16)
    r1 = x - hi.astype(F32)
    mid = r1.astype(BF16)
    lo = (r1 - mid.astype(F32)).astype(BF16)
    return hi, mid, lo


def _dot_small_int(x, e):
    hi, mid, lo = _split3(x)
    return _bdot(hi, e) + _bdot(mid, e) + _bdot(lo, e)


def _dot_f32(a, b):
    ah, am, al = _split3(a)
    bh, bm, bl = _split3(b)
    return (_bdot(ah, bh) + (_bdot(ah, bm) + _bdot(am, bh))
            + (_bdot(ah, bl) + _bdot(al, bh) + _bdot(am, bm)))


def _softmax_parts(s, valid):
    s = jnp.where(valid, s, NEG_INF)
    m = jnp.max(s, axis=-1, keepdims=True)
    m = jnp.where(m == NEG_INF, 0.0, m)
    e = jnp.exp2(s - m)
    return m, e, jnp.sum(e, axis=-1, keepdims=True)


def _flash_step(m_ref, acc_ref, g, s, v_ones):
    m_old = m_ref[g]
    m_new = jnp.maximum(m_old, jnp.max(s, axis=-1, keepdims=True))
    m_safe = jnp.where(m_new == NEG_INF, 0.0, m_new)
    acc_ref[g] = jnp.exp2(m_old - m_safe) * acc_ref[g] + _bdot(jnp.exp2(s - m_safe).astype(BF16), v_ones)
    m_ref[g] = m_new


def _alibi_slope(kvh, g):
    return LOG2E * 2.0 ** (-8.0 * (kvh * GQA + g + 1) / N_HEADS)


def _bf16_terms(x, n=3):
    out = []
    for _ in range(n):
        hi = float(np.asarray(x, dtype=np.float32).astype(BF16))
        out.append(hi)
        x -= hi
    return out


def _head_slope(kvh, g):
    assert N_KV_HEADS == 2
    return jnp.where(kvh == 0, _alibi_slope(0, g), _alibi_slope(1, g))


def _inproj_kernel(x_ref, gmix_ref, w_ref, qn_ref, ksn_ref, kwn_ref,
                   u_ref, q_ref, cmp_ref, slc_ref, win_ref, gate_ref):
    xn = _rms(x_ref[...], gmix_ref[...]).astype(BF16)

    def proj(c0, c1):
        return _bdot(xn, w_ref[:, c0:c1])

    u_ref[...] = proj(0, SSM_WIDTH)
    q = proj(SSM_WIDTH, D_MODEL)
    for h in range(N_HEADS):
        sl = slice(h * HEAD_DIM, (h + 1) * HEAD_DIM)
        q_ref[:, sl] = (_rms(q[:, sl], qn_ref[...]) * QSCALE).astype(BF16)
    c0 = D_MODEL
    for ref, nref in ((cmp_ref, None), (slc_ref, ksn_ref), (win_ref, kwn_ref)):
        k = proj(c0, c0 + KV_WIDTH)
        v = proj(c0 + KV_WIDTH, c0 + 2 * KV_WIDTH)
        c0 += 2 * KV_WIDTH
        for kvh in range(N_KV_HEADS):
            src = slice(kvh * HEAD_DIM, (kvh + 1) * HEAD_DIM)
            dst = kvh * 2 * HEAD_DIM
            ref[:, dst:dst + HEAD_DIM] = k[:, src] if nref is None else _rms(k[:, src], nref[...])
            ref[:, dst + HEAD_DIM:dst + 2 * HEAD_DIM] = v[:, src]
    gate_ref[...] = jax.nn.sigmoid(proj(c0, c0 + N_GATES))


def _inproj(x, prm, tm):
    m = x.shape[0]
    row = lambda n: pl.BlockSpec((tm, n), lambda i: (i, 0))
    kvw = 2 * KV_WIDTH
    return pl.pallas_call(
        _inproj_kernel,
        grid=(m // tm,),
        in_specs=[row(D_MODEL), _resident((1, D_MODEL)), _resident_layer((D_MODEL, IN_WIDTH), prm["layer"]),
                  _resident((1, HEAD_DIM)), _resident((1, HEAD_DIM)), _resident((1, HEAD_DIM))],
        out_specs=[row(SSM_WIDTH), row(ATTN_WIDTH), row(kvw), row(kvw), row(kvw), row(N_GATES)],
        out_shape=[jax.ShapeDtypeStruct((m, SSM_WIDTH), F32), jax.ShapeDtypeStruct((m, ATTN_WIDTH), BF16),
                   jax.ShapeDtypeStruct((m, kvw), F32), jax.ShapeDtypeStruct((m, kvw), F32),
                   jax.ShapeDtypeStruct((m, kvw), F32), jax.ShapeDtypeStruct((m, N_GATES), F32)],
        compiler_params=_cparams(("parallel",), 48),
        name="inproj",
    )(x, prm["norm_mix"], prm["w_in"], prm["q_norm"], prm["k_norm_slc"], prm["k_norm_win"])


def _s5_group_body(gi, u_ref, h0_ref, lrow_ref, lcol_ref, bt_ref, bn_ref, ct_ref, d_ref,
                   y_ref, hl_ref, tm_ref, x_ref, hin_ref, *, lp, lreal, nb, nc):
    k = lp * SSM_CH
    n = SSM_STATE

    def lam_bar(a_re, a_im, log_dt):
        dt = jnp.exp(log_dt)
        e = jnp.exp(a_re * dt)
        return e * jnp.cos(a_im * dt), e * jnp.sin(a_im * dt)

    def zoh_coef(l_re, l_im, a_re, a_im):
        den = a_re * a_re + a_im * a_im
        x_re = l_re - 1.0
        return (x_re * a_re + l_im * a_im) / den, (l_im * a_re - x_re * a_im) / den

    lc = lcol_ref[gi]
    a_re_c, a_im_c = lc[:, 0:1], lc[:, 1:2]
    l_re_c, l_im_c = lam_bar(a_re_c, a_im_c, lc[:, 2:3])
    lr = lrow_ref[gi]
    a_re_r, a_im_r = lr[0:1], lr[1:2]
    l_re_r, l_im_r = lam_bar(a_re_r, a_im_r, lr[2:3])

    def pow_table(e):
        p_re = jnp.ones((n, k), F32)
        p_im = jnp.zeros((n, k), F32)
        s_re, s_im = l_re_c, l_im_c
        for j in range(max(lp.bit_length() - 1, 1)):
            bit = ((e >> j) & 1) == 1
            m_re = jnp.where(bit, s_re, 1.0)
            m_im = jnp.where(bit, s_im, 0.0)
            p_re, p_im = p_re * m_re - p_im * m_im, p_re * m_im + p_im * m_re
            s_re, s_im = s_re * s_re - s_im * s_im, 2.0 * s_re * s_im
        return p_re, p_im

    lane = lax.broadcasted_iota(jnp.int32, (1, k), 1)
    tau = lane >> (SSM_CH.bit_length() - 1)
    ch_onehot = jnp.where((lax.broadcasted_iota(jnp.int32, (SSM_CH, k), 1) & (SSM_CH - 1))
                          == lax.broadcasted_iota(jnp.int32, (SSM_CH, k), 0), 1.0, 0.0).astype(BF16)

    p_re, p_im = pow_table(tau)
    c_re = _dot_small_int(ct_ref[gi, 0], ch_onehot)
    c_im = _dot_small_int(ct_ref[gi, 1], ch_onehot)
    g_re = c_re * p_re - c_im * p_im
    g_im = c_re * p_im + c_im * p_re

    cf_re_r, cf_im_r = zoh_coef(l_re_r, l_im_r, a_re_r, a_im_r)
    bbt_re = cf_re_r * bt_ref[gi, 0] - cf_im_r * bt_ref[gi, 1]
    bbt_im = cf_re_r * bt_ref[gi, 1] + cf_im_r * bt_ref[gi, 0]
    a = _dot_f32(bbt_re, g_re) - _dot_f32(bbt_im, g_im)

    lane16 = lax.broadcasted_iota(jnp.int32, (SSM_CH, k), 1)
    for s in range(lp):
        blk = a if s == 0 else jnp.where(lane16 >= SSM_CH * s, pltpu.roll(a, SSM_CH * s, axis=1), 0.0)
        tm_ref[gi, SSM_CH * s:SSM_CH * (s + 1), :] = blk.astype(BF16)

    cf_re_c, cf_im_c = zoh_coef(l_re_c, l_im_c, a_re_c, a_im_c)
    bbn_re = cf_re_c * bn_ref[gi, 0] - cf_im_c * bn_ref[gi, 1]
    bbn_im = cf_re_c * bn_ref[gi, 1] + cf_im_c * bn_ref[gi, 0]
    bb_re = _dot_small_int(bbn_re, ch_onehot)
    bb_im = _dot_small_int(bbn_im, ch_onehot)
    rev = (lreal - 1) - tau
    q_re, q_im = pow_table(jnp.maximum(rev, 0))
    live = rev >= 0
    wx_re = jnp.where(live, q_re * bb_re - q_im * bb_im, 0.0).astype(BF16)
    wx_im = jnp.where(live, q_re * bb_im + q_im * bb_re, 0.0).astype(BF16)

    u = u_ref[gi]
    ub = u.astype(BF16)
    x_ref[gi, 0] = _bdot_nt(ub, wx_re)
    x_ref[gi, 1] = _bdot_nt(ub, wx_im)

    ll_re, ll_im = l_re_r, l_im_r
    for _ in range(lreal.bit_length() - 1):
        ll_re, ll_im = ll_re * ll_re - ll_im * ll_im, 2.0 * ll_re * ll_im
    h0 = h0_ref[gi]
    h_re, h_im = h0[:, :n], h0[:, n:]
    for c in range(nc):
        rows = slice(c * nb, (c + 1) * nb)
        hin_ref[gi, 0, rows, :] = h_re
        hin_ref[gi, 1, rows, :] = h_im
        h_re, h_im = (ll_re * h_re - ll_im * h_im + x_ref[gi, 0, rows, :],
                      ll_re * h_im + ll_im * h_re + x_ref[gi, 1, rows, :])
    hl_ref[gi] = jnp.concatenate([h_re, h_im], axis=1)

    g1_re = (g_re * l_re_c - g_im * l_im_c).astype(BF16)
    g1_im = (g_re * l_im_c + g_im * l_re_c).astype(BF16)
    y_state = _bdot(hin_ref[gi, 0].astype(BF16), g1_re) - _bdot(hin_ref[gi, 1].astype(BF16), g1_im)

    y_ref[gi] = _bdot(ub, tm_ref[gi]) + y_state + u * d_ref[gi]


def _s5_kernel(*refs, gb, **kw):
    for gi in range(gb):
        _s5_group_body(gi, *refs, **kw)


def _s5(u_g, h0_g, prm, lp, lreal, nb, nc):
    g, m, k = u_g.shape
    gb = S5_GROUPS_PER_STEP
    blk = lambda *s: pl.BlockSpec((gb,) + s, lambda i: (i,) + (0,) * len(s))
    return pl.pallas_call(
        functools.partial(_s5_kernel, gb=gb, lp=lp, lreal=lreal, nb=nb, nc=nc),
        grid=(g // gb,),
        in_specs=[blk(m, k), blk(nb, 2 * SSM_STATE), blk(3, SSM_STATE), blk(SSM_STATE, 3),
                  blk(2, SSM_CH, SSM_STATE), blk(2, SSM_STATE, SSM_CH), blk(2, SSM_STATE, SSM_CH), blk(1, k)],
        out_specs=[blk(m, k), blk(nb, 2 * SSM_STATE)],
        out_shape=[jax.ShapeDtypeStruct((g, m, k), F32), jax.ShapeDtypeStruct((g, nb, 2 * SSM_STATE), F32)],
        scratch_shapes=[pltpu.VMEM((gb, k, k), BF16), pltpu.VMEM((gb, 2, m, SSM_STATE), F32),
                        pltpu.VMEM((gb, 2, m, SSM_STATE), F32)],
        compiler_params=_cparams(("parallel",), 32),
        name="s5_group",
    )(u_g, h0_g, prm["lrow"], prm["lcol"], prm["bt"], prm["bn"], prm["ct"], prm["dtile"][lp])


def _s5_mixer(u, h0, prm, lp, lreal):
    b, t, _ = u.shape
    nc = t // lreal
    ug = u.reshape(b, nc, lreal, SSM_GROUPS, SSM_CH)
    if lp != lreal:
        ug = jnp.pad(ug, ((0, 0), (0, 0), (0, lp - lreal), (0, 0), (0, 0)))
    ug = ug.transpose(3, 1, 0, 2, 4).reshape(SSM_GROUPS, nc * b, lp * SSM_CH)
    h0g = h0.transpose(1, 0, 3, 2).reshape(SSM_GROUPS, b, 2 * SSM_STATE)
    yg, hl = _s5(ug, h0g, prm, lp, lreal, b, nc)
    y = yg.reshape(SSM_GROUPS, nc, b, lp, SSM_CH)[:, :, :, :lreal].transpose(2, 1, 3, 0, 4)
    h_last = hl.reshape(SSM_GROUPS, b, 2, SSM_STATE).transpose(1, 0, 3, 2)
    return y.reshape(b, t, SSM_WIDTH), h_last


def _cmp_proj_kernel(*refs, n_in, rowwise):
    refs = refs[len(refs) - n_in - 3:]
    x_refs, (w1k_ref, w1v_ref, o_ref) = refs[:n_in], refs[n_in:]

    def rows_of(j, r):
        if rowwise:
            n = x_refs[0].shape[0] // (4 * CMP_STRIDE)
            parts = [x[pl.ds(4 * r + j, n, stride=4 * CMP_STRIDE), :] for x in x_refs]
            return parts[0] if n_in == 1 else jnp.concatenate(parts, axis=0)
        x = x_refs[j]
        return x[0, pl.ds(r, x.shape[1] // CMP_STRIDE, stride=CMP_STRIDE), :]

    for j in range(4):
        c = jnp.concatenate([rows_of(j, r) for r in range(CMP_STRIDE)], axis=1).astype(BF16)
        w = w1k_ref if j % 2 == 0 else w1v_ref
        o_ref[0, :, j * 2 * HEAD_DIM:(j + 1) * 2 * HEAD_DIM] = _bdot(c, w[...])


def _cmp_combine_kernel(h_ref, b1_ref, w2k_ref, w2v_ref, kn_ref, o_ref, *, nchunk):
    h = h_ref[0]
    keep = lax.broadcasted_iota(jnp.int32, (nchunk, 1), 0) < nchunk - 1
    for j in range(4):
        kvh, is_v = j // 2, j % 2
        lo = h[:, j * 2 * HEAD_DIM: j * 2 * HEAD_DIM + HEAD_DIM]
        hi = h[:, j * 2 * HEAD_DIM + HEAD_DIM:(j + 1) * 2 * HEAD_DIM]
        pre = lo + pltpu.roll(hi, nchunk - 1, axis=0) + b1_ref[is_v:is_v + 1, :]
        out = _bdot(_gelu(pre).astype(BF16), (w2v_ref if is_v else w2k_ref)[...])
        if not is_v:
            out = _rms(out, kn_ref[...])
        c0 = is_v * N_KV_HEADS * HEAD_DIM + kvh * HEAD_DIM
        o_ref[0, :, c0:c0 + HEAD_DIM] = jnp.where(keep, out, 0.0)


def _cmp_combine(h, prm):
    b, nchunk, w = h.shape
    return pl.pallas_call(
        functools.partial(_cmp_combine_kernel, nchunk=nchunk),
        grid=(b,),
        in_specs=[pl.BlockSpec((1, nchunk, w), lambda i: (i, 0, 0)), _resident((2, HEAD_DIM)),
                  _resident((HEAD_DIM, HEAD_DIM)), _resident((HEAD_DIM, HEAD_DIM)), _resident((1, HEAD_DIM))],
        out_specs=pl.BlockSpec((1, nchunk, 4 * HEAD_DIM), lambda i: (i, 0, 0)),
        out_shape=jax.ShapeDtypeStruct((b, nchunk, 4 * HEAD_DIM), F32),
        compiler_params=_cparams(("parallel",), 40),
        name="cmp_combine",
    )(h, prm["cmp_b1"], prm["cmp_w2_k"], prm["cmp_w2_v"], prm["k_norm_cmp"])


def _compress_prompt(kv_cmp, prm):
    b, t, w = kv_cmp.shape
    nchunk = t // CMP_STRIDE
    streams = w // HEAD_DIM
    h = pl.pallas_call(
        functools.partial(_cmp_proj_kernel, n_in=streams, rowwise=False),
        grid=(b,),
        in_specs=[pl.BlockSpec((1, t, HEAD_DIM), functools.partial(lambda i, j: (i, 0, j), j=j))
                  for j in range(streams)] + [
                  _resident((CMP_STRIDE * HEAD_DIM, 2 * HEAD_DIM)), _resident((CMP_STRIDE * HEAD_DIM, 2 * HEAD_DIM))],
        out_specs=pl.BlockSpec((1, nchunk, 8 * HEAD_DIM), lambda i: (i, 0, 0)),
        out_shape=jax.ShapeDtypeStruct((b, nchunk, 8 * HEAD_DIM), F32),
        compiler_params=_cparams(("parallel",), 40),
        name="cmp_proj_prompt",
    )(*([kv_cmp] * streams), prm["cmp_w1_k"], prm["cmp_w1_v"])
    return _cmp_combine(h, prm)


PAGES_PER_STEP = 16


def _rowwise_pages(cache):
    return cache.reshape(cache.shape[0], cache.shape[1], PAGE_SIZE * 2 * N_KV_HEADS, HEAD_DIM)


def _compress_sample(cache_cmp, layer, page_table, prm):
    bsz, n_pages = page_table.shape
    chunks_per_page = PAGE_SIZE // CMP_STRIDE
    pages = _rowwise_pages(cache_cmp)
    steps = n_pages // PAGES_PER_STEP

    def page_spec(p):
        return pl.BlockSpec((None, None, 4 * PAGE_SIZE, HEAD_DIM),
                            lambda b, s, pt: (layer, pt[b * n_pages + s * PAGES_PER_STEP + p], 0, 0))

    rows = PAGES_PER_STEP * chunks_per_page
    h = pl.pallas_call(
        functools.partial(_cmp_proj_kernel, n_in=PAGES_PER_STEP, rowwise=True),
        grid_spec=pltpu.PrefetchScalarGridSpec(
            num_scalar_prefetch=1,
            grid=(bsz, steps),
            in_specs=[page_spec(p) for p in range(PAGES_PER_STEP)] + [
                pl.BlockSpec((CMP_STRIDE * HEAD_DIM, 2 * HEAD_DIM), lambda b, s, pt: (0, 0)),
                pl.BlockSpec((CMP_STRIDE * HEAD_DIM, 2 * HEAD_DIM), lambda b, s, pt: (0, 0))],
            out_specs=pl.BlockSpec((1, rows, 8 * HEAD_DIM), lambda b, s, pt: (b, s, 0)),
        ),
        out_shape=jax.ShapeDtypeStruct((bsz, n_pages * chunks_per_page, 8 * HEAD_DIM), F32),
        compiler_params=_cparams(("parallel", "parallel"), 40),
        name="cmp_proj_sample",
    )(page_table.reshape(-1), *([pages] * PAGES_PER_STEP), prm["cmp_w1_k"], prm["cmp_w1_v"])
    return _cmp_combine(h, prm)


def _cmp_select_kernel(q_ref, ck_ref, cv_ref, o_ref, rank_ref, *idx_ref, tq, nc, nbp, nblk, qpos0):
    kvh = pl.program_id(1)
    qpos = qpos0 + pl.program_id(2) * tq + lax.broadcasted_iota(jnp.int32, (tq, 1), 0)
    start = lax.broadcasted_iota(jnp.int32, (1, nc), 1) * CMP_STRIDE
    valid = (start + (CMP_LEN - 1)) <= qpos
    dist = (qpos - start).astype(F32) - (CMP_LEN - 1) / 2
    ck = ck_ref[0].astype(BF16)
    cv = cv_ref[0].astype(BF16)
    psum = jnp.zeros((tq, nc), F32)
    for g in range(GQA):
        sl = slice(g * HEAD_DIM, (g + 1) * HEAD_DIM)
        s = _bdot_nt(q_ref[0, :, sl], ck) - _head_slope(kvh, g) * dist
        _, e, den = _softmax_parts(s, valid)
        p = e / jnp.maximum(den, F32_TINY)
        o_ref[0, :, sl] = _bdot(p.astype(BF16), cv)
        psum = psum + p

    per_shift = (SEL_LEN // CMP_STRIDE).bit_length() - 1
    ci = lax.broadcasted_iota(jnp.int32, (nc, nbp), 0)
    bj = lax.broadcasted_iota(jnp.int32, (nc, nbp), 1)
    overlap = (jnp.where(ci >> per_shift == bj, 1.0, 0.0)
               + jnp.where((ci + 1) >> per_shift == bj, 1.0, 0.0)).astype(BF16)
    score = _dot_small_int(psum, overlap)

    j = lax.broadcasted_iota(jnp.int32, (1, nbp), 1)
    qb = qpos >> SEL_SHIFT
    forced = (j == 0) | (j == qb) | (j == qb - 1)
    causal = (j * SEL_LEN) <= qpos
    score = jnp.where(forced, FORCED_SCORE, jnp.where(causal, score, -1.0))
    score = jnp.where(j < nblk, score, -2.0)

    def ranks():
        rank = jnp.zeros((tq, nbp), F32)
        for i in range(nblk):
            si = score[:, i:i + 1]
            rank = rank + jnp.where(j > i, jnp.where(si >= score, 1.0, 0.0), jnp.where(si > score, 1.0, 0.0))
        return rank

    if idx_ref or qpos0 >= N_SEL * SEL_LEN:
        rank = ranks()
        rank_ref[0, 0] = rank
    else:
        crowded = qpos0 + (pl.program_id(2) + 1) * tq > N_SEL * SEL_LEN

        @pl.when(crowded)
        def _():
            rank_ref[0, 0] = ranks()

        @pl.when(jnp.logical_not(crowded))
        def _():
            rank_ref[0, 0] = jnp.zeros((tq, nbp), F32)

    if idx_ref:
        jf = j.astype(F32)
        lane = lax.broadcasted_iota(jnp.int32, (tq, V7X_LANES), 1)
        idx = jnp.zeros((tq, V7X_LANES), F32)
        for r in range(N_SEL):
            col = jnp.sum(jnp.where(rank == float(r), jf, 0.0), axis=-1, keepdims=True)
            idx = jnp.where(lane == r, col, idx)
        idx_ref[0][0, 0] = idx.astype(jnp.int32)


def _cmp_select(q, ckv, tq, nblk, qpos0, want_idx):
    b, t, _ = q.shape
    nc = ckv.shape[1]
    nbp = -(-nblk // V7X_LANES) * V7X_LANES
    hw = GQA * HEAD_DIM
    out_specs = [pl.BlockSpec((1, tq, hw), lambda bi, k, i: (bi, i, k)),
                 pl.BlockSpec((1, 1, tq, nbp), lambda bi, k, i: (bi, k, i, 0))]
    out_shape = [jax.ShapeDtypeStruct((b, t, ATTN_WIDTH), F32),
                 jax.ShapeDtypeStruct((b, N_KV_HEADS, t, nbp), F32)]
    if want_idx:
        out_specs.append(pl.BlockSpec((1, 1, tq, V7X_LANES), lambda bi, k, i: (bi, k, i, 0)))
        out_shape.append(jax.ShapeDtypeStruct((b, N_KV_HEADS, t, V7X_LANES), jnp.int32))
    return pl.pallas_call(
        functools.partial(_cmp_select_kernel, tq=tq, nc=nc, nbp=nbp, nblk=nblk, qpos0=qpos0),
        grid=(b, N_KV_HEADS, t // tq),
        in_specs=[pl.BlockSpec((1, tq, hw), lambda bi, k, i: (bi, i, k)),
                  pl.BlockSpec((1, nc, HEAD_DIM), lambda bi, k, i: (bi, 0, k)),
                  pl.BlockSpec((1, nc, HEAD_DIM), lambda bi, k, i: (bi, 0, N_KV_HEADS + k))],
        out_specs=out_specs,
        out_shape=out_shape,
        compiler_params=_cparams(("parallel", "parallel", "parallel"), 40),
        name="cmp_select",
    )(q, ckv, ckv)


ATT_TQ = 256
SEL_TK = 512
FILL_ROWS = 256
POS_RADIX = 256
RATE_TERMS = 3
MASK_BIAS = 2.0 ** 100


def _key_features(kidx, nblk):
    lane = lax.broadcasted_iota(jnp.int32, (1, V7X_LANES), 1)
    radix_shift = POS_RADIX.bit_length() - 1
    digits = jnp.where(((lane - nblk) & 1) == 0, (kidx >> radix_shift).astype(F32),
                       (kidx & (POS_RADIX - 1)).astype(F32))
    feats = jnp.where(lane < nblk + 2 * RATE_TERMS, digits, 0.0)
    if nblk:
        feats = jnp.where(lane < nblk, jnp.where((kidx >> SEL_SHIFT) == lane, 1.0, 0.0), feats)
    return feats


def _query_features(kvh, g, rank, nblk, rows):
    lane = lax.broadcasted_iota(jnp.int32, (rows, V7X_LANES), 1)
    feats = jnp.zeros((rows, V7X_LANES), F32)
    if nblk:
        feats = jnp.where((lane < nblk) & (rank >= float(N_SEL)), -MASK_BIAS, 0.0)
    for i, (c0, c1) in enumerate(zip(_bf16_terms(_alibi_slope(0, g), RATE_TERMS),
                                     _bf16_terms(_alibi_slope(1, g), RATE_TERMS))):
        c = jnp.where(kvh == 0, c0, c1)
        feats = jnp.where(lane == nblk + 2 * i, c * POS_RADIX, feats)
        feats = jnp.where(lane == nblk + 2 * i + 1, c, feats)
    return feats


def _fill_key_value_scratch(k_ref, v_ref, kf_ref, vo_ref, nblk):
    def chunk(c, carry):
        r0 = pl.multiple_of(c * FILL_ROWS, FILL_ROWS)
        rows = pl.ds(r0, FILL_ROWS)
        kidx = r0 + lax.broadcasted_iota(jnp.int32, (FILL_ROWS, 1), 0)
        kf_ref[rows, :HEAD_DIM] = k_ref[0, rows, :].astype(BF16)
        kf_ref[rows, HEAD_DIM:] = _key_features(kidx, nblk).astype(BF16)
        vo_ref[rows, :HEAD_DIM] = v_ref[0, rows, :].astype(BF16)
        vo_ref[rows, HEAD_DIM:] = jnp.ones((FILL_ROWS, V7X_LANES), BF16)
        return carry

    lax.fori_loop(0, k_ref.shape[1] // FILL_ROWS, chunk, 0)


def _attn_out(acc):
    return acc[:, :HEAD_DIM] / jnp.maximum(acc[:, HEAD_DIM:HEAD_DIM + 1], F32_TINY)


def _sel_prompt_kernel(q_ref, k_ref, v_ref, rank_ref, o_ref, kf_ref, vo_ref, qf_ref, m_ref, acc_ref, *, nblk):
    kvh, i = pl.program_id(1), pl.program_id(2)

    @pl.when(i == 0)
    def _():
        _fill_key_value_scratch(k_ref, v_ref, kf_ref, vo_ref, nblk)

    q0 = i * ATT_TQ
    qpos = q0 + (lax.broadcasted_iota(jnp.int32, (GQA * ATT_TQ, 1), 0) & (ATT_TQ - 1))
    rank = rank_ref[0, 0]
    for g in range(GQA):
        rows = slice(g * ATT_TQ, (g + 1) * ATT_TQ)
        qf_ref[rows, :HEAD_DIM] = q_ref[0, :, g * HEAD_DIM:(g + 1) * HEAD_DIM]
        qf_ref[rows, HEAD_DIM:] = _query_features(kvh, g, rank, nblk, ATT_TQ).astype(BF16)
    m_ref[...] = jnp.full(m_ref.shape, NEG_INF, F32)
    acc_ref[...] = jnp.zeros(acc_ref.shape, F32)

    def tile(jt, carry):
        k0 = pl.multiple_of(jt * SEL_TK, SEL_TK)
        causal = (k0 + lax.broadcasted_iota(jnp.int32, (1, SEL_TK), 1)) <= qpos
        s = jnp.where(causal, _bdot_nt(qf_ref[...], kf_ref[pl.ds(k0, SEL_TK), :]), NEG_INF)
        _flash_step(m_ref, acc_ref, 0, s, vo_ref[pl.ds(k0, SEL_TK), :])
        return carry

    lax.fori_loop(0, (q0 + ATT_TQ + SEL_TK - 1) // SEL_TK, tile, 0)
    out = _attn_out(acc_ref[0])
    for g in range(GQA):
        o_ref[0, :, g * HEAD_DIM:(g + 1) * HEAD_DIM] = out[g * ATT_TQ:(g + 1) * ATT_TQ]


def _sel_prompt(q, kv_slc, rank, nblk):
    b, t, _ = q.shape
    nbp = rank.shape[-1]
    hw = GQA * HEAD_DIM
    assert nbp == V7X_LANES and nblk + 2 * RATE_TERMS <= V7X_LANES and t <= POS_RADIX * POS_RADIX
    return pl.pallas_call(
        functools.partial(_sel_prompt_kernel, nblk=nblk),
        grid=(b, N_KV_HEADS, t // ATT_TQ),
        in_specs=[pl.BlockSpec((1, ATT_TQ, hw), lambda bi, k, i: (bi, i, k)),
                  pl.BlockSpec((1, t, HEAD_DIM), lambda bi, k, i: (bi, 0, 2 * k)),
                  pl.BlockSpec((1, t, HEAD_DIM), lambda bi, k, i: (bi, 0, 2 * k + 1)),
                  pl.BlockSpec((1, 1, ATT_TQ, nbp), lambda bi, k, i: (bi, k, i, 0))],
        out_specs=pl.BlockSpec((1, ATT_TQ, hw), lambda bi, k, i: (bi, i, k)),
        out_shape=jax.ShapeDtypeStruct((b, t, ATTN_WIDTH), F32),
        scratch_shapes=[pltpu.VMEM((t, 2 * HEAD_DIM), BF16), pltpu.VMEM((t, 2 * HEAD_DIM), BF16),
                        pltpu.VMEM((GQA * ATT_TQ, 2 * HEAD_DIM), BF16),
                        pltpu.VMEM((1, GQA * ATT_TQ, 1), F32), pltpu.VMEM((1, GQA * ATT_TQ, 2 * HEAD_DIM), F32)],
        compiler_params=_cparams(("parallel", "parallel", "arbitrary"), 32),
        name="sel_prompt",
    )(q, kv_slc, kv_slc, rank)


def _win_prompt_kernel(q_ref, k_ref, v_ref, o_ref, kf_ref, vo_ref):
    kvh, i = pl.program_id(1), pl.program_id(2)

    @pl.when(i == 0)
    def _():
        _fill_key_value_scratch(k_ref, v_ref, kf_ref, vo_ref, 0)

    span = WINDOW + WIN_QBLK
    qpos = i * WIN_QBLK + (lax.broadcasted_iota(jnp.int32, (GQA * WIN_QBLK, 1), 0) & (WIN_QBLK - 1))
    k0 = pl.multiple_of(jnp.maximum(i * WIN_QBLK - WINDOW, 0), WIN_QBLK)
    d = qpos - (k0 + lax.broadcasted_iota(jnp.int32, (1, span), 1))
    qf = jnp.concatenate(
        [jnp.concatenate([q_ref[0, :, g * HEAD_DIM:(g + 1) * HEAD_DIM],
                          _query_features(kvh, g, None, 0, WIN_QBLK).astype(BF16)], axis=1) for g in range(GQA)],
        axis=0)
    s = jnp.where((d >= 0) & (d < WINDOW), _bdot_nt(qf, kf_ref[pl.ds(k0, span), :]), NEG_INF)
    e = jnp.exp2(s - jnp.max(s, axis=-1, keepdims=True))
    out = _attn_out(_bdot(e.astype(BF16), vo_ref[pl.ds(k0, span), :]))
    for g in range(GQA):
        o_ref[0, :, g * HEAD_DIM:(g + 1) * HEAD_DIM] = out[g * WIN_QBLK:(g + 1) * WIN_QBLK]


def _win_prompt(q, kv_win):
    b, t, _ = q.shape
    hw = GQA * HEAD_DIM
    assert t <= POS_RADIX * POS_RADIX
    return pl.pallas_call(
        _win_prompt_kernel,
        grid=(b, N_KV_HEADS, t // WIN_QBLK),
        in_specs=[pl.BlockSpec((1, WIN_QBLK, hw), lambda bi, k, i: (bi, i, k)),
                  pl.BlockSpec((1, t, HEAD_DIM), lambda bi, k, i: (bi, 0, 2 * k)),
                  pl.BlockSpec((1, t, HEAD_DIM), lambda bi, k, i: (bi, 0, 2 * k + 1))],
        out_specs=pl.BlockSpec((1, WIN_QBLK, hw), lambda bi, k, i: (bi, i, k)),
        out_shape=jax.ShapeDtypeStruct((b, t, ATTN_WIDTH), F32),
        scratch_shapes=[pltpu.VMEM((t, 2 * HEAD_DIM), BF16), pltpu.VMEM((t, 2 * HEAD_DIM), BF16)],
        compiler_params=_cparams(("parallel", "parallel", "arbitrary"), 32),
        name="win_prompt",
    )(q, kv_win, kv_win)


def _slope_col(kvh):
    g = lax.broadcasted_iota(jnp.int32, (GQA, 1), 0)
    col = jnp.zeros((GQA, 1), F32)
    for i in range(GQA):
        col = jnp.where(g == i, _alibi_slope(kvh, i), col)
    return col


def _stream_rows(ref, lead, stream, n):
    return ref[lead + (pl.ds(stream, n, stride=2 * N_KV_HEADS), slice(None))]


def _sel_sample_kernel(idx_ref, pt_ref, q_ref, *refs, t_len):
    past, new_ref, o_ref = refs[:N_KV_HEADS * N_SEL], refs[N_KV_HEADS * N_SEL], refs[N_KV_HEADS * N_SEL + 1]
    b, t = pl.program_id(0), pl.program_id(1)
    past_blocks = PAST_LEN // SEL_LEN
    lane = lax.broadcasted_iota(jnp.int32, (1, N_SEL * SEL_LEN), 1)
    for kvh in range(N_KV_HEADS):
        ks, vs = [], []
        base = jnp.zeros((1, N_SEL * SEL_LEN), jnp.int32)
        for r in range(N_SEL):
            blk = idx_ref[((b * N_KV_HEADS + kvh) * t_len + t) * N_SEL + r]
            is_new = blk >= past_blocks
            ref = past[kvh * N_SEL + r]
            ks.append(jnp.where(is_new, _stream_rows(new_ref, (0,), 2 * kvh, SEL_LEN),
                                _stream_rows(ref, (), 2 * kvh, SEL_LEN)).astype(BF16))
            vs.append(jnp.where(is_new, _stream_rows(new_ref, (0,), 2 * kvh + 1, SEL_LEN),
                                _stream_rows(ref, (), 2 * kvh + 1, SEL_LEN)).astype(BF16))
            base = jnp.where((lane >> SEL_SHIFT) == r, blk * SEL_LEN, base)
        d = (PAST_LEN + t) - (base + (lane & (SEL_LEN - 1)))
        rows = slice(kvh * GQA, (kvh + 1) * GQA)
        s = _bdot_nt(q_ref[0, 0, rows, :], jnp.concatenate(ks, axis=0)) - _slope_col(kvh) * d.astype(F32)
        _, e, z = _softmax_parts(s, d >= 0)
        o_ref[0, 0, rows, :] = _bdot(e.astype(BF16), jnp.concatenate(vs, axis=0)) / jnp.maximum(z, F32_TINY)


def _sel_sample(q_h, cache_slc, layer, page_table, idx, kv_new_rows):
    bsz, t_len = q_h.shape[:2]
    n_pages = page_table.shape[1]
    per_page = PAGE_SIZE // SEL_LEN
    past_blocks = PAST_LEN // SEL_LEN
    pages = _rowwise_pages(cache_slc)
    blk_rows = SEL_LEN * 2 * N_KV_HEADS

    def past_spec(kvh, r):
        def imap(b, t, idx_ref, pt_ref):
            blk = jnp.minimum(idx_ref[((b * N_KV_HEADS + kvh) * t_len + t) * N_SEL + r], past_blocks - 1)
            return (layer, pt_ref[b * n_pages + blk // per_page], blk % per_page, 0)
        return pl.BlockSpec((None, None, blk_rows, HEAD_DIM), imap)

    qspec = pl.BlockSpec((1, 1, N_HEADS, HEAD_DIM), lambda b, t, i_, p_: (b, t, 0, 0))
    return pl.pallas_call(
        functools.partial(_sel_sample_kernel, t_len=t_len),
        grid_spec=pltpu.PrefetchScalarGridSpec(
            num_scalar_prefetch=2,
            grid=(bsz, t_len),
            in_specs=[qspec] + [past_spec(kvh, r) for kvh in range(N_KV_HEADS) for r in range(N_SEL)]
            + [pl.BlockSpec((1, blk_rows, HEAD_DIM), lambda b, t, i_, p_: (b, 0, 0))],
            out_specs=qspec,
        ),
        out_shape=jax.ShapeDtypeStruct(q_h.shape, F32),
        compiler_params=_cparams(("parallel", "parallel"), 32),
        name="sel_sample",
    )(idx.reshape(-1), page_table.reshape(-1), q_h, *([pages] * (N_KV_HEADS * N_SEL)), kv_new_rows)


def _win_sample_kernel(q_ref, past_ref, new_ref, o_ref, *, t_len, t_pad, wbuf):
    t = lax.broadcasted_iota(jnp.int32, (t_len, 1), 0)
    d_past = wbuf + t - lax.broadcasted_iota(jnp.int32, (1, wbuf), 1)
    row_new = lax.broadcasted_iota(jnp.int32, (1, t_pad), 1)
    d_new = t - row_new
    valid_past = (d_past >= 0) & (d_past < WINDOW)
    valid_new = (d_new >= 0) & (d_new < WINDOW) & (row_new < t_len)
    for kvh in range(N_KV_HEADS):
        kp = _stream_rows(past_ref, (0,), 2 * kvh, wbuf).astype(BF16)
        vp = _stream_rows(past_ref, (0,), 2 * kvh + 1, wbuf).astype(BF16)
        kn = _stream_rows(new_ref, (0,), 2 * kvh, t_pad).astype(BF16)
        vn = _stream_rows(new_ref, (0,), 2 * kvh + 1, t_pad).astype(BF16)
        for g in range(GQA):
            q = q_ref[0, kvh, g]
            slope = _alibi_slope(kvh, g)
            m1, e1, z1 = _softmax_parts(_bdot_nt(q, kp) - slope * d_past.astype(F32), valid_past)
            m2, e2, z2 = _softmax_parts(_bdot_nt(q, kn) - slope * d_new.astype(F32), valid_new)
            m = jnp.maximum(m1, m2)
            a1 = jnp.where(z1 > 0.0, jnp.exp2(m1 - m), 0.0)
            a2 = jnp.where(z2 > 0.0, jnp.exp2(m2 - m), 0.0)
            num = a1 * _bdot(e1.astype(BF16), vp) + a2 * _bdot(e2.astype(BF16), vn)
            o_ref[0, kvh, g] = num / jnp.maximum(a1 * z1 + a2 * z2, F32_TINY)


def _win_sample(q_g, win_rows, new_rows):
    bsz, _, _, t_len, _ = q_g.shape
    streams = 2 * N_KV_HEADS
    wbuf, t_pad = win_rows.shape[1] // streams, new_rows.shape[1] // streams
    qspec = pl.BlockSpec((1, N_KV_HEADS, GQA, t_len, HEAD_DIM), lambda b: (b, 0, 0, 0, 0))
    return pl.pallas_call(
        functools.partial(_win_sample_kernel, t_len=t_len, t_pad=t_pad, wbuf=wbuf),
        grid=(bsz,),
        in_specs=[qspec, pl.BlockSpec((1, wbuf * streams, HEAD_DIM), lambda b: (b, 0, 0)),
                  pl.BlockSpec((1, t_pad * streams, HEAD_DIM), lambda b: (b, 0, 0))],
        out_specs=qspec,
        out_shape=jax.ShapeDtypeStruct(q_g.shape, F32),
        compiler_params=_cparams(("parallel",), 16),
        name="win_sample",
    )(q_g, win_rows, new_rows)


def _mix_kernel(x_ref, y_ref, oc_ref, os_ref, ow_ref, gate_ref, wglu_ref, gs_ref, ga_ref, wout_ref, o_ref):
    z = _gelu(y_ref[...])
    z = z * jax.nn.sigmoid(_bdot(z.astype(BF16), wglu_ref[...]))
    gate = gate_ref[...]
    cols = []
    for h in range(N_HEADS):
        sl = slice(h * HEAD_DIM, (h + 1) * HEAD_DIM)
        cols.append(gate[:, 3 * h:3 * h + 1] * oc_ref[:, sl] + gate[:, 3 * h + 1:3 * h + 2] * os_ref[:, sl]
                    + gate[:, 3 * h + 2:3 * h + 3] * ow_ref[:, sl])
    attn = jnp.concatenate(cols, axis=1)
    hs = _rms(z, gs_ref[...]).astype(BF16)
    ha = _rms(attn, ga_ref[...]).astype(BF16)
    o_ref[...] = x_ref[...] + (_bdot(hs, wout_ref[:SSM_WIDTH, :]) + _bdot(ha, wout_ref[SSM_WIDTH:, :]))


def _mix(x, y, oc, osl, ow, gates, prm, tm):
    m = x.shape[0]
    row = lambda n: pl.BlockSpec((tm, n), lambda i: (i, 0))
    return pl.pallas_call(
        _mix_kernel,
        grid=(m // tm,),
        in_specs=[row(D_MODEL), row(SSM_WIDTH), row(ATTN_WIDTH), row(ATTN_WIDTH), row(ATTN_WIDTH), row(N_GATES),
                  _resident_layer((SSM_WIDTH, SSM_WIDTH), prm["layer"]), _resident((1, SSM_WIDTH)),
                  _resident((1, ATTN_WIDTH)), _resident_layer((D_MODEL, D_MODEL), prm["layer"])],
        out_specs=row(D_MODEL),
        out_shape=jax.ShapeDtypeStruct((m, D_MODEL), F32),
        compiler_params=_cparams(("parallel",), 48),
        name="mix_outproj",
    )(x, y, oc, osl, ow, gates, prm["ssm_w_glu"], prm["norm_ssm_out"], prm["norm_attn_out"], prm["w_out"])


FFN_TF = 512


def _ffn_kernel(x_ref, g_ref, wg_ref, wu_ref, wd_ref, o_ref, xn_ref, acc_ref):
    j = pl.program_id(1)

    @pl.when(j == 0)
    def _():
        xn_ref[...] = _rms(x_ref[...], g_ref[...]).astype(BF16)
        acc_ref[...] = jnp.zeros(acc_ref.shape, F32)

    xn = xn_ref[...]
    a = _bdot(xn, wg_ref[...])
    h = (a * jax.nn.sigmoid(a)) * _bdot(xn, wu_ref[...])
    acc_ref[...] += _bdot(h.astype(BF16), wd_ref[...])

    @pl.when(j == pl.num_programs(1) - 1)
    def _():
        o_ref[...] = x_ref[...] + acc_ref[...]


def _ffn(x, prm, tm):
    m = x.shape[0]
    layer = prm["layer"]
    return pl.pallas_call(
        _ffn_kernel,
        grid=(m // tm, D_FF // FFN_TF),
        in_specs=[pl.BlockSpec((tm, D_MODEL), lambda i, j: (i, 0)),
                  pl.BlockSpec((1, D_MODEL), lambda i, j: (0, 0)),
                  pl.BlockSpec((None, D_MODEL, FFN_TF), lambda i, j: (layer, 0, j)),
                  pl.BlockSpec((None, D_MODEL, FFN_TF), lambda i, j: (layer, 0, j)),
                  pl.BlockSpec((None, FFN_TF, D_MODEL), lambda i, j: (layer, j, 0))],
        out_specs=pl.BlockSpec((tm, D_MODEL), lambda i, j: (i, 0)),
        out_shape=jax.ShapeDtypeStruct((m, D_MODEL), F32),
        scratch_shapes=[pltpu.VMEM((tm, D_MODEL), BF16), pltpu.VMEM((tm, D_MODEL), F32)],
        compiler_params=_cparams(("parallel", "arbitrary"), 48),
        name="ffn",
    )(x, prm["norm_ffn"], prm["w_ffn_gate"], prm["w_ffn_up"], prm["w_ffn_down"])


S5_CHUNK = 64
S5_GROUPS_PER_STEP = 2
SAMPLE_S5_PAD = 8


BIG_WEIGHTS = ("w_in", "ssm_w_glu", "w_out", "w_ffn_gate", "w_ffn_up", "w_ffn_down")


def _layer_params(l, w, big):
    def w1cat(w1):
        half = CMP_STRIDE * HEAD_DIM
        return jnp.concatenate([w1[:half], w1[half:]], axis=1).astype(BF16)

    row = lambda v: v.reshape(1, -1)
    a_re, a_im = w["ssm_a_re"][l], w["ssm_a_im"][l]
    log_dt = jnp.broadcast_to(w["ssm_log_dt"][l][:, None], a_re.shape)
    lrow = jnp.stack([a_re, a_im, log_dt], axis=1)
    d = w["ssm_d"][l]
    return {
        "layer": l, **big, "norm_mix": row(w["norm_mix"][l]),
        "q_norm": row(w["q_norm"][l]), "k_norm_slc": row(w["k_norm_slc"][l]),
        "k_norm_win": row(w["k_norm_win"][l]), "k_norm_cmp": row(w["k_norm_cmp"][l]),
        "lrow": lrow, "lcol": lrow.transpose(0, 2, 1),
        "bt": jnp.stack([w["ssm_b_re"][l], w["ssm_b_im"][l]], axis=1).transpose(0, 1, 3, 2),
        "bn": jnp.stack([w["ssm_b_re"][l], w["ssm_b_im"][l]], axis=1),
        "ct": jnp.stack([w["ssm_c_re"][l], w["ssm_c_im"][l]], axis=1).transpose(0, 1, 3, 2),
        "dtile": {lp: jnp.tile(d, (1, lp))[:, None, :] for lp in (S5_CHUNK, SAMPLE_S5_PAD)},
        "cmp_w1_k": w1cat(w["cmp_w1_k"][l]), "cmp_w1_v": w1cat(w["cmp_w1_v"][l]),
        "cmp_b1": jnp.stack([w["cmp_b1_k"][l], w["cmp_b1_v"][l]]),
        "cmp_w2_k": w["cmp_w2_k"][l].astype(BF16), "cmp_w2_v": w["cmp_w2_v"][l].astype(BF16),
        "norm_ssm_out": row(w["norm_ssm_out"][l]), "norm_attn_out": row(w["norm_attn_out"][l]),
        "norm_ffn": row(w["norm_ffn"][l]),
    }


def _kv_out(kv, b, t):
    return kv.reshape(b, t, N_KV_HEADS, 2, HEAD_DIM)


PROMPT_TM = 256
FFN_TM = 512


def _prompt_layer(x, prm):
    b, t, _ = x.shape
    m = b * t
    x2 = x.reshape(m, D_MODEL)
    u, q, kv_cmp, kv_slc, kv_win, gates = _inproj(x2, prm, PROMPT_TM)
    h0 = jnp.zeros((b, SSM_GROUPS, SSM_STATE, 2), F32)
    y, h_last = _s5_mixer(u.reshape(b, t, SSM_WIDTH), h0, prm, S5_CHUNK, S5_CHUNK)
    q3 = q.reshape(b, t, ATTN_WIDTH)
    kv_cmp3, kv_slc3, kv_win3 = (a.reshape(b, t, 2 * KV_WIDTH) for a in (kv_cmp, kv_slc, kv_win))
    ckv = _compress_prompt(kv_cmp3, prm)
    o_cmp, rank = _cmp_select(q3, ckv, 256, t // SEL_LEN, 0, False)
    o_slc = _sel_prompt(q3, kv_slc3, rank, t // SEL_LEN)
    o_win = _win_prompt(q3, kv_win3)
    x2 = _mix(x2, y.reshape(m, SSM_WIDTH), o_cmp.reshape(m, -1), o_slc.reshape(m, -1), o_win.reshape(m, -1),
              gates, prm, PROMPT_TM)
    x2 = _ffn(x2, prm, FFN_TM)
    wbuf = min(WINDOW, PAST_LEN)
    win_state = kv_win3[:, t - wbuf:] if t >= wbuf else jnp.pad(kv_win3, ((0, 0), (wbuf - t, 0), (0, 0)))
    return (x2.reshape(b, t, D_MODEL), _kv_out(kv_cmp3, b, t), _kv_out(kv_slc3, b, t),
            _kv_out(win_state, b, wbuf), h_last)


def _sample_layer(x, prm, cache_cmp, cache_slc, layer, page_table, win_buf, h0):
    b, t, _ = x.shape
    m = b * t
    assert (PAST_LEN + t) // CMP_STRIDE == PAST_LEN // CMP_STRIDE and PAST_LEN % PAGE_SIZE == 0
    x2 = x.reshape(m, D_MODEL)
    u, q, kv_cmp, kv_slc, kv_win, gates = _inproj(x2, prm, m)
    y, h_last = _s5_mixer(u.reshape(b, t, SSM_WIDTH), h0, prm, SAMPLE_S5_PAD, t)
    q3 = q.reshape(b, t, ATTN_WIDTH)
    kv_cmp3, kv_slc3, kv_win3 = (a.reshape(b, t, 2 * KV_WIDTH) for a in (kv_cmp, kv_slc, kv_win))
    ckv = _compress_sample(cache_cmp, layer, page_table, prm)
    n_blocks = -(-(PAST_LEN + t) // SEL_LEN)
    o_cmp, _, idx = _cmp_select(q3, ckv, t, n_blocks, PAST_LEN, True)
    streams = 2 * N_KV_HEADS

    def stream_rows(kv3, t_pad):
        return jnp.pad(kv3.reshape(b, t * streams, HEAD_DIM), ((0, 0), (0, (t_pad - t) * streams), (0, 0)))

    o_slc = _sel_sample(q3.reshape(b, t, N_HEADS, HEAD_DIM), cache_slc, layer, page_table, idx[..., :N_SEL],
                        stream_rows(kv_slc3, SEL_LEN)).reshape(m, ATTN_WIDTH)
    wbuf = win_buf.shape[1]
    q5 = q3.reshape(b, t, N_KV_HEADS, GQA, HEAD_DIM)
    o_win = _win_sample(q5.transpose(0, 2, 3, 1, 4), win_buf.reshape(b, wbuf * streams, HEAD_DIM),
                        stream_rows(kv_win3, 8))
    o_win = o_win.transpose(0, 3, 1, 2, 4).reshape(m, ATTN_WIDTH)
    x2 = _mix(x2, y.reshape(m, SSM_WIDTH), o_cmp.reshape(m, -1), o_slc, o_win, gates, prm, m)
    x2 = _ffn(x2, prm, m)
    win_state = jnp.concatenate([win_buf, _kv_out(kv_win3, b, t)], axis=1)[:, -wbuf:]
    return (x2.reshape(b, t, D_MODEL), _kv_out(kv_cmp3, b, t), _kv_out(kv_slc3, b, t), win_state, h_last)


def kernel(x_prompt, x_sample, cache_cmp_kv, cache_slc_kv, state_win_kv, state_ssm, page_table, norm_mix, w_in, ssm_a_re, ssm_a_im, ssm_log_dt, ssm_b_re, ssm_b_im, ssm_c_re, ssm_c_im, ssm_d, ssm_w_glu, q_norm, k_norm_cmp, k_norm_slc, k_norm_win, cmp_w1_k, cmp_b1_k, cmp_w2_k, cmp_w1_v, cmp_b1_v, cmp_w2_v, norm_ssm_out, norm_attn_out, w_out, norm_ffn, w_ffn_gate, w_ffn_up, w_ffn_down):
    w = dict(norm_mix=norm_mix, w_in=w_in, ssm_a_re=ssm_a_re, ssm_a_im=ssm_a_im, ssm_log_dt=ssm_log_dt,
             ssm_b_re=ssm_b_re, ssm_b_im=ssm_b_im, ssm_c_re=ssm_c_re, ssm_c_im=ssm_c_im, ssm_d=ssm_d,
             ssm_w_glu=ssm_w_glu, q_norm=q_norm, k_norm_cmp=k_norm_cmp, k_norm_slc=k_norm_slc,
             k_norm_win=k_norm_win, cmp_w1_k=cmp_w1_k, cmp_b1_k=cmp_b1_k, cmp_w2_k=cmp_w2_k,
             cmp_w1_v=cmp_w1_v, cmp_b1_v=cmp_b1_v, cmp_w2_v=cmp_w2_v, norm_ssm_out=norm_ssm_out,
             norm_attn_out=norm_attn_out, w_out=w_out, norm_ffn=norm_ffn, w_ffn_gate=w_ffn_gate,
             w_ffn_up=w_ffn_up, w_ffn_down=w_ffn_down)
    big = {name: w[name].astype(BF16) for name in BIG_WEIGHTS}
    y_p, y_s = x_prompt, x_sample
    outs_p, outs_s = [], []
    for l in range(DEPTH):
        prm = _layer_params(l, w, big)
        y_p, *rest = _prompt_layer(y_p, prm)
        outs_p.append(rest)
        y_s, *rest = _sample_layer(y_s, prm, cache_cmp_kv, cache_slc_kv, l, page_table,
                                   state_win_kv[l], state_ssm[l])
        outs_s.append(rest)
    stack = lambda outs, i: jnp.stack([o[i] for o in outs])
    return (y_p, y_s, stack(outs_p, 0), stack(outs_p, 1), stack(outs_p, 2), stack(outs_p, 3),
            stack(outs_s, 0), stack(outs_s, 1), stack(outs_s, 2), stack(outs_s, 3))
```

```python
import functools
import math

import jax
import jax.numpy as jnp
import numpy as np
from jax import lax
from jax.experimental import pallas as pl
from jax.experimental.pallas import tpu as pltpu

F32 = jnp.float32
BF16 = jnp.bfloat16

D_MODEL = 2048
DEPTH = 2
PAST_LEN = 16384
PAGE_SIZE = 128
SSM_WIDTH = 1024
ATTN_WIDTH = 1024
SSM_CH = 16
SSM_GROUPS = 64
SSM_STATE = 64
HEAD_DIM = 128
N_HEADS = 8
N_KV_HEADS = 2
GQA = 4
KV_WIDTH = 256
CMP_LEN = 32
CMP_STRIDE = 16
SEL_LEN = 64
SEL_SHIFT = SEL_LEN.bit_length() - 1
N_SEL = 16
WINDOW = 512
WIN_TQ = 256
FORCED_SCORE = 1e4
D_FF = 5632
NORM_EPS = 1e-6
LOG2E = math.log2(math.e)
QSCALE = HEAD_DIM ** -0.5 * LOG2E
N_GATES = N_HEADS * 3
IN_WIDTH = SSM_WIDTH + ATTN_WIDTH + 6 * KV_WIDTH + N_GATES
F32_TINY = float(jnp.finfo(jnp.float32).tiny)
NEG_INF = float("-inf")

V7X_VMEM_BYTES = 64 * 1024 * 1024
V7X_LANES = 128


def _cparams(semantics, vmem_mib):
    assert vmem_mib * 1024 * 1024 < V7X_VMEM_BYTES
    return pltpu.CompilerParams(dimension_semantics=semantics, vmem_limit_bytes=vmem_mib * 1024 * 1024)


def _resident(shape):
    nd = len(shape)
    return pl.BlockSpec(shape, lambda *_: (0,) * nd, pipeline_mode=pl.Buffered(1))


def _resident_layer(shape, layer):
    nd = len(shape)
    return pl.BlockSpec((None,) + tuple(shape), lambda *_: (layer,) + (0,) * nd, pipeline_mode=pl.Buffered(1))


def _rms(x, g):
    return x * lax.rsqrt(jnp.mean(x * x, axis=-1, keepdims=True) + NORM_EPS) * g


def _gelu(x):
    c = math.sqrt(2.0 / math.pi)
    return x * (0.5 * (1.0 + jnp.tanh(c * (x + 0.044715 * (x * x * x)))))


def _bdot(a, b):
    return jnp.dot(a, b, preferred_element_type=F32)


def _bdot_nt(a, b):
    return lax.dot_general(a, b, (((1,), (1,)), ((), ())), preferred_element_type=F32)


def _split3(x):
    hi = x.astype(BF16)
    r1 = x - hi.astype(F32)
    mid = r1.astype(BF16)
    lo = (r1 - mid.astype(F32)).astype(BF16)
    return hi, mid, lo


def _dot_small_int(x, e):
    hi, mid, lo = _split3(x)
    return _bdot(hi, e) + _bdot(mid, e) + _bdot(lo, e)


def _dot_f32(a, b):
    ah, am, al = _split3(a)
    bh, bm, bl = _split3(b)
    return (_bdot(ah, bh) + (_bdot(ah, bm) + _bdot(am, bh))
            + (_bdot(ah, bl) + _bdot(al, bh) + _bdot(am, bm)))


def _softmax_parts(s, valid):
    s = jnp.where(valid, s, NEG_INF)
    m = jnp.max(s, axis=-1, keepdims=True)
    m = jnp.where(m == NEG_INF, 0.0, m)
    e = jnp.exp2(s - m)
    return m, e, jnp.sum(e, axis=-1, keepdims=True)


def _flash_step(m_ref, acc_ref, g, s, v_ones):
    m_old = m_ref[g]
    m_new = jnp.maximum(m_old, jnp.max(s, axis=-1, keepdims=True))
    m_safe = jnp.where(m_new == NEG_INF, 0.0, m_new)
    acc_ref[g] = jnp.exp2(m_old - m_safe) * acc_ref[g] + _bdot(jnp.exp2(s - m_safe).astype(BF16), v_ones)
    m_ref[g] = m_new


def _alibi_slope(kvh, g):
    return LOG2E * 2.0 ** (-8.0 * (kvh * GQA + g + 1) / N_HEADS)


def _bf16_terms(x, n=3):
    out = []
    for _ in range(n):
        hi = float(np.asarray(x, dtype=np.float32).astype(BF16))
        out.append(hi)
        x -= hi
    return out


def _head_slope(kvh, g):
    assert N_KV_HEADS == 2
    return jnp.where(kvh == 0, _alibi_slope(0, g), _alibi_slope(1, g))


def _inproj_kernel(x_ref, gmix_ref, w_ref, qn_ref, ksn_ref, kwn_ref,
                   u_ref, q_ref, cmp_ref, slc_ref, win_ref, gate_ref):
    xn = _rms(x_ref[...], gmix_ref[...]).astype(BF16)

    def proj(c0, c1):
        return _bdot(xn, w_ref[:, c0:c1])

    u_ref[...] = proj(0, SSM_WIDTH)
    q = proj(SSM_WIDTH, D_MODEL)
    for h in range(N_HEADS):
        sl = slice(h * HEAD_DIM, (h + 1) * HEAD_DIM)
        q_ref[:, sl] = (_rms(q[:, sl], qn_ref[...]) * QSCALE).astype(BF16)
    c0 = D_MODEL
    for ref, nref in ((cmp_ref, None), (slc_ref, ksn_ref), (win_ref, kwn_ref)):
        k = proj(c0, c0 + KV_WIDTH)
        v = proj(c0 + KV_WIDTH, c0 + 2 * KV_WIDTH)
        c0 += 2 * KV_WIDTH
        for kvh in range(N_KV_HEADS):
            src = slice(kvh * HEAD_DIM, (kvh + 1) * HEAD_DIM)
            dst = kvh * 2 * HEAD_DIM
            ref[:, dst:dst + HEAD_DIM] = k[:, src] if nref is None else _rms(k[:, src], nref[...])
            ref[:, dst + HEAD_DIM:dst + 2 * HEAD_DIM] = v[:, src]
    gate_ref[...] = jax.nn.sigmoid(proj(c0, c0 + N_GATES))


def _inproj(x, prm, tm):
    m = x.shape[0]
    row = lambda n: pl.BlockSpec((tm, n), lambda i: (i, 0))
    kvw = 2 * KV_WIDTH
    return pl.pallas_call(
        _inproj_kernel,
        grid=(m // tm,),
        in_specs=[row(D_MODEL), _resident((1, D_MODEL)), _resident_layer((D_MODEL, IN_WIDTH), prm["layer"]),
                  _resident((1, HEAD_DIM)), _resident((1, HEAD_DIM)), _resident((1, HEAD_DIM))],
        out_specs=[row(SSM_WIDTH), row(ATTN_WIDTH), row(kvw), row(kvw), row(kvw), row(N_GATES)],
        out_shape=[jax.ShapeDtypeStruct((m, SSM_WIDTH), F32), jax.ShapeDtypeStruct((m, ATTN_WIDTH), BF16),
                   jax.ShapeDtypeStruct((m, kvw), F32), jax.ShapeDtypeStruct((m, kvw), F32),
                   jax.ShapeDtypeStruct((m, kvw), F32), jax.ShapeDtypeStruct((m, N_GATES), F32)],
        compiler_params=_cparams(("parallel",), 48),
        name="inproj",
    )(x, prm["norm_mix"], prm["w_in"], prm["q_norm"], prm["k_norm_slc"], prm["k_norm_win"])


def _s5_group_body(gi, u_ref, h0_ref, lrow_ref, lcol_ref, bt_ref, bn_ref, ct_ref, d_ref,
                   y_ref, hl_ref, tm_ref, x_ref, hin_ref, *, lp, lreal, nb, nc):
    k = lp * SSM_CH
    n = SSM_STATE

    def lam_bar(a_re, a_im, log_dt):
        dt = jnp.exp(log_dt)
        e = jnp.exp(a_re * dt)
        return e * jnp.cos(a_im * dt), e * jnp.sin(a_im * dt)

    def zoh_coef(l_re, l_im, a_re, a_im):
        den = a_re * a_re + a_im * a_im
        x_re = l_re - 1.0
        return (x_re * a_re + l_im * a_im) / den, (l_im * a_re - x_re * a_im) / den

    lc = lcol_ref[gi]
    a_re_c, a_im_c = lc[:, 0:1], lc[:, 1:2]
    l_re_c, l_im_c = lam_bar(a_re_c, a_im_c, lc[:, 2:3])
    lr = lrow_ref[gi]
    a_re_r, a_im_r = lr[0:1], lr[1:2]
    l_re_r, l_im_r = lam_bar(a_re_r, a_im_r, lr[2:3])

    def pow_table(e):
        p_re = jnp.ones((n, k), F32)
        p_im = jnp.zeros((n, k), F32)
        s_re, s_im = l_re_c, l_im_c
        for j in range(max(lp.bit_length() - 1, 1)):
            bit = ((e >> j) & 1) == 1
            m_re = jnp.where(bit, s_re, 1.0)
            m_im = jnp.where(bit, s_im, 0.0)
            p_re, p_im = p_re * m_re - p_im * m_im, p_re * m_im + p_im * m_re
            s_re, s_im = s_re * s_re - s_im * s_im, 2.0 * s_re * s_im
        return p_re, p_im

    lane = lax.broadcasted_iota(jnp.int32, (1, k), 1)
    tau = lane >> (SSM_CH.bit_length() - 1)
    ch_onehot = jnp.where((lax.broadcasted_iota(jnp.int32, (SSM_CH, k), 1) & (SSM_CH - 1))
                          == lax.broadcasted_iota(jnp.int32, (SSM_CH, k), 0), 1.0, 0.0).astype(BF16)

    p_re, p_im = pow_table(tau)
    c_re = _dot_small_int(ct_ref[gi, 0], ch_onehot)
    c_im = _dot_small_int(ct_ref[gi, 1], ch_onehot)
    g_re = c_re * p_re - c_im * p_im
    g_im = c_re * p_im + c_im * p_re

    cf_re_r, cf_im_r = zoh_coef(l_re_r, l_im_r, a_re_r, a_im_r)
    bbt_re = cf_re_r * bt_ref[gi, 0] - cf_im_r * bt_ref[gi, 1]
    bbt_im = cf_re_r * bt_ref[gi, 1] + cf_im_r * bt_ref[gi, 0]
    a = _dot_f32(bbt_re, g_re) - _dot_f32(bbt_im, g_im)

    lane16 = lax.broadcasted_iota(jnp.int32, (SSM_CH, k), 1)
    for s in range(lp):
        blk = a if s == 0 else jnp.where(lane16 >= SSM_CH * s, pltpu.roll(a, SSM_CH * s, axis=1), 0.0)
        tm_ref[gi, SSM_CH * s:SSM_CH * (s + 1), :] = blk.astype(BF16)

    cf_re_c, cf_im_c = zoh_coef(l_re_c, l_im_c, a_re_c, a_im_c)
    bbn_re = cf_re_c * bn_ref[gi, 0] - cf_im_c * bn_ref[gi, 1]
    bbn_im = cf_re_c * bn_ref[gi, 1] + cf_im_c * bn_ref[gi, 0]
    bb_re = _dot_small_int(bbn_re, ch_onehot)
    bb_im = _dot_small_int(bbn_im, ch_onehot)
    rev = (lreal - 1) - tau
    q_re, q_im = pow_table(jnp.maximum(rev, 0))
    live = rev >= 0
    zeros_nk = jnp.zeros((n, k), BF16)
    wx_re = jnp.concatenate([jnp.where(live, q_re * bb_re - q_im * bb_im, 0.0).astype(BF16), zeros_nk], axis=0)
    wx_im = jnp.concatenate([jnp.where(live, q_re * bb_im + q_im * bb_re, 0.0).astype(BF16), zeros_nk], axis=0)

    u = u_ref[gi]
    ub = u.astype(BF16)
    x_ref[gi, 0] = _bdot_nt(ub, wx_re)
    x_ref[gi, 1] = _bdot_nt(ub, wx_im)

    ll_re, ll_im = l_re_r, l_im_r
    for _ in range(lreal.bit_length() - 1):
        ll_re, ll_im = ll_re * ll_re - ll_im * ll_im, 2.0 * ll_re * ll_im
    ll_re = jnp.concatenate([ll_re, jnp.zeros_like(ll_re)], axis=1)
    ll_im = jnp.concatenate([ll_im, jnp.zeros_like(ll_im)], axis=1)
    lower = lax.broadcasted_iota(jnp.int32, (1, 2 * n), 1) < n
    h0 = h0_ref[gi]
    h_re = jnp.where(lower, h0, 0.0)
    h_im = jnp.where(lower, pltpu.roll(h0, n, axis=1), 0.0)
    for c in range(nc):
        rows = pl.ds(c, nb, stride=nc) if nc > 1 else slice(0, nb)
        hin_ref[gi, 0, rows, :] = h_re
        hin_ref[gi, 1, rows, :] = h_im
        h_re, h_im = (ll_re * h_re - ll_im * h_im + x_ref[gi, 0, rows, :],
                      ll_re * h_im + ll_im * h_re + x_ref[gi, 1, rows, :])
    hl_ref[gi] = jnp.where(lower, h_re, pltpu.roll(h_im, n, axis=1))

    g1_re = jnp.concatenate([(g_re * l_re_c - g_im * l_im_c).astype(BF16), zeros_nk], axis=0)
    g1_im = jnp.concatenate([(g_re * l_im_c + g_im * l_re_c).astype(BF16), zeros_nk], axis=0)
    y_state = _bdot(hin_ref[gi, 0].astype(BF16), g1_re) - _bdot(hin_ref[gi, 1].astype(BF16), g1_im)

    y_ref[gi] = _bdot(ub, tm_ref[gi]) + y_state + u * d_ref[gi]


def _s5_kernel(*refs, gb, **kw):
    for gi in range(gb):
        _s5_group_body(gi, *refs, **kw)


def _s5(u_g, h0_g, prm, lp, lreal, nb, nc):
    g, m, k = u_g.shape
    gb = S5_GROUPS_PER_STEP
    blk = lambda *s: pl.BlockSpec((gb,) + s, lambda i: (i,) + (0,) * len(s))
    return pl.pallas_call(
        functools.partial(_s5_kernel, gb=gb, lp=lp, lreal=lreal, nb=nb, nc=nc),
        grid=(g // gb,),
        in_specs=[blk(m, k), blk(nb, 2 * SSM_STATE), blk(3, SSM_STATE), blk(SSM_STATE, 3),
                  blk(2, SSM_CH, SSM_STATE), blk(2, SSM_STATE, SSM_CH), blk(2, SSM_STATE, SSM_CH), blk(1, k)],
        out_specs=[blk(m, k), blk(nb, 2 * SSM_STATE)],
        out_shape=[jax.ShapeDtypeStruct((g, m, k), F32), jax.ShapeDtypeStruct((g, nb, 2 * SSM_STATE), F32)],
        scratch_shapes=[pltpu.VMEM((gb, k, k), BF16), pltpu.VMEM((gb, 2, m, 2 * SSM_STATE), F32),
                        pltpu.VMEM((gb, 2, m, 2 * SSM_STATE), F32)],
        compiler_params=_cparams(("parallel",), 32),
        name="s5_group",
    )(u_g, h0_g, prm["lrow"], prm["lcol"], prm["bt"], prm["bn"], prm["ct"], prm["dtile"][lp])


def _s5_mixer(u, h0, prm, lp, lreal):
    b, t, _ = u.shape
    nc = t // lreal
    ug = u.reshape(b * nc, lreal, SSM_GROUPS, SSM_CH)
    if lp != lreal:
        ug = jnp.pad(ug, ((0, 0), (0, lp - lreal), (0, 0), (0, 0)))
    ug = ug.transpose(2, 0, 1, 3).reshape(SSM_GROUPS, b * nc, lp * SSM_CH)
    h0g = h0.transpose(1, 0, 3, 2).reshape(SSM_GROUPS, b, 2 * SSM_STATE)
    yg, hl = _s5(ug, h0g, prm, lp, lreal, b, nc)
    y = yg.reshape(SSM_GROUPS, b * nc, lp, SSM_CH)[:, :, :lreal].transpose(1, 2, 0, 3)
    h_last = hl.reshape(SSM_GROUPS, b, 2, SSM_STATE).transpose(1, 0, 3, 2)
    return y.reshape(b, t, SSM_WIDTH), h_last


def _cmp_proj_kernel(*refs, n_in, rowwise):
    refs = refs[len(refs) - n_in - 3:]
    x_refs, (w1k_ref, w1v_ref, o_ref) = refs[:n_in], refs[n_in:]

    def rows_of(j, r):
        if rowwise:
            n = x_refs[0].shape[0] // (4 * CMP_STRIDE)
            parts = [x[pl.ds(4 * r + j, n, stride=4 * CMP_STRIDE), :] for x in x_refs]
            return parts[0] if n_in == 1 else jnp.concatenate(parts, axis=0)
        x = x_refs[j]
        return x[0, pl.ds(r, x.shape[1] // CMP_STRIDE, stride=CMP_STRIDE), :]

    for j in range(4):
        c = jnp.concatenate([rows_of(j, r) for r in range(CMP_STRIDE)], axis=1).astype(BF16)
        w = w1k_ref if j % 2 == 0 else w1v_ref
        o_ref[0, :, j * 2 * HEAD_DIM:(j + 1) * 2 * HEAD_DIM] = _bdot(c, w[...])


def _cmp_combine_kernel(h_ref, b1_ref, w2k_ref, w2v_ref, kn_ref, o_ref, *, nchunk):
    h = h_ref[0]
    keep = lax.broadcasted_iota(jnp.int32, (nchunk, 1), 0) < nchunk - 1
    for j in range(4):
        kvh, is_v = j // 2, j % 2
        lo = h[:, j * 2 * HEAD_DIM: j * 2 * HEAD_DIM + HEAD_DIM]
        hi = h[:, j * 2 * HEAD_DIM + HEAD_DIM:(j + 1) * 2 * HEAD_DIM]
        pre = lo + pltpu.roll(hi, nchunk - 1, axis=0) + b1_ref[is_v:is_v + 1, :]
        out = _bdot(_gelu(pre).astype(BF16), (w2v_ref if is_v else w2k_ref)[...])
        if not is_v:
            out = _rms(out, kn_ref[...])
        c0 = is_v * N_KV_HEADS * HEAD_DIM + kvh * HEAD_DIM
        o_ref[0, :, c0:c0 + HEAD_DIM] = jnp.where(keep, out, 0.0)


def _cmp_combine(h, prm):
    b, nchunk, w = h.shape
    return pl.pallas_call(
        functools.partial(_cmp_combine_kernel, nchunk=nchunk),
        grid=(b,),
        in_specs=[pl.BlockSpec((1, nchunk, w), lambda i: (i, 0, 0)), _resident((2, HEAD_DIM)),
                  _resident((HEAD_DIM, HEAD_DIM)), _resident((HEAD_DIM, HEAD_DIM)), _resident((1, HEAD_DIM))],
        out_specs=pl.BlockSpec((1, nchunk, 4 * HEAD_DIM), lambda i: (i, 0, 0)),
        out_shape=jax.ShapeDtypeStruct((b, nchunk, 4 * HEAD_DIM), F32),
        compiler_params=_cparams(("parallel",), 40),
        name="cmp_combine",
    )(h, prm["cmp_b1"], prm["cmp_w2_k"], prm["cmp_w2_v"], prm["k_norm_cmp"])


def _compress_prompt(kv_cmp, prm):
    b, t, w = kv_cmp.shape
    nchunk = t // CMP_STRIDE
    streams = w // HEAD_DIM
    h = pl.pallas_call(
        functools.partial(_cmp_proj_kernel, n_in=streams, rowwise=False),
        grid=(b,),
        in_specs=[pl.BlockSpec((1, t, HEAD_DIM), functools.partial(lambda i, j: (i, 0, j), j=j))
                  for j in range(streams)] + [
                  _resident((CMP_STRIDE * HEAD_DIM, 2 * HEAD_DIM)), _resident((CMP_STRIDE * HEAD_DIM, 2 * HEAD_DIM))],
        out_specs=pl.BlockSpec((1, nchunk, 8 * HEAD_DIM), lambda i: (i, 0, 0)),
        out_shape=jax.ShapeDtypeStruct((b, nchunk, 8 * HEAD_DIM), F32),
        compiler_params=_cparams(("parallel",), 40),
        name="cmp_proj_prompt",
    )(*([kv_cmp] * streams), prm["cmp_w1_k"], prm["cmp_w1_v"])
    return _cmp_combine(h, prm)


PAGES_PER_STEP = 16


def _rowwise_pages(cache):
    return cache.reshape(cache.shape[0], cache.shape[1], PAGE_SIZE * 2 * N_KV_HEADS, HEAD_DIM)


def _compress_sample(cache_cmp, layer, page_table, prm):
    bsz, n_pages = page_table.shape
    chunks_per_page = PAGE_SIZE // CMP_STRIDE
    pages = _rowwise_pages(cache_cmp)
    steps = n_pages // PAGES_PER_STEP

    def page_spec(p):
        return pl.BlockSpec((None, None, 4 * PAGE_SIZE, HEAD_DIM),
                            lambda b, s, pt: (layer, pt[b * n_pages + s * PAGES_PER_STEP + p], 0, 0))

    rows = PAGES_PER_STEP * chunks_per_page
    h = pl.pallas_call(
        functools.partial(_cmp_proj_kernel, n_in=PAGES_PER_STEP, rowwise=True),
        grid_spec=pltpu.PrefetchScalarGridSpec(
            num_scalar_prefetch=1,
            grid=(bsz, steps),
            in_specs=[page_spec(p) for p in range(PAGES_PER_STEP)] + [
                pl.BlockSpec((CMP_STRIDE * HEAD_DIM, 2 * HEAD_DIM), lambda b, s, pt: (0, 0)),
                pl.BlockSpec((CMP_STRIDE * HEAD_DIM, 2 * HEAD_DIM), lambda b, s, pt: (0, 0))],
            out_specs=pl.BlockSpec((1, rows, 8 * HEAD_DIM), lambda b, s, pt: (b, s, 0)),
        ),
        out_shape=jax.ShapeDtypeStruct((bsz, n_pages * chunks_per_page, 8 * HEAD_DIM), F32),
        compiler_params=_cparams(("parallel", "parallel"), 40),
        name="cmp_proj_sample",
    )(page_table.reshape(-1), *([pages] * PAGES_PER_STEP), prm["cmp_w1_k"], prm["cmp_w1_v"])
    return _cmp_combine(h, prm)


def _cmp_select_kernel(q_ref, ck_ref, cv_ref, o_ref, rank_ref, *idx_ref, tq, nc, nbp, nblk, qpos0):
    kvh = pl.program_id(1)
    qpos = qpos0 + pl.program_id(2) * tq + lax.broadcasted_iota(jnp.int32, (tq, 1), 0)
    start = lax.broadcasted_iota(jnp.int32, (1, nc), 1) * CMP_STRIDE
    valid = (start + (CMP_LEN - 1)) <= qpos
    dist = (qpos - start).astype(F32) - (CMP_LEN - 1) / 2
    ck = ck_ref[0].astype(BF16)
    cv = cv_ref[0].astype(BF16)
    psum = jnp.zeros((tq, nc), F32)
    for g in range(GQA):
        sl = slice(g * HEAD_DIM, (g + 1) * HEAD_DIM)
        s = _bdot_nt(q_ref[0, :, sl], ck) - _head_slope(kvh, g) * dist
        _, e, den = _softmax_parts(s, valid)
        p = e / jnp.maximum(den, F32_TINY)
        o_ref[0, :, sl] = _bdot(p.astype(BF16), cv)
        psum = psum + p

    per_shift = (SEL_LEN // CMP_STRIDE).bit_length() - 1
    ci = lax.broadcasted_iota(jnp.int32, (nc, nbp), 0)
    bj = lax.broadcasted_iota(jnp.int32, (nc, nbp), 1)
    overlap = (jnp.where(ci >> per_shift == bj, 1.0, 0.0)
               + jnp.where((ci + 1) >> per_shift == bj, 1.0, 0.0)).astype(BF16)
    score = _dot_small_int(psum, overlap)

    j = lax.broadcasted_iota(jnp.int32, (1, nbp), 1)
    qb = qpos >> SEL_SHIFT
    forced = (j == 0) | (j == qb) | (j == qb - 1)
    causal = (j * SEL_LEN) <= qpos
    score = jnp.where(forced, FORCED_SCORE, jnp.where(causal, score, -1.0))
    score = jnp.where(j < nblk, score, -2.0)

    def ranks():
        rank = jnp.zeros((tq, nbp), F32)
        for i in range(nblk):
            si = score[:, i:i + 1]
            rank = rank + jnp.where(j > i, jnp.where(si >= score, 1.0, 0.0), jnp.where(si > score, 1.0, 0.0))
        return rank

    if idx_ref or qpos0 >= N_SEL * SEL_LEN:
        rank = ranks()
        rank_ref[0, 0] = rank
    else:
        crowded = qpos0 + (pl.program_id(2) + 1) * tq > N_SEL * SEL_LEN

        @pl.when(crowded)
        def _():
            rank_ref[0, 0] = ranks()

        @pl.when(jnp.logical_not(crowded))
        def _():
            rank_ref[0, 0] = jnp.zeros((tq, nbp), F32)

    if idx_ref:
        jf = j.astype(F32)
        lane = lax.broadcasted_iota(jnp.int32, (tq, V7X_LANES), 1)
        idx = jnp.zeros((tq, V7X_LANES), F32)
        for r in range(N_SEL):
            col = jnp.sum(jnp.where(rank == float(r), jf, 0.0), axis=-1, keepdims=True)
            idx = jnp.where(lane == r, col, idx)
        idx_ref[0][0, 0] = idx.astype(jnp.int32)


def _cmp_select(q, ckv, tq, nblk, qpos0, want_idx):
    b, t, _ = q.shape
    nc = ckv.shape[1]
    nbp = -(-nblk // V7X_LANES) * V7X_LANES
    hw = GQA * HEAD_DIM
    out_specs = [pl.BlockSpec((1, tq, hw), lambda bi, k, i: (bi, i, k)),
                 pl.BlockSpec((1, 1, tq, nbp), lambda bi, k, i: (bi, k, i, 0))]
    out_shape = [jax.ShapeDtypeStruct((b, t, ATTN_WIDTH), F32),
                 jax.ShapeDtypeStruct((b, N_KV_HEADS, t, nbp), F32)]
    if want_idx:
        out_specs.append(pl.BlockSpec((1, 1, tq, V7X_LANES), lambda bi, k, i: (bi, k, i, 0)))
        out_shape.append(jax.ShapeDtypeStruct((b, N_KV_HEADS, t, V7X_LANES), jnp.int32))
    return pl.pallas_call(
        functools.partial(_cmp_select_kernel, tq=tq, nc=nc, nbp=nbp, nblk=nblk, qpos0=qpos0),
        grid=(b, N_KV_HEADS, t // tq),
        in_specs=[pl.BlockSpec((1, tq, hw), lambda bi, k, i: (bi, i, k)),
                  pl.BlockSpec((1, nc, HEAD_DIM), lambda bi, k, i: (bi, 0, k)),
                  pl.BlockSpec((1, nc, HEAD_DIM), lambda bi, k, i: (bi, 0, N_KV_HEADS + k))],
        out_specs=out_specs,
        out_shape=out_shape,
        compiler_params=_cparams(("parallel", "parallel", "parallel"), 40),
        name="cmp_select",
    )(q, ckv, ckv)


ATT_TQ = 256
SEL_TK = 512
FILL_ROWS = 256
POS_RADIX = 256
RATE_TERMS = 3
MASK_BIAS = 2.0 ** 100


def _key_features(kidx, nblk):
    lane = lax.broadcasted_iota(jnp.int32, (1, V7X_LANES), 1)
    radix_shift = POS_RADIX.bit_length() - 1
    digits = jnp.where(((lane - nblk) & 1) == 0, (kidx >> radix_shift).astype(F32),
                       (kidx & (POS_RADIX - 1)).astype(F32))
    feats = jnp.where(lane < nblk + 2 * RATE_TERMS, digits, 0.0)
    if nblk:
        feats = jnp.where(lane < nblk, jnp.where((kidx >> SEL_SHIFT) == lane, 1.0, 0.0), feats)
    return feats


def _query_features(kvh, g, rank, nblk, rows):
    lane = lax.broadcasted_iota(jnp.int32, (rows, V7X_LANES), 1)
    feats = jnp.zeros((rows, V7X_LANES), F32)
    if nblk:
        feats = jnp.where((lane < nblk) & (rank >= float(N_SEL)), -MASK_BIAS, 0.0)
    for i, (c0, c1) in enumerate(zip(_bf16_terms(_alibi_slope(0, g), RATE_TERMS),
                                     _bf16_terms(_alibi_slope(1, g), RATE_TERMS))):
        c = jnp.where(kvh == 0, c0, c1)
        feats = jnp.where(lane == nblk + 2 * i, c * POS_RADIX, feats)
        feats = jnp.where(lane == nblk + 2 * i + 1, c, feats)
    return feats


def _fill_key_value_scratch(k_ref, v_ref, kf_ref, vo_ref, nblk):
    def chunk(c, carry):
        r0 = pl.multiple_of(c * FILL_ROWS, FILL_ROWS)
        rows = pl.ds(r0, FILL_ROWS)
        kidx = r0 + lax.broadcasted_iota(jnp.int32, (FILL_ROWS, 1), 0)
        kf_ref[rows, :HEAD_DIM] = k_ref[0, rows, :].astype(BF16)
        kf_ref[rows, HEAD_DIM:] = _key_features(kidx, nblk).astype(BF16)
        vo_ref[rows, :HEAD_DIM] = v_ref[0, rows, :].astype(BF16)
        vo_ref[rows, HEAD_DIM:] = jnp.ones((FILL_ROWS, V7X_LANES), BF16)
        return carry

    lax.fori_loop(0, k_ref.shape[1] // FILL_ROWS, chunk, 0)


def _attn_out(acc):
    return acc[:, :HEAD_DIM] / jnp.maximum(acc[:, HEAD_DIM:HEAD_DIM + 1], F32_TINY)


def _sel_prompt_kernel(q_ref, k_ref, v_ref, rank_ref, o_ref, kf_ref, vo_ref, qf_ref, m_ref, acc_ref, *, nblk):
    kvh, i = pl.program_id(1), pl.program_id(2)

    @pl.when(i == 0)
    def _():
        _fill_key_value_scratch(k_ref, v_ref, kf_ref, vo_ref, nblk)

    q0 = i * ATT_TQ
    qpos = q0 + (lax.broadcasted_iota(jnp.int32, (GQA * ATT_TQ, 1), 0) & (ATT_TQ - 1))
    rank = rank_ref[0, 0]
    for g in range(GQA):
        rows = slice(g * ATT_TQ, (g + 1) * ATT_TQ)
        qf_ref[rows, :HEAD_DIM] = q_ref[0, :, g * HEAD_DIM:(g + 1) * HEAD_DIM]
        qf_ref[rows, HEAD_DIM:] = _query_features(kvh, g, rank, nblk, ATT_TQ).astype(BF16)
    m_ref[...] = jnp.full(m_ref.shape, NEG_INF, F32)
    acc_ref[...] = jnp.zeros(acc_ref.shape, F32)

    def tile(jt, carry):
        k0 = pl.multiple_of(jt * SEL_TK, SEL_TK)
        causal = (k0 + lax.broadcasted_iota(jnp.int32, (1, SEL_TK), 1)) <= qpos
        s = jnp.where(causal, _bdot_nt(qf_ref[...], kf_ref[pl.ds(k0, SEL_TK), :]), NEG_INF)
        _flash_step(m_ref, acc_ref, 0, s, vo_ref[pl.ds(k0, SEL_TK), :])
        return carry

    lax.fori_loop(0, (q0 + ATT_TQ + SEL_TK - 1) // SEL_TK, tile, 0)
    out = _attn_out(acc_ref[0])
    for g in range(GQA):
        o_ref[0, :, g * HEAD_DIM:(g + 1) * HEAD_DIM] = out[g * ATT_TQ:(g + 1) * ATT_TQ]


def _sel_prompt(q, kv_slc, rank, nblk):
    b, t, _ = q.shape
    nbp = rank.shape[-1]
    hw = GQA * HEAD_DIM
    assert nbp == V7X_LANES and nblk + 2 * RATE_TERMS <= V7X_LANES and t <= POS_RADIX * POS_RADIX
    return pl.pallas_call(
        functools.partial(_sel_prompt_kernel, nblk=nblk),
        grid=(b, N_KV_HEADS, t // ATT_TQ),
        in_specs=[pl.BlockSpec((1, ATT_TQ, hw), lambda bi, k, i: (bi, i, k)),
                  pl.BlockSpec((1, t, HEAD_DIM), lambda bi, k, i: (bi, 0, 2 * k)),
                  pl.BlockSpec((1, t, HEAD_DIM), lambda bi, k, i: (bi, 0, 2 * k + 1)),
                  pl.BlockSpec((1, 1, ATT_TQ, nbp), lambda bi, k, i: (bi, k, i, 0))],
        out_specs=pl.BlockSpec((1, ATT_TQ, hw), lambda bi, k, i: (bi, i, k)),
        out_shape=jax.ShapeDtypeStruct((b, t, ATTN_WIDTH), F32),
        scratch_shapes=[pltpu.VMEM((t, 2 * HEAD_DIM), BF16), pltpu.VMEM((t, 2 * HEAD_DIM), BF16),
                        pltpu.VMEM((GQA * ATT_TQ, 2 * HEAD_DIM), BF16),
                        pltpu.VMEM((1, GQA * ATT_TQ, 1), F32), pltpu.VMEM((1, GQA * ATT_TQ, 2 * HEAD_DIM), F32)],
        compiler_params=_cparams(("parallel", "parallel", "arbitrary"), 32),
        name="sel_prompt",
    )(q, kv_slc, kv_slc, rank)


def _win_prompt_kernel(q_ref, k_ref, v_ref, o_ref, kf_ref, vo_ref):
    kvh, i = pl.program_id(1), pl.program_id(2)

    @pl.when(i == 0)
    def _():
        _fill_key_value_scratch(k_ref, v_ref, kf_ref, vo_ref, 0)

    span = WINDOW + WIN_TQ
    qpos = i * WIN_TQ + (lax.broadcasted_iota(jnp.int32, (GQA * WIN_TQ, 1), 0) & (WIN_TQ - 1))
    k0 = pl.multiple_of(jnp.maximum(i * WIN_TQ - WINDOW, 0), WIN_TQ)
    d = qpos - (k0 + lax.broadcasted_iota(jnp.int32, (1, span), 1))
    qf = jnp.concatenate(
        [jnp.concatenate([q_ref[0, :, g * HEAD_DIM:(g + 1) * HEAD_DIM],
                          _query_features(kvh, g, None, 0, WIN_TQ).astype(BF16)], axis=1) for g in range(GQA)],
        axis=0)
    s = jnp.where((d >= 0) & (d < WINDOW), _bdot_nt(qf, kf_ref[pl.ds(k0, span), :]), NEG_INF)
    e = jnp.exp2(s - jnp.max(s, axis=-1, keepdims=True))
    out = _attn_out(_bdot(e.astype(BF16), vo_ref[pl.ds(k0, span), :]))
    for g in range(GQA):
        o_ref[0, :, g * HEAD_DIM:(g + 1) * HEAD_DIM] = out[g * WIN_TQ:(g + 1) * WIN_TQ]


def _win_prompt(q, kv_win):
    b, t, _ = q.shape
    hw = GQA * HEAD_DIM
    assert t <= POS_RADIX * POS_RADIX
    return pl.pallas_call(
        _win_prompt_kernel,
        grid=(b, N_KV_HEADS, t // WIN_TQ),
        in_specs=[pl.BlockSpec((1, WIN_TQ, hw), lambda bi, k, i: (bi, i, k)),
                  pl.BlockSpec((1, t, HEAD_DIM), lambda bi, k, i: (bi, 0, 2 * k)),
                  pl.BlockSpec((1, t, HEAD_DIM), lambda bi, k, i: (bi, 0, 2 * k + 1))],
        out_specs=pl.BlockSpec((1, WIN_TQ, hw), lambda bi, k, i: (bi, i, k)),
        out_shape=jax.ShapeDtypeStruct((b, t, ATTN_WIDTH), F32),
        scratch_shapes=[pltpu.VMEM((t, 2 * HEAD_DIM), BF16), pltpu.VMEM((t, 2 * HEAD_DIM), BF16)],
        compiler_params=_cparams(("parallel", "parallel", "arbitrary"), 32),
        name="win_prompt",
    )(q, kv_win, kv_win)


def _slope_col(kvh):
    g = lax.broadcasted_iota(jnp.int32, (GQA, 1), 0)
    col = jnp.zeros((GQA, 1), F32)
    for i in range(GQA):
        col = jnp.where(g == i, _alibi_slope(kvh, i), col)
    return col


def _stream_rows(ref, lead, stream, n):
    return ref[lead + (pl.ds(stream, n, stride=2 * N_KV_HEADS), slice(None))]


def _sel_sample_kernel(idx_ref, pt_ref, q_ref, *refs, t_len):
    past, new_ref, o_ref = refs[:N_KV_HEADS * N_SEL], refs[N_KV_HEADS * N_SEL], refs[N_KV_HEADS * N_SEL + 1]
    b, t = pl.program_id(0), pl.program_id(1)
    past_blocks = PAST_LEN // SEL_LEN
    lane = lax.broadcasted_iota(jnp.int32, (1, N_SEL * SEL_LEN), 1)
    for kvh in range(N_KV_HEADS):
        ks, vs = [], []
        base = jnp.zeros((1, N_SEL * SEL_LEN), jnp.int32)
        for r in range(N_SEL):
            blk = idx_ref[((b * N_KV_HEADS + kvh) * t_len + t) * N_SEL + r]
            is_new = blk >= past_blocks
            ref = past[kvh * N_SEL + r]
            ks.append(jnp.where(is_new, _stream_rows(new_ref, (0,), 2 * kvh, SEL_LEN),
                                _stream_rows(ref, (), 2 * kvh, SEL_LEN)).astype(BF16))
            vs.append(jnp.where(is_new, _stream_rows(new_ref, (0,), 2 * kvh + 1, SEL_LEN),
                                _stream_rows(ref, (), 2 * kvh + 1, SEL_LEN)).astype(BF16))
            base = jnp.where((lane >> SEL_SHIFT) == r, blk * SEL_LEN, base)
        d = (PAST_LEN + t) - (base + (lane & (SEL_LEN - 1)))
        rows = slice(kvh * GQA, (kvh + 1) * GQA)
        s = _bdot_nt(q_ref[0, 0, rows, :], jnp.concatenate(ks, axis=0)) - _slope_col(kvh) * d.astype(F32)
        _, e, z = _softmax_parts(s, d >= 0)
        o_ref[0, 0, rows, :] = _bdot(e.astype(BF16), jnp.concatenate(vs, axis=0)) / jnp.maximum(z, F32_TINY)


def _sel_sample(q_h, cache_slc, layer, page_table, idx, kv_new_rows):
    bsz, t_len = q_h.shape[:2]
    n_pages = page_table.shape[1]
    per_page = PAGE_SIZE // SEL_LEN
    past_blocks = PAST_LEN // SEL_LEN
    pages = _rowwise_pages(cache_slc)
    blk_rows = SEL_LEN * 2 * N_KV_HEADS

    def past_spec(kvh, r):
        def imap(b, t, idx_ref, pt_ref):
            blk = jnp.minimum(idx_ref[((b * N_KV_HEADS + kvh) * t_len + t) * N_SEL + r], past_blocks - 1)
            return (layer, pt_ref[b * n_pages + blk // per_page], blk % per_page, 0)
        return pl.BlockSpec((None, None, blk_rows, HEAD_DIM), imap)

    qspec = pl.BlockSpec((1, 1, N_HEADS, HEAD_DIM), lambda b, t, i_, p_: (b, t, 0, 0))
    return pl.pallas_call(
        functools.partial(_sel_sample_kernel, t_len=t_len),
        grid_spec=pltpu.PrefetchScalarGridSpec(
            num_scalar_prefetch=2,
            grid=(bsz, t_len),
            in_specs=[qspec] + [past_spec(kvh, r) for kvh in range(N_KV_HEADS) for r in range(N_SEL)]
            + [pl.BlockSpec((1, blk_rows, HEAD_DIM), lambda b, t, i_, p_: (b, 0, 0))],
            out_specs=qspec,
        ),
        out_shape=jax.ShapeDtypeStruct(q_h.shape, F32),
        compiler_params=_cparams(("parallel", "parallel"), 32),
        name="sel_sample",
    )(idx.reshape(-1), page_table.reshape(-1), q_h, *([pages] * (N_KV_HEADS * N_SEL)), kv_new_rows)


def _win_sample_kernel(q_ref, past_ref, new_ref, o_ref, *, t_len, t_pad, wbuf):
    t = lax.broadcasted_iota(jnp.int32, (t_len, 1), 0)
    d_past = wbuf + t - lax.broadcasted_iota(jnp.int32, (1, wbuf), 1)
    row_new = lax.broadcasted_iota(jnp.int32, (1, t_pad), 1)
    d_new = t - row_new
    valid_past = (d_past >= 0) & (d_past < WINDOW)
    valid_new = (d_new >= 0) & (d_new < WINDOW) & (row_new < t_len)
    for kvh in range(N_KV_HEADS):
        kp = _stream_rows(past_ref, (0,), 2 * kvh, wbuf).astype(BF16)
        vp = _stream_rows(past_ref, (0,), 2 * kvh + 1, wbuf).astype(BF16)
        kn = _stream_rows(new_ref, (0,), 2 * kvh, t_pad).astype(BF16)
        vn = _stream_rows(new_ref, (0,), 2 * kvh + 1, t_pad).astype(BF16)
        for g in range(GQA):
            q = q_ref[0, kvh, g]
            slope = _alibi_slope(kvh, g)
            m1, e1, z1 = _softmax_parts(_bdot_nt(q, kp) - slope * d_past.astype(F32), valid_past)
            m2, e2, z2 = _softmax_parts(_bdot_nt(q, kn) - slope * d_new.astype(F32), valid_new)
            m = jnp.maximum(m1, m2)
            a1 = jnp.where(z1 > 0.0, jnp.exp2(m1 - m), 0.0)
            a2 = jnp.where(z2 > 0.0, jnp.exp2(m2 - m), 0.0)
            num = a1 * _bdot(e1.astype(BF16), vp) + a2 * _bdot(e2.astype(BF16), vn)
            o_ref[0, kvh, g] = num / jnp.maximum(a1 * z1 + a2 * z2, F32_TINY)


def _win_sample(q_g, win_rows, new_rows):
    bsz, _, _, t_len, _ = q_g.shape
    streams = 2 * N_KV_HEADS
    wbuf, t_pad = win_rows.shape[1] // streams, new_rows.shape[1] // streams
    qspec = pl.BlockSpec((1, N_KV_HEADS, GQA, t_len, HEAD_DIM), lambda b: (b, 0, 0, 0, 0))
    return pl.pallas_call(
        functools.partial(_win_sample_kernel, t_len=t_len, t_pad=t_pad, wbuf=wbuf),
        grid=(bsz,),
        in_specs=[qspec, pl.BlockSpec((1, wbuf * streams, HEAD_DIM), lambda b: (b, 0, 0)),
                  pl.BlockSpec((1, t_pad * streams, HEAD_DIM), lambda b: (b, 0, 0))],
        out_specs=qspec,
        out_shape=jax.ShapeDtypeStruct(q_g.shape, F32),
        compiler_params=_cparams(("parallel",), 16),
        name="win_sample",
    )(q_g, win_rows, new_rows)


def _mix_kernel(x_ref, y_ref, oc_ref, os_ref, ow_ref, gate_ref, wglu_ref, gs_ref, ga_ref, wout_ref, o_ref):
    z = _gelu(y_ref[...])
    z = z * jax.nn.sigmoid(_bdot(z.astype(BF16), wglu_ref[...]))
    gate = gate_ref[...]
    cols = []
    for h in range(N_HEADS):
        sl = slice(h * HEAD_DIM, (h + 1) * HEAD_DIM)
        cols.append(gate[:, 3 * h:3 * h + 1] * oc_ref[:, sl] + gate[:, 3 * h + 1:3 * h + 2] * os_ref[:, sl]
                    + gate[:, 3 * h + 2:3 * h + 3] * ow_ref[:, sl])
    attn = jnp.concatenate(cols, axis=1)
    hs = _rms(z, gs_ref[...]).astype(BF16)
    ha = _rms(attn, ga_ref[...]).astype(BF16)
    o_ref[...] = x_ref[...] + (_bdot(hs, wout_ref[:SSM_WIDTH, :]) + _bdot(ha, wout_ref[SSM_WIDTH:, :]))


def _mix(x, y, oc, osl, ow, gates, prm, tm):
    m = x.shape[0]
    row = lambda n: pl.BlockSpec((tm, n), lambda i: (i, 0))
    return pl.pallas_call(
        _mix_kernel,
        grid=(m // tm,),
        in_specs=[row(D_MODEL), row(SSM_WIDTH), row(ATTN_WIDTH), row(ATTN_WIDTH), row(ATTN_WIDTH), row(N_GATES),
                  _resident_layer((SSM_WIDTH, SSM_WIDTH), prm["layer"]), _resident((1, SSM_WIDTH)),
                  _resident((1, ATTN_WIDTH)), _resident_layer((D_MODEL, D_MODEL), prm["layer"])],
        out_specs=row(D_MODEL),
        out_shape=jax.ShapeDtypeStruct((m, D_MODEL), F32),
        compiler_params=_cparams(("parallel",), 48),
        name="mix_outproj",
    )(x, y, oc, osl, ow, gates, prm["ssm_w_glu"], prm["norm_ssm_out"], prm["norm_attn_out"], prm["w_out"])


FFN_TF = 512


def _ffn_kernel(x_ref, g_ref, wg_ref, wu_ref, wd_ref, o_ref, xn_ref, acc_ref):
    j = pl.program_id(1)

    @pl.when(j == 0)
    def _():
        xn_ref[...] = _rms(x_ref[...], g_ref[...]).astype(BF16)
        acc_ref[...] = jnp.zeros(acc_ref.shape, F32)

    xn = xn_ref[...]
    a = _bdot(xn, wg_ref[...])
    h = (a * jax.nn.sigmoid(a)) * _bdot(xn, wu_ref[...])
    acc_ref[...] += _bdot(h.astype(BF16), wd_ref[...])

    @pl.when(j == pl.num_programs(1) - 1)
    def _():
        o_ref[...] = x_ref[...] + acc_ref[...]


def _ffn(x, prm, tm):
    m = x.shape[0]
    layer = prm["layer"]
    return pl.pallas_call(
        _ffn_kernel,
        grid=(m // tm, D_FF // FFN_TF),
        in_specs=[pl.BlockSpec((tm, D_MODEL), lambda i, j: (i, 0)),
                  pl.BlockSpec((1, D_MODEL), lambda i, j: (0, 0)),
                  pl.BlockSpec((None, D_MODEL, FFN_TF), lambda i, j: (layer, 0, j)),
                  pl.BlockSpec((None, D_MODEL, FFN_TF), lambda i, j: (layer, 0, j)),
                  pl.BlockSpec((None, FFN_TF, D_MODEL), lambda i, j: (layer, j, 0))],
        out_specs=pl.BlockSpec((tm, D_MODEL), lambda i, j: (i, 0)),
        out_shape=jax.ShapeDtypeStruct((m, D_MODEL), F32),
        scratch_shapes=[pltpu.VMEM((tm, D_MODEL), BF16), pltpu.VMEM((tm, D_MODEL), F32)],
        compiler_params=_cparams(("parallel", "arbitrary"), 48),
        name="ffn",
    )(x, prm["norm_ffn"], prm["w_ffn_gate"], prm["w_ffn_up"], prm["w_ffn_down"])


S5_CHUNK = 64
S5_GROUPS_PER_STEP = 2
SAMPLE_S5_PAD = 8


BIG_WEIGHTS = ("w_in", "ssm_w_glu", "w_out", "w_ffn_gate", "w_ffn_up", "w_ffn_down")


def _layer_params(l, w, big):
    def w1cat(w1):
        half = CMP_STRIDE * HEAD_DIM
        return jnp.concatenate([w1[:half], w1[half:]], axis=1).astype(BF16)

    row = lambda v: v.reshape(1, -1)
    a_re, a_im = w["ssm_a_re"][l], w["ssm_a_im"][l]
    log_dt = jnp.broadcast_to(w["ssm_log_dt"][l][:, None], a_re.shape)
    lrow = jnp.stack([a_re, a_im, log_dt], axis=1)
    d = w["ssm_d"][l]
    return {
        "layer": l, **big, "norm_mix": row(w["norm_mix"][l]),
        "q_norm": row(w["q_norm"][l]), "k_norm_slc": row(w["k_norm_slc"][l]),
        "k_norm_win": row(w["k_norm_win"][l]), "k_norm_cmp": row(w["k_norm_cmp"][l]),
        "lrow": lrow, "lcol": lrow.transpose(0, 2, 1),
        "bt": jnp.stack([w["ssm_b_re"][l], w["ssm_b_im"][l]], axis=1).transpose(0, 1, 3, 2),
        "bn": jnp.stack([w["ssm_b_re"][l], w["ssm_b_im"][l]], axis=1),
        "ct": jnp.stack([w["ssm_c_re"][l], w["ssm_c_im"][l]], axis=1).transpose(0, 1, 3, 2),
        "dtile": {lp: jnp.tile(d, (1, lp))[:, None, :] for lp in (S5_CHUNK, SAMPLE_S5_PAD)},
        "cmp_w1_k": w1cat(w["cmp_w1_k"][l]), "cmp_w1_v": w1cat(w["cmp_w1_v"][l]),
        "cmp_b1": jnp.stack([w["cmp_b1_k"][l], w["cmp_b1_v"][l]]),
        "cmp_w2_k": w["cmp_w2_k"][l].astype(BF16), "cmp_w2_v": w["cmp_w2_v"][l].astype(BF16),
        "norm_ssm_out": row(w["norm_ssm_out"][l]), "norm_attn_out": row(w["norm_attn_out"][l]),
        "norm_ffn": row(w["norm_ffn"][l]),
    }


def _kv_out(kv, b, t):
    return kv.reshape(b, t, N_KV_HEADS, 2, HEAD_DIM)


PROMPT_TM = 256
FFN_TM = 512


def _prompt_layer(x, prm):
    b, t, _ = x.shape
    m = b * t
    x2 = x.reshape(m, D_MODEL)
    u, q, kv_cmp, kv_slc, kv_win, gates = _inproj(x2, prm, PROMPT_TM)
    h0 = jnp.zeros((b, SSM_GROUPS, SSM_STATE, 2), F32)
    y, h_last = _s5_mixer(u.reshape(b, t, SSM_WIDTH), h0, prm, S5_CHUNK, S5_CHUNK)
    q3 = q.reshape(b, t, ATTN_WIDTH)
    kv_cmp3, kv_slc3, kv_win3 = (a.reshape(b, t, 2 * KV_WIDTH) for a in (kv_cmp, kv_slc, kv_win))
    ckv = _compress_prompt(kv_cmp3, prm)
    o_cmp, rank = _cmp_select(q3, ckv, 256, t // SEL_LEN, 0, False)
    o_slc = _sel_prompt(q3, kv_slc3, rank, t // SEL_LEN)
    o_win = _win_prompt(q3, kv_win3)
    x2 = _mix(x2, y.reshape(m, SSM_WIDTH), o_cmp.reshape(m, -1), o_slc.reshape(m, -1), o_win.reshape(m, -1),
              gates, prm, PROMPT_TM)
    x2 = _ffn(x2, prm, FFN_TM)
    wbuf = min(WINDOW, PAST_LEN)
    win_state = kv_win3[:, t - wbuf:] if t >= wbuf else jnp.pad(kv_win3, ((0, 0), (wbuf - t, 0), (0, 0)))
    return (x2.reshape(b, t, D_MODEL), _kv_out(kv_cmp3, b, t), _kv_out(kv_slc3, b, t),
            _kv_out(win_state, b, wbuf), h_last)


def _sample_layer(x, prm, cache_cmp, cache_slc, layer, page_table, win_buf, h0):
    b, t, _ = x.shape
    m = b * t
    assert (PAST_LEN + t) // CMP_STRIDE == PAST_LEN // CMP_STRIDE and PAST_LEN % PAGE_SIZE == 0
    x2 = x.reshape(m, D_MODEL)
    u, q, kv_cmp, kv_slc, kv_win, gates = _inproj(x2, prm, m)
    y, h_last = _s5_mixer(u.reshape(b, t, SSM_WIDTH), h0, prm, SAMPLE_S5_PAD, t)
    q3 = q.reshape(b, t, ATTN_WIDTH)
    kv_cmp3, kv_slc3, kv_win3 = (a.reshape(b, t, 2 * KV_WIDTH) for a in (kv_cmp, kv_slc, kv_win))
    ckv = _compress_sample(cache_cmp, layer, page_table, prm)
    n_blocks = -(-(PAST_LEN + t) // SEL_LEN)
    o_cmp, _, idx = _cmp_select(q3, ckv, t, n_blocks, PAST_LEN, True)
    streams = 2 * N_KV_HEADS

    def stream_rows(kv3, t_pad):
        return jnp.pad(kv3.reshape(b, t * streams, HEAD_DIM), ((0, 0), (0, (t_pad - t) * streams), (0, 0)))

    o_slc = _sel_sample(q3.reshape(b, t, N_HEADS, HEAD_DIM), cache_slc, layer, page_table, idx[..., :N_SEL],
                        stream_rows(kv_slc3, SEL_LEN)).reshape(m, ATTN_WIDTH)
    wbuf = win_buf.shape[1]
    q5 = q3.reshape(b, t, N_KV_HEADS, GQA, HEAD_DIM)
    o_win = _win_sample(q5.transpose(0, 2, 3, 1, 4), win_buf.reshape(b, wbuf * streams, HEAD_DIM),
                        stream_rows(kv_win3, 8))
    o_win = o_win.transpose(0, 3, 1, 2, 4).reshape(m, ATTN_WIDTH)
    x2 = _mix(x2, y.reshape(m, SSM_WIDTH), o_cmp.reshape(m, -1), o_slc, o_win, gates, prm, m)
    x2 = _ffn(x2, prm, m)
    win_state = jnp.concatenate([win_buf, _kv_out(kv_win3, b, t)], axis=1)[:, -wbuf:]
    return (x2.reshape(b, t, D_MODEL), _kv_out(kv_cmp3, b, t), _kv_out(kv_slc3, b, t), win_state, h_last)


def kernel(x_prompt, x_sample, cache_cmp_kv, cache_slc_kv, state_win_kv, state_ssm, page_table, norm_mix, w_in, ssm_a_re, ssm_a_im, ssm_log_dt, ssm_b_re, ssm_b_im, ssm_c_re, ssm_c_im, ssm_d, ssm_w_glu, q_norm, k_norm_cmp, k_norm_slc, k_norm_win, cmp_w1_k, cmp_b1_k, cmp_w2_k, cmp_w1_v, cmp_b1_v, cmp_w2_v, norm_ssm_out, norm_attn_out, w_out, norm_ffn, w_ffn_gate, w_ffn_up, w_ffn_down):
    w = dict(norm_mix=norm_mix, w_in=w_in, ssm_a_re=ssm_a_re, ssm_a_im=ssm_a_im, ssm_log_dt=ssm_log_dt,
             ssm_b_re=ssm_b_re, ssm_b_im=ssm_b_im, ssm_c_re=ssm_c_re, ssm_c_im=ssm_c_im, ssm_d=ssm_d,
             ssm_w_glu=ssm_w_glu, q_norm=q_norm, k_norm_cmp=k_norm_cmp, k_norm_slc=k_norm_slc,
             k_norm_win=k_norm_win, cmp_w1_k=cmp_w1_k, cmp_b1_k=cmp_b1_k, cmp_w2_k=cmp_w2_k,
             cmp_w1_v=cmp_w1_v, cmp_b1_v=cmp_b1_v, cmp_w2_v=cmp_w2_v, norm_ssm_out=norm_ssm_out,
             norm_attn_out=norm_attn_out, w_out=w_out, norm_ffn=norm_ffn, w_ffn_gate=w_ffn_gate,
             w_ffn_up=w_ffn_up, w_ffn_down=w_ffn_down)
    big = {name: w[name].astype(BF16) for name in BIG_WEIGHTS}
    y_p, y_s = x_prompt, x_sample
    outs_p, outs_s = [], []
    for l in range(DEPTH):
        prm = _layer_params(l, w, big)
        y_p, *rest = _prompt_layer(y_p, prm)
        outs_p.append(rest)
        y_s, *rest = _sample_layer(y_s, prm, cache_cmp_kv, cache_slc_kv, l, page_table,
                                   state_win_kv[l], state_ssm[l])
        outs_s.append(rest)
    stack = lambda outs, i: jnp.stack([o[i] for o in outs])
    return (y_p, y_s, stack(outs_p, 0), stack(outs_p, 1), stack(outs_p, 2), stack(outs_p, 3),
            stack(outs_s, 0), stack(outs_s, 1), stack(outs_s, 2), stack(outs_s, 3))
```

```python
import functools
import math

import jax
import jax.numpy as jnp
import numpy as np
from jax import lax
from jax.experimental import pallas as pl
from jax.experimental.pallas import tpu as pltpu

F32 = jnp.float32
BF16 = jnp.bfloat16

D_MODEL = 2048
DEPTH = 2
PAST_LEN = 16384
PAGE_SIZE = 128
SSM_WIDTH = 1024
ATTN_WIDTH = 1024
SSM_CH = 16
SSM_GROUPS = 64
SSM_STATE = 64
HEAD_DIM = 128
N_HEADS = 8
N_KV_HEADS = 2
GQA = 4
KV_WIDTH = 256
CMP_LEN = 32
CMP_STRIDE = 16
SEL_LEN = 64
SEL_SHIFT = SEL_LEN.bit_length() - 1
N_SEL = 16
WINDOW = 512
WIN_TQ = 256
FORCED_SCORE = 1e4
D_FF = 5632
NORM_EPS = 1e-6
LOG2E = math.log2(math.e)
QSCALE = HEAD_DIM ** -0.5 * LOG2E
N_GATES = N_HEADS * 3
IN_WIDTH = SSM_WIDTH + ATTN_WIDTH + 6 * KV_WIDTH + N_GATES
F32_TINY = float(jnp.finfo(jnp.float32).tiny)
NEG_INF = float("-inf")

V7X_VMEM_BYTES = 64 * 1024 * 1024
V7X_LANES = 128


def _cparams(semantics, vmem_mib):
    assert vmem_mib * 1024 * 1024 < V7X_VMEM_BYTES
    return pltpu.CompilerParams(dimension_semantics=semantics, vmem_limit_bytes=vmem_mib * 1024 * 1024)


def _resident(shape):
    nd = len(shape)
    return pl.BlockSpec(shape, lambda *_: (0,) * nd, pipeline_mode=pl.Buffered(1))


def _resident_layer(shape, layer):
    nd = len(shape)
    return pl.BlockSpec((None,) + tuple(shape), lambda *_: (layer,) + (0,) * nd, pipeline_mode=pl.Buffered(1))


def _rms(x, g):
    return x * lax.rsqrt(jnp.mean(x * x, axis=-1, keepdims=True) + NORM_EPS) * g


def _gelu(x):
    c = math.sqrt(2.0 / math.pi)
    return x * (0.5 * (1.0 + jnp.tanh(c * (x + 0.044715 * (x * x * x)))))


def _bdot(a, b):
    return jnp.dot(a, b, preferred_element_type=F32)


def _bdot_nt(a, b):
    return lax.dot_general(a, b, (((1,), (1,)), ((), ())), preferred_element_type=F32)


def _split3(x):
    hi = x.astype(BF16)
    r1 = x - hi.astype(F32)
    mid = r1.astype(BF16)
    lo = (r1 - mid.astype(F32)).astype(BF16)
    return hi, mid, lo


def _dot_small_int(x, e):
    hi, mid, lo = _split3(x)
    return _bdot(hi, e) + _bdot(mid, e) + _bdot(lo, e)


def _dot_f32(a, b):
    ah, am, al = _split3(a)
    bh, bm, bl = _split3(b)
    return (_bdot(ah, bh) + (_bdot(ah, bm) + _bdot(am, bh))
            + (_bdot(ah, bl) + _bdot(al, bh) + _bdot(am, bm)))


def _softmax_parts(s, valid):
    s = jnp.where(valid, s, NEG_INF)
    m = jnp.max(s, axis=-1, keepdims=True)
    m = jnp.where(m == NEG_INF, 0.0, m)
    e = jnp.exp2(s - m)
    return m, e, jnp.sum(e, axis=-1, keepdims=True)


def _flash_step(m_ref, acc_ref, g, s, v_ones):
    m_old = m_ref[g]
    m_new = jnp.maximum(m_old, jnp.max(s, axis=-1, keepdims=True))
    m_safe = jnp.where(m_new == NEG_INF, 0.0, m_new)
    acc_ref[g] = jnp.exp2(m_old - m_safe) * acc_ref[g] + _bdot(jnp.exp2(s - m_safe).astype(BF16), v_ones)
    m_ref[g] = m_new


def _alibi_slope(kvh, g):
    return LOG2E * 2.0 ** (-8.0 * (kvh * GQA + g + 1) / N_HEADS)


def _bf16_terms(x, n=3):
    out = []
    for _ in range(n):
        hi = float(np.asarray(x, dtype=np.float32).astype(BF16))
        out.append(hi)
        x -= hi
    return out


def _head_slope(kvh, g):
    assert N_KV_HEADS == 2
    return jnp.where(kvh == 0, _alibi_slope(0, g), _alibi_slope(1, g))


def _inproj_kernel(x_ref, gmix_ref, w_ref, qn_ref, ksn_ref, kwn_ref,
                   u_ref, q_ref, cmp_ref, slc_ref, win_ref, gate_ref):
    xn = _rms(x_ref[...], gmix_ref[...]).astype(BF16)

    def proj(c0, c1):
        return _bdot(xn, w_ref[:, c0:c1])

    u_ref[...] = proj(0, SSM_WIDTH)
    q = proj(SSM_WIDTH, D_MODEL)
    for h in range(N_HEADS):
        sl = slice(h * HEAD_DIM, (h + 1) * HEAD_DIM)
        q_ref[:, sl] = (_rms(q[:, sl], qn_ref[...]) * QSCALE).astype(BF16)
    c0 = D_MODEL
    for ref, nref in ((cmp_ref, None), (slc_ref, ksn_ref), (win_ref, kwn_ref)):
        k = proj(c0, c0 + KV_WIDTH)
        v = proj(c0 + KV_WIDTH, c0 + 2 * KV_WIDTH)
        c0 += 2 * KV_WIDTH
        for kvh in range(N_KV_HEADS):
            src = slice(kvh * HEAD_DIM, (kvh + 1) * HEAD_DIM)
            dst = kvh * 2 * HEAD_DIM
            ref[:, dst:dst + HEAD_DIM] = k[:, src] if nref is None else _rms(k[:, src], nref[...])
            ref[:, dst + HEAD_DIM:dst + 2 * HEAD_DIM] = v[:, src]
    gate_ref[...] = jax.nn.sigmoid(proj(c0, c0 + N_GATES))


def _inproj(x, prm, tm):
    m = x.shape[0]
    row = lambda n: pl.BlockSpec((tm, n), lambda i: (i, 0))
    kvw = 2 * KV_WIDTH
    return pl.pallas_call(
        _inproj_kernel,
        grid=(m // tm,),
        in_specs=[row(D_MODEL), _resident((1, D_MODEL)), _resident_layer((D_MODEL, IN_WIDTH), prm["layer"]),
                  _resident((1, HEAD_DIM)), _resident((1, HEAD_DIM)), _resident((1, HEAD_DIM))],
        out_specs=[row(SSM_WIDTH), row(ATTN_WIDTH), row(kvw), row(kvw), row(kvw), row(N_GATES)],
        out_shape=[jax.ShapeDtypeStruct((m, SSM_WIDTH), F32), jax.ShapeDtypeStruct((m, ATTN_WIDTH), BF16),
                   jax.ShapeDtypeStruct((m, kvw), F32), jax.ShapeDtypeStruct((m, kvw), F32),
                   jax.ShapeDtypeStruct((m, kvw), F32), jax.ShapeDtypeStruct((m, N_GATES), F32)],
        compiler_params=_cparams(("parallel",), 48),
        name="inproj",
    )(x, prm["norm_mix"], prm["w_in"], prm["q_norm"], prm["k_norm_slc"], prm["k_norm_win"])


def _s5_group_body(gi, u_ref, h0_ref, lrow_ref, lcol_ref, bt_ref, bn_ref, ct_ref, d_ref,
                   y_ref, hl_ref, tm_ref, x_ref, hin_ref, *, lp, lreal, nb, nc):
    k = lp * SSM_CH
    n = SSM_STATE

    def lam_bar(a_re, a_im, log_dt):
        dt = jnp.exp(log_dt)
        e = jnp.exp(a_re * dt)
        return e * jnp.cos(a_im * dt), e * jnp.sin(a_im * dt)

    def zoh_coef(l_re, l_im, a_re, a_im):
        den = a_re * a_re + a_im * a_im
        x_re = l_re - 1.0
        return (x_re * a_re + l_im * a_im) / den, (l_im * a_re - x_re * a_im) / den

    lc = lcol_ref[gi]
    a_re_c, a_im_c = lc[:, 0:1], lc[:, 1:2]
    l_re_c, l_im_c = lam_bar(a_re_c, a_im_c, lc[:, 2:3])
    lr = lrow_ref[gi]
    a_re_r, a_im_r = lr[0:1], lr[1:2]
    l_re_r, l_im_r = lam_bar(a_re_r, a_im_r, lr[2:3])

    def pow_table(e):
        p_re = jnp.ones((n, k), F32)
        p_im = jnp.zeros((n, k), F32)
        s_re, s_im = l_re_c, l_im_c
        for j in range(max(lp.bit_length() - 1, 1)):
            bit = ((e >> j) & 1) == 1
            m_re = jnp.where(bit, s_re, 1.0)
            m_im = jnp.where(bit, s_im, 0.0)
            p_re, p_im = p_re * m_re - p_im * m_im, p_re * m_im + p_im * m_re
            s_re, s_im = s_re * s_re - s_im * s_im, 2.0 * s_re * s_im
        return p_re, p_im

    lane = lax.broadcasted_iota(jnp.int32, (1, k), 1)
    tau = lane >> (SSM_CH.bit_length() - 1)
    ch_onehot = jnp.where((lax.broadcasted_iota(jnp.int32, (SSM_CH, k), 1) & (SSM_CH - 1))
                          == lax.broadcasted_iota(jnp.int32, (SSM_CH, k), 0), 1.0, 0.0).astype(BF16)

    p_re, p_im = pow_table(tau)
    c_re = _dot_small_int(ct_ref[gi, 0], ch_onehot)
    c_im = _dot_small_int(ct_ref[gi, 1], ch_onehot)
    g_re = c_re * p_re - c_im * p_im
    g_im = c_re * p_im + c_im * p_re

    cf_re_r, cf_im_r = zoh_coef(l_re_r, l_im_r, a_re_r, a_im_r)
    bbt_re = cf_re_r * bt_ref[gi, 0] - cf_im_r * bt_ref[gi, 1]
    bbt_im = cf_re_r * bt_ref[gi, 1] + cf_im_r * bt_ref[gi, 0]
    a = _dot_f32(bbt_re, g_re) - _dot_f32(bbt_im, g_im)

    lane16 = lax.broadcasted_iota(jnp.int32, (SSM_CH, k), 1)
    for s in range(lp):
        blk = a if s == 0 else jnp.where(lane16 >= SSM_CH * s, pltpu.roll(a, SSM_CH * s, axis=1), 0.0)
        tm_ref[gi, SSM_CH * s:SSM_CH * (s + 1), :] = blk.astype(BF16)

    cf_re_c, cf_im_c = zoh_coef(l_re_c, l_im_c, a_re_c, a_im_c)
    bbn_re = cf_re_c * bn_ref[gi, 0] - cf_im_c * bn_ref[gi, 1]
    bbn_im = cf_re_c * bn_ref[gi, 1] + cf_im_c * bn_ref[gi, 0]
    bb_re = _dot_small_int(bbn_re, ch_onehot)
    bb_im = _dot_small_int(bbn_im, ch_onehot)
    rev = (lreal - 1) - tau
    q_re, q_im = pow_table(jnp.maximum(rev, 0))
    live = rev >= 0
    zeros_nk = jnp.zeros((n, k), BF16)
    wx_re = jnp.concatenate([jnp.where(live, q_re * bb_re - q_im * bb_im, 0.0).astype(BF16), zeros_nk], axis=0)
    wx_im = jnp.concatenate([jnp.where(live, q_re * bb_im + q_im * bb_re, 0.0).astype(BF16), zeros_nk], axis=0)

    u = u_ref[gi]
    ub = u.astype(BF16)
    x_ref[gi, 0] = _bdot_nt(ub, wx_re)
    x_ref[gi, 1] = _bdot_nt(ub, wx_im)

    ll_re, ll_im = l_re_r, l_im_r
    for _ in range(lreal.bit_length() - 1):
        ll_re, ll_im = ll_re * ll_re - ll_im * ll_im, 2.0 * ll_re * ll_im
    ll_re = jnp.concatenate([ll_re, jnp.zeros_like(ll_re)], axis=1)
    ll_im = jnp.concatenate([ll_im, jnp.zeros_like(ll_im)], axis=1)
    lower = lax.broadcasted_iota(jnp.int32, (1, 2 * n), 1) < n
    h0 = h0_ref[gi]
    h_re = jnp.where(lower, h0, 0.0)
    h_im = jnp.where(lower, pltpu.roll(h0, n, axis=1), 0.0)
    for c in range(nc):
        rows = pl.ds(c, nb, stride=nc) if nc > 1 else slice(0, nb)
        hin_ref[gi, 0, rows, :] = h_re
        hin_ref[gi, 1, rows, :] = h_im
        h_re, h_im = (ll_re * h_re - ll_im * h_im + x_ref[gi, 0, rows, :],
                      ll_re * h_im + ll_im * h_re + x_ref[gi, 1, rows, :])
    hl_ref[gi] = jnp.where(lower, h_re, pltpu.roll(h_im, n, axis=1))

    g1_re = jnp.concatenate([(g_re * l_re_c - g_im * l_im_c).astype(BF16), zeros_nk], axis=0)
    g1_im = jnp.concatenate([(g_re * l_im_c + g_im * l_re_c).astype(BF16), zeros_nk], axis=0)
    y_state = _bdot(hin_ref[gi, 0].astype(BF16), g1_re) - _bdot(hin_ref[gi, 1].astype(BF16), g1_im)

    y_ref[gi] = _bdot(ub, tm_ref[gi]) + y_state + u * d_ref[gi]


def _s5_kernel(*refs, gb, **kw):
    for gi in range(gb):
        _s5_group_body(gi, *refs, **kw)


def _s5(u_g, h0_g, prm, lp, lreal, nb, nc):
    g, m, k = u_g.shape
    gb = S5_GROUPS_PER_STEP
    blk = lambda *s: pl.BlockSpec((gb,) + s, lambda i: (i,) + (0,) * len(s))
    return pl.pallas_call(
        functools.partial(_s5_kernel, gb=gb, lp=lp, lreal=lreal, nb=nb, nc=nc),
        grid=(g // gb,),
        in_specs=[blk(m, k), blk(nb, 2 * SSM_STATE), blk(3, SSM_STATE), blk(SSM_STATE, 3),
                  blk(2, SSM_CH, SSM_STATE), blk(2, SSM_STATE, SSM_CH), blk(2, SSM_STATE, SSM_CH), blk(1, k)],
        out_specs=[blk(m, k), blk(nb, 2 * SSM_STATE)],
        out_shape=[jax.ShapeDtypeStruct((g, m, k), F32), jax.ShapeDtypeStruct((g, nb, 2 * SSM_STATE), F32)],
        scratch_shapes=[pltpu.VMEM((gb, k, k), BF16), pltpu.VMEM((gb, 2, m, 2 * SSM_STATE), F32),
                        pltpu.VMEM((gb, 2, m, 2 * SSM_STATE), F32)],
        compiler_params=_cparams(("parallel",), 32),
        name="s5_group",
    )(u_g, h0_g, prm["lrow"], prm["lcol"], prm["bt"], prm["bn"], prm["ct"], prm["dtile"][lp])


def _s5_mixer(u, h0, prm, lp, lreal):
    b, t, _ = u.shape
    nc = t // lreal
    ug = u.reshape(b * nc, lreal, SSM_GROUPS, SSM_CH)
    if lp != lreal:
        ug = jnp.pad(ug, ((0, 0), (0, lp - lreal), (0, 0), (0, 0)))
    ug = ug.transpose(2, 0, 1, 3).reshape(SSM_GROUPS, b * nc, lp * SSM_CH)
    h0g = h0.transpose(1, 0, 3, 2).reshape(SSM_GROUPS, b, 2 * SSM_STATE)
    yg, hl = _s5(ug, h0g, prm, lp, lreal, b, nc)
    y = yg.reshape(SSM_GROUPS, b * nc, lp, SSM_CH)[:, :, :lreal].transpose(1, 2, 0, 3)
    h_last = hl.reshape(SSM_GROUPS, b, 2, SSM_STATE).transpose(1, 0, 3, 2)
    return y.reshape(b, t, SSM_WIDTH), h_last


def _cmp_proj_kernel(*refs, n_in, rowwise):
    refs = refs[len(refs) - n_in - 3:]
    x_refs, (w1k_ref, w1v_ref, o_ref) = refs[:n_in], refs[n_in:]

    def rows_of(j, r):
        if rowwise:
            n = x_refs[0].shape[0] // (4 * CMP_STRIDE)
            parts = [x[pl.ds(4 * r + j, n, stride=4 * CMP_STRIDE), :] for x in x_refs]
            return parts[0] if n_in == 1 else jnp.concatenate(parts, axis=0)
        x = x_refs[j]
        return x[0, pl.ds(r, x.shape[1] // CMP_STRIDE, stride=CMP_STRIDE), :]

    for j in range(4):
        c = jnp.concatenate([rows_of(j, r) for r in range(CMP_STRIDE)], axis=1).astype(BF16)
        w = w1k_ref if j % 2 == 0 else w1v_ref
        o_ref[0, :, j * 2 * HEAD_DIM:(j + 1) * 2 * HEAD_DIM] = _bdot(c, w[...])


def _cmp_combine_kernel(h_ref, b1_ref, w2k_ref, w2v_ref, kn_ref, o_ref, *, nchunk):
    h = h_ref[0]
    keep = lax.broadcasted_iota(jnp.int32, (nchunk, 1), 0) < nchunk - 1
    for j in range(4):
        kvh, is_v = j // 2, j % 2
        lo = h[:, j * 2 * HEAD_DIM: j * 2 * HEAD_DIM + HEAD_DIM]
        hi = h[:, j * 2 * HEAD_DIM + HEAD_DIM:(j + 1) * 2 * HEAD_DIM]
        pre = lo + pltpu.roll(hi, nchunk - 1, axis=0) + b1_ref[is_v:is_v + 1, :]
        out = _bdot(_gelu(pre).astype(BF16), (w2v_ref if is_v else w2k_ref)[...])
        if not is_v:
            out = _rms(out, kn_ref[...])
        c0 = is_v * N_KV_HEADS * HEAD_DIM + kvh * HEAD_DIM
        o_ref[0, :, c0:c0 + HEAD_DIM] = jnp.where(keep, out, 0.0)


def _cmp_combine(h, prm):
    b, nchunk, w = h.shape
    return pl.pallas_call(
        functools.partial(_cmp_combine_kernel, nchunk=nchunk),
        grid=(b,),
        in_specs=[pl.BlockSpec((1, nchunk, w), lambda i: (i, 0, 0)), _resident((2, HEAD_DIM)),
                  _resident((HEAD_DIM, HEAD_DIM)), _resident((HEAD_DIM, HEAD_DIM)), _resident((1, HEAD_DIM))],
        out_specs=pl.BlockSpec((1, nchunk, 4 * HEAD_DIM), lambda i: (i, 0, 0)),
        out_shape=jax.ShapeDtypeStruct((b, nchunk, 4 * HEAD_DIM), F32),
        compiler_params=_cparams(("parallel",), 40),
        name="cmp_combine",
    )(h, prm["cmp_b1"], prm["cmp_w2_k"], prm["cmp_w2_v"], prm["k_norm_cmp"])


def _compress_prompt(kv_cmp, prm):
    b, t, w = kv_cmp.shape
    nchunk = t // CMP_STRIDE
    streams = w // HEAD_DIM
    h = pl.pallas_call(
        functools.partial(_cmp_proj_kernel, n_in=streams, rowwise=False),
        grid=(b,),
        in_specs=[pl.BlockSpec((1, t, HEAD_DIM), functools.partial(lambda i, j: (i, 0, j), j=j))
                  for j in range(streams)] + [
                  _resident((CMP_STRIDE * HEAD_DIM, 2 * HEAD_DIM)), _resident((CMP_STRIDE * HEAD_DIM, 2 * HEAD_DIM))],
        out_specs=pl.BlockSpec((1, nchunk, 8 * HEAD_DIM), lambda i: (i, 0, 0)),
        out_shape=jax.ShapeDtypeStruct((b, nchunk, 8 * HEAD_DIM), F32),
        compiler_params=_cparams(("parallel",), 40),
        name="cmp_proj_prompt",
    )(*([kv_cmp] * streams), prm["cmp_w1_k"], prm["cmp_w1_v"])
    return _cmp_combine(h, prm)


PAGES_PER_STEP = 16


def _rowwise_pages(cache):
    return cache.reshape(cache.shape[0], cache.shape[1], PAGE_SIZE * 2 * N_KV_HEADS, HEAD_DIM)


def _compress_sample(cache_cmp, layer, page_table, prm):
    bsz, n_pages = page_table.shape
    chunks_per_page = PAGE_SIZE // CMP_STRIDE
    pages = _rowwise_pages(cache_cmp)
    steps = n_pages // PAGES_PER_STEP

    def page_spec(p):
        return pl.BlockSpec((None, None, 4 * PAGE_SIZE, HEAD_DIM),
                            lambda b, s, pt: (layer, pt[b * n_pages + s * PAGES_PER_STEP + p], 0, 0))

    rows = PAGES_PER_STEP * chunks_per_page
    h = pl.pallas_call(
        functools.partial(_cmp_proj_kernel, n_in=PAGES_PER_STEP, rowwise=True),
        grid_spec=pltpu.PrefetchScalarGridSpec(
            num_scalar_prefetch=1,
            grid=(bsz, steps),
            in_specs=[page_spec(p) for p in range(PAGES_PER_STEP)] + [
                pl.BlockSpec((CMP_STRIDE * HEAD_DIM, 2 * HEAD_DIM), lambda b, s, pt: (0, 0)),
                pl.BlockSpec((CMP_STRIDE * HEAD_DIM, 2 * HEAD_DIM), lambda b, s, pt: (0, 0))],
            out_specs=pl.BlockSpec((1, rows, 8 * HEAD_DIM), lambda b, s, pt: (b, s, 0)),
        ),
        out_shape=jax.ShapeDtypeStruct((bsz, n_pages * chunks_per_page, 8 * HEAD_DIM), F32),
        compiler_params=_cparams(("parallel", "parallel"), 40),
        name="cmp_proj_sample",
    )(page_table.reshape(-1), *([pages] * PAGES_PER_STEP), prm["cmp_w1_k"], prm["cmp_w1_v"])
    return _cmp_combine(h, prm)


def _cmp_select_kernel(q_ref, ck_ref, cv_ref, o_ref, rank_ref, *idx_ref, tq, nc, nbp, nblk, qpos0):
    kvh = pl.program_id(1)
    qpos = qpos0 + pl.program_id(2) * tq + lax.broadcasted_iota(jnp.int32, (tq, 1), 0)
    assert tq & (tq - 1) == 0
    start = lax.broadcasted_iota(jnp.int32, (1, nc), 1) * CMP_STRIDE
    ck = ck_ref[0].astype(BF16)
    cv = cv_ref[0].astype(BF16)
    q_rows = jnp.concatenate([q_ref[0, :, g * HEAD_DIM:(g + 1) * HEAD_DIM] for g in range(GQA)], axis=0)
    row = lax.broadcasted_iota(jnp.int32, (GQA * tq, 1), 0)
    qpos_rows = qpos0 + pl.program_id(2) * tq + (row & (tq - 1))
    slope_rows = jnp.zeros((GQA * tq, 1), F32)
    for g in range(GQA):
        slope_rows = jnp.where(row >= g * tq, _head_slope(kvh, g), slope_rows)
    bias = slope_rows * ((qpos_rows - start).astype(F32) - (CMP_LEN - 1) / 2)
    _, e, den = _softmax_parts(_bdot_nt(q_rows, ck) - bias, (start + (CMP_LEN - 1)) <= qpos_rows)
    p = e / jnp.maximum(den, F32_TINY)
    o_rows = _bdot(p.astype(BF16), cv)
    psum = jnp.zeros((tq, nc), F32)
    for g in range(GQA):
        o_ref[0, :, g * HEAD_DIM:(g + 1) * HEAD_DIM] = o_rows[g * tq:(g + 1) * tq]
        psum = psum + p[g * tq:(g + 1) * tq]

    per_shift = (SEL_LEN // CMP_STRIDE).bit_length() - 1

    def ranks(blocks_on_rows):
        nbs = -(-nblk // 8) * 8
        ov_shape = (nbs, nc) if blocks_on_rows else (nc, nbp)
        ci = lax.broadcasted_iota(jnp.int32, ov_shape, 1 if blocks_on_rows else 0)
        bj = lax.broadcasted_iota(jnp.int32, ov_shape, 0 if blocks_on_rows else 1)
        overlap = (jnp.where(ci >> per_shift == bj, 1.0, 0.0)
                   + jnp.where((ci + 1) >> per_shift == bj, 1.0, 0.0)).astype(BF16)
        if blocks_on_rows:
            score = sum(_bdot_nt(overlap, part) for part in _split3(psum))
            j = lax.broadcasted_iota(jnp.int32, (nbs, 1), 0)
            pos = qpos0 + pl.program_id(2) * tq + lax.broadcasted_iota(jnp.int32, (1, tq), 1)
        else:
            score = _dot_small_int(psum, overlap)
            j = lax.broadcasted_iota(jnp.int32, (1, nbp), 1)
            pos = qpos
        qb = pos >> SEL_SHIFT
        forced = (j == 0) | (j == qb) | (j == qb - 1)
        score = jnp.where(forced, FORCED_SCORE, jnp.where((j * SEL_LEN) <= pos, score, -1.0))
        score = jnp.where(j < nblk, score, -2.0)
        rank = jnp.zeros(score.shape, F32)
        for i in range(nblk):
            si = score[i:i + 1, :] if blocks_on_rows else score[:, i:i + 1]
            rank = rank + jnp.where(j > i, jnp.where(si >= score, 1.0, 0.0), jnp.where(si > score, 1.0, 0.0))
        return rank

    if idx_ref or qpos0 >= N_SEL * SEL_LEN:
        rank = ranks(False)
        rank_ref[0, 0] = rank
    else:
        crowded = qpos0 + (pl.program_id(2) + 1) * tq > N_SEL * SEL_LEN

        @pl.when(crowded)
        def _():
            rank_t = ranks(True)
            unused = jnp.full((nbp - rank_t.shape[0], tq), float(nblk), F32)
            rank_ref[0, 0] = jnp.concatenate([rank_t, unused], axis=0).T

        @pl.when(jnp.logical_not(crowded))
        def _():
            rank_ref[0, 0] = jnp.zeros((tq, nbp), F32)

    if idx_ref:
        jf = lax.broadcasted_iota(jnp.int32, (1, nbp), 1).astype(F32)
        lane = lax.broadcasted_iota(jnp.int32, (tq, V7X_LANES), 1)
        idx = jnp.zeros((tq, V7X_LANES), F32)
        for r in range(N_SEL):
            col = jnp.sum(jnp.where(rank == float(r), jf, 0.0), axis=-1, keepdims=True)
            idx = jnp.where(lane == r, col, idx)
        idx_ref[0][0, 0] = idx.astype(jnp.int32)


def _cmp_select(q, ckv, tq, nblk, qpos0, want_idx):
    b, t, _ = q.shape
    nc = ckv.shape[1]
    nbp = -(-nblk // V7X_LANES) * V7X_LANES
    hw = GQA * HEAD_DIM
    out_specs = [pl.BlockSpec((1, tq, hw), lambda bi, k, i: (bi, i, k)),
                 pl.BlockSpec((1, 1, tq, nbp), lambda bi, k, i: (bi, k, i, 0))]
    out_shape = [jax.ShapeDtypeStruct((b, t, ATTN_WIDTH), F32),
                 jax.ShapeDtypeStruct((b, N_KV_HEADS, t, nbp), F32)]
    if want_idx:
        out_specs.append(pl.BlockSpec((1, 1, tq, V7X_LANES), lambda bi, k, i: (bi, k, i, 0)))
        out_shape.append(jax.ShapeDtypeStruct((b, N_KV_HEADS, t, V7X_LANES), jnp.int32))
    return pl.pallas_call(
        functools.partial(_cmp_select_kernel, tq=tq, nc=nc, nbp=nbp, nblk=nblk, qpos0=qpos0),
        grid=(b, N_KV_HEADS, t // tq),
        in_specs=[pl.BlockSpec((1, tq, hw), lambda bi, k, i: (bi, i, k)),
                  pl.BlockSpec((1, nc, HEAD_DIM), lambda bi, k, i: (bi, 0, k)),
                  pl.BlockSpec((1, nc, HEAD_DIM), lambda bi, k, i: (bi, 0, N_KV_HEADS + k))],
        out_specs=out_specs,
        out_shape=out_shape,
        compiler_params=_cparams(("parallel", "parallel", "parallel"), 40),
        name="cmp_select",
    )(q, ckv, ckv)


ATT_TQ = 256
SEL_TK = 512
FILL_ROWS = 256
POS_RADIX = 256
RATE_TERMS = 3
MASK_BIAS = 2.0 ** 100


def _key_features(kidx, nblk):
    lane = lax.broadcasted_iota(jnp.int32, (1, V7X_LANES), 1)
    radix_shift = POS_RADIX.bit_length() - 1
    digits = jnp.where(((lane - nblk) & 1) == 0, (kidx >> radix_shift).astype(F32),
                       (kidx & (POS_RADIX - 1)).astype(F32))
    feats = jnp.where(lane < nblk + 2 * RATE_TERMS, digits, 0.0)
    if nblk:
        feats = jnp.where(lane < nblk, jnp.where((kidx >> SEL_SHIFT) == lane, 1.0, 0.0), feats)
    return feats


def _query_features(kvh, g, rank, nblk, rows):
    lane = lax.broadcasted_iota(jnp.int32, (rows, V7X_LANES), 1)
    feats = jnp.zeros((rows, V7X_LANES), F32)
    if nblk:
        feats = jnp.where((lane < nblk) & (rank >= float(N_SEL)), -MASK_BIAS, 0.0)
    for i, (c0, c1) in enumerate(zip(_bf16_terms(_alibi_slope(0, g), RATE_TERMS),
                                     _bf16_terms(_alibi_slope(1, g), RATE_TERMS))):
        c = jnp.where(kvh == 0, c0, c1)
        feats = jnp.where(lane == nblk + 2 * i, c * POS_RADIX, feats)
        feats = jnp.where(lane == nblk + 2 * i + 1, c, feats)
    return feats


def _fill_key_value_scratch(k_ref, v_ref, kf_ref, vo_ref, nblk):
    def chunk(c, carry):
        r0 = pl.multiple_of(c * FILL_ROWS, FILL_ROWS)
        rows = pl.ds(r0, FILL_ROWS)
        kidx = r0 + lax.broadcasted_iota(jnp.int32, (FILL_ROWS, 1), 0)
        kf_ref[rows, :HEAD_DIM] = k_ref[0, rows, :].astype(BF16)
        kf_ref[rows, HEAD_DIM:] = _key_features(kidx, nblk).astype(BF16)
        vo_ref[rows, :HEAD_DIM] = v_ref[0, rows, :].astype(BF16)
        vo_ref[rows, HEAD_DIM:] = jnp.ones((FILL_ROWS, V7X_LANES), BF16)
        return carry

    lax.fori_loop(0, k_ref.shape[1] // FILL_ROWS, chunk, 0)


def _attn_out(acc):
    return acc[:, :HEAD_DIM] / jnp.maximum(acc[:, HEAD_DIM:HEAD_DIM + 1], F32_TINY)


def _sel_prompt_kernel(q_ref, k_ref, v_ref, rank_ref, o_ref, kf_ref, vo_ref, qf_ref, m_ref, acc_ref, *, nblk):
    kvh, i = pl.program_id(1), pl.program_id(2)

    @pl.when(i == 0)
    def _():
        _fill_key_value_scratch(k_ref, v_ref, kf_ref, vo_ref, nblk)

    q0 = i * ATT_TQ
    qpos = q0 + (lax.broadcasted_iota(jnp.int32, (GQA * ATT_TQ, 1), 0) & (ATT_TQ - 1))
    rank = rank_ref[0, 0]
    for g in range(GQA):
        rows = slice(g * ATT_TQ, (g + 1) * ATT_TQ)
        qf_ref[rows, :HEAD_DIM] = q_ref[0, :, g * HEAD_DIM:(g + 1) * HEAD_DIM]
        qf_ref[rows, HEAD_DIM:] = _query_features(kvh, g, rank, nblk, ATT_TQ).astype(BF16)
    m_ref[...] = jnp.full(m_ref.shape, NEG_INF, F32)
    acc_ref[...] = jnp.zeros(acc_ref.shape, F32)

    def tile(jt, carry):
        k0 = pl.multiple_of(jt * SEL_TK, SEL_TK)
        causal = (k0 + lax.broadcasted_iota(jnp.int32, (1, SEL_TK), 1)) <= qpos
        s = jnp.where(causal, _bdot_nt(qf_ref[...], kf_ref[pl.ds(k0, SEL_TK), :]), NEG_INF)
        _flash_step(m_ref, acc_ref, 0, s, vo_ref[pl.ds(k0, SEL_TK), :])
        return carry

    lax.fori_loop(0, (q0 + ATT_TQ + SEL_TK - 1) // SEL_TK, tile, 0)
    out = _attn_out(acc_ref[0])
    for g in range(GQA):
        o_ref[0, :, g * HEAD_DIM:(g + 1) * HEAD_DIM] = out[g * ATT_TQ:(g + 1) * ATT_TQ]


def _sel_prompt(q, kv_slc, rank, nblk):
    b, t, _ = q.shape
    nbp = rank.shape[-1]
    hw = GQA * HEAD_DIM
    assert nbp == V7X_LANES and nblk + 2 * RATE_TERMS <= V7X_LANES and t <= POS_RADIX * POS_RADIX
    return pl.pallas_call(
        functools.partial(_sel_prompt_kernel, nblk=nblk),
        grid=(b, N_KV_HEADS, t // ATT_TQ),
        in_specs=[pl.BlockSpec((1, ATT_TQ, hw), lambda bi, k, i: (bi, i, k)),
                  pl.BlockSpec((1, t, HEAD_DIM), lambda bi, k, i: (bi, 0, 2 * k)),
                  pl.BlockSpec((1, t, HEAD_DIM), lambda bi, k, i: (bi, 0, 2 * k + 1)),
                  pl.BlockSpec((1, 1, ATT_TQ, nbp), lambda bi, k, i: (bi, k, i, 0))],
        out_specs=pl.BlockSpec((1, ATT_TQ, hw), lambda bi, k, i: (bi, i, k)),
        out_shape=jax.ShapeDtypeStruct((b, t, ATTN_WIDTH), F32),
        scratch_shapes=[pltpu.VMEM((t, 2 * HEAD_DIM), BF16), pltpu.VMEM((t, 2 * HEAD_DIM), BF16),
                        pltpu.VMEM((GQA * ATT_TQ, 2 * HEAD_DIM), BF16),
                        pltpu.VMEM((1, GQA * ATT_TQ, 1), F32), pltpu.VMEM((1, GQA * ATT_TQ, 2 * HEAD_DIM), F32)],
        compiler_params=_cparams(("parallel", "parallel", "arbitrary"), 32),
        name="sel_prompt",
    )(q, kv_slc, kv_slc, rank)


def _win_prompt_kernel(q_ref, k_ref, v_ref, o_ref, kf_ref, vo_ref):
    kvh, i = pl.program_id(1), pl.program_id(2)

    @pl.when(i == 0)
    def _():
        _fill_key_value_scratch(k_ref, v_ref, kf_ref, vo_ref, 0)

    span = WINDOW + WIN_TQ
    qpos = i * WIN_TQ + (lax.broadcasted_iota(jnp.int32, (GQA * WIN_TQ, 1), 0) & (WIN_TQ - 1))
    k0 = pl.multiple_of(jnp.maximum(i * WIN_TQ - WINDOW, 0), WIN_TQ)
    d = qpos - (k0 + lax.broadcasted_iota(jnp.int32, (1, span), 1))
    qf = jnp.concatenate(
        [jnp.concatenate([q_ref[0, :, g * HEAD_DIM:(g + 1) * HEAD_DIM],
                          _query_features(kvh, g, None, 0, WIN_TQ).astype(BF16)], axis=1) for g in range(GQA)],
        axis=0)
    s = jnp.where((d >= 0) & (d < WINDOW), _bdot_nt(qf, kf_ref[pl.ds(k0, span), :]), NEG_INF)
    e = jnp.exp2(s - jnp.max(s, axis=-1, keepdims=True))
    out = _attn_out(_bdot(e.astype(BF16), vo_ref[pl.ds(k0, span), :]))
    for g in range(GQA):
        o_ref[0, :, g * HEAD_DIM:(g + 1) * HEAD_DIM] = out[g * WIN_TQ:(g + 1) * WIN_TQ]


def _win_prompt(q, kv_win):
    b, t, _ = q.shape
    hw = GQA * HEAD_DIM
    assert t <= POS_RADIX * POS_RADIX
    return pl.pallas_call(
        _win_prompt_kernel,
        grid=(b, N_KV_HEADS, t // WIN_TQ),
        in_specs=[pl.BlockSpec((1, WIN_TQ, hw), lambda bi, k, i: (bi, i, k)),
                  pl.BlockSpec((1, t, HEAD_DIM), lambda bi, k, i: (bi, 0, 2 * k)),
                  pl.BlockSpec((1, t, HEAD_DIM), lambda bi, k, i: (bi, 0, 2 * k + 1))],
        out_specs=pl.BlockSpec((1, WIN_TQ, hw), lambda bi, k, i: (bi, i, k)),
        out_shape=jax.ShapeDtypeStruct((b, t, ATTN_WIDTH), F32),
        scratch_shapes=[pltpu.VMEM((t, 2 * HEAD_DIM), BF16), pltpu.VMEM((t, 2 * HEAD_DIM), BF16)],
        compiler_params=_cparams(("parallel", "parallel", "arbitrary"), 32),
        name="win_prompt",
    )(q, kv_win, kv_win)


def _slope_col(kvh):
    g = lax.broadcasted_iota(jnp.int32, (GQA, 1), 0)
    col = jnp.zeros((GQA, 1), F32)
    for i in range(GQA):
        col = jnp.where(g == i, _alibi_slope(kvh, i), col)
    return col


def _stream_rows(ref, lead, stream, n):
    return ref[lead + (pl.ds(stream, n, stride=2 * N_KV_HEADS), slice(None))]


def _sel_sample_kernel(idx_ref, pt_ref, q_ref, *refs, t_len):
    past, new_ref, o_ref = refs[:N_KV_HEADS * N_SEL], refs[N_KV_HEADS * N_SEL], refs[N_KV_HEADS * N_SEL + 1]
    b, t = pl.program_id(0), pl.program_id(1)
    past_blocks = PAST_LEN // SEL_LEN
    lane = lax.broadcasted_iota(jnp.int32, (1, N_SEL * SEL_LEN), 1)
    for kvh in range(N_KV_HEADS):
        ks, vs = [], []
        base = jnp.zeros((1, N_SEL * SEL_LEN), jnp.int32)
        for r in range(N_SEL):
            blk = idx_ref[((b * N_KV_HEADS + kvh) * t_len + t) * N_SEL + r]
            is_new = blk >= past_blocks
            ref = past[kvh * N_SEL + r]
            ks.append(jnp.where(is_new, _stream_rows(new_ref, (0,), 2 * kvh, SEL_LEN),
                                _stream_rows(ref, (), 2 * kvh, SEL_LEN)).astype(BF16))
            vs.append(jnp.where(is_new, _stream_rows(new_ref, (0,), 2 * kvh + 1, SEL_LEN),
                                _stream_rows(ref, (), 2 * kvh + 1, SEL_LEN)).astype(BF16))
            base = jnp.where((lane >> SEL_SHIFT) == r, blk * SEL_LEN, base)
        d = (PAST_LEN + t) - (base + (lane & (SEL_LEN - 1)))
        rows = slice(kvh * GQA, (kvh + 1) * GQA)
        s = _bdot_nt(q_ref[0, 0, rows, :], jnp.concatenate(ks, axis=0)) - _slope_col(kvh) * d.astype(F32)
        _, e, z = _softmax_parts(s, d >= 0)
        o_ref[0, 0, rows, :] = _bdot(e.astype(BF16), jnp.concatenate(vs, axis=0)) / jnp.maximum(z, F32_TINY)


def _sel_sample(q_h, cache_slc, layer, page_table, idx, kv_new_rows):
    bsz, t_len = q_h.shape[:2]
    n_pages = page_table.shape[1]
    per_page = PAGE_SIZE // SEL_LEN
    past_blocks = PAST_LEN // SEL_LEN
    pages = _rowwise_pages(cache_slc)
    blk_rows = SEL_LEN * 2 * N_KV_HEADS

    def past_spec(kvh, r):
        def imap(b, t, idx_ref, pt_ref):
            blk = jnp.minimum(idx_ref[((b * N_KV_HEADS + kvh) * t_len + t) * N_SEL + r], past_blocks - 1)
            return (layer, pt_ref[b * n_pages + blk // per_page], blk % per_page, 0)
        return pl.BlockSpec((None, None, blk_rows, HEAD_DIM), imap)

    qspec = pl.BlockSpec((1, 1, N_HEADS, HEAD_DIM), lambda b, t, i_, p_: (b, t, 0, 0))
    return pl.pallas_call(
        functools.partial(_sel_sample_kernel, t_len=t_len),
        grid_spec=pltpu.PrefetchScalarGridSpec(
            num_scalar_prefetch=2,
            grid=(bsz, t_len),
            in_specs=[qspec] + [past_spec(kvh, r) for kvh in range(N_KV_HEADS) for r in range(N_SEL)]
            + [pl.BlockSpec((1, blk_rows, HEAD_DIM), lambda b, t, i_, p_: (b, 0, 0))],
            out_specs=qspec,
        ),
        out_shape=jax.ShapeDtypeStruct(q_h.shape, F32),
        compiler_params=_cparams(("parallel", "parallel"), 32),
        name="sel_sample",
    )(idx.reshape(-1), page_table.reshape(-1), q_h, *([pages] * (N_KV_HEADS * N_SEL)), kv_new_rows)


def _win_sample_kernel(q_ref, past_ref, new_ref, o_ref, *, t_len, t_pad, wbuf):
    t = lax.broadcasted_iota(jnp.int32, (t_len, 1), 0)
    d_past = wbuf + t - lax.broadcasted_iota(jnp.int32, (1, wbuf), 1)
    row_new = lax.broadcasted_iota(jnp.int32, (1, t_pad), 1)
    d_new = t - row_new
    valid_past = (d_past >= 0) & (d_past < WINDOW)
    valid_new = (d_new >= 0) & (d_new < WINDOW) & (row_new < t_len)
    for kvh in range(N_KV_HEADS):
        kp = _stream_rows(past_ref, (0,), 2 * kvh, wbuf).astype(BF16)
        vp = _stream_rows(past_ref, (0,), 2 * kvh + 1, wbuf).astype(BF16)
        kn = _stream_rows(new_ref, (0,), 2 * kvh, t_pad).astype(BF16)
        vn = _stream_rows(new_ref, (0,), 2 * kvh + 1, t_pad).astype(BF16)
        for g in range(GQA):
            q = q_ref[0, kvh, g]
            slope = _alibi_slope(kvh, g)
            m1, e1, z1 = _softmax_parts(_bdot_nt(q, kp) - slope * d_past.astype(F32), valid_past)
            m2, e2, z2 = _softmax_parts(_bdot_nt(q, kn) - slope * d_new.astype(F32), valid_new)
            m = jnp.maximum(m1, m2)
            a1 = jnp.where(z1 > 0.0, jnp.exp2(m1 - m), 0.0)
            a2 = jnp.where(z2 > 0.0, jnp.exp2(m2 - m), 0.0)
            num = a1 * _bdot(e1.astype(BF16), vp) + a2 * _bdot(e2.astype(BF16), vn)
            o_ref[0, kvh, g] = num / jnp.maximum(a1 * z1 + a2 * z2, F32_TINY)


def _win_sample(q_g, win_rows, new_rows):
    bsz, _, _, t_len, _ = q_g.shape
    streams = 2 * N_KV_HEADS
    wbuf, t_pad = win_rows.shape[1] // streams, new_rows.shape[1] // streams
    qspec = pl.BlockSpec((1, N_KV_HEADS, GQA, t_len, HEAD_DIM), lambda b: (b, 0, 0, 0, 0))
    return pl.pallas_call(
        functools.partial(_win_sample_kernel, t_len=t_len, t_pad=t_pad, wbuf=wbuf),
        grid=(bsz,),
        in_specs=[qspec, pl.BlockSpec((1, wbuf * streams, HEAD_DIM), lambda b: (b, 0, 0)),
                  pl.BlockSpec((1, t_pad * streams, HEAD_DIM), lambda b: (b, 0, 0))],
        out_specs=qspec,
        out_shape=jax.ShapeDtypeStruct(q_g.shape, F32),
        compiler_params=_cparams(("parallel",), 16),
        name="win_sample",
    )(q_g, win_rows, new_rows)


def _mix_kernel(x_ref, y_ref, oc_ref, os_ref, ow_ref, gate_ref, wglu_ref, gs_ref, ga_ref, wout_ref, o_ref):
    z = _gelu(y_ref[...])
    z = z * jax.nn.sigmoid(_bdot(z.astype(BF16), wglu_ref[...]))
    gate = gate_ref[...]
    cols = []
    for h in range(N_HEADS):
        sl = slice(h * HEAD_DIM, (h + 1) * HEAD_DIM)
        cols.append(gate[:, 3 * h:3 * h + 1] * oc_ref[:, sl] + gate[:, 3 * h + 1:3 * h + 2] * os_ref[:, sl]
                    + gate[:, 3 * h + 2:3 * h + 3] * ow_ref[:, sl])
    attn = jnp.concatenate(cols, axis=1)
    hs = _rms(z, gs_ref[...]).astype(BF16)
    ha = _rms(attn, ga_ref[...]).astype(BF16)
    o_ref[...] = x_ref[...] + (_bdot(hs, wout_ref[:SSM_WIDTH, :]) + _bdot(ha, wout_ref[SSM_WIDTH:, :]))


def _mix(x, y, oc, osl, ow, gates, prm, tm):
    m = x.shape[0]
    row = lambda n: pl.BlockSpec((tm, n), lambda i: (i, 0))
    return pl.pallas_call(
        _mix_kernel,
        grid=(m // tm,),
        in_specs=[row(D_MODEL), row(SSM_WIDTH), row(ATTN_WIDTH), row(ATTN_WIDTH), row(ATTN_WIDTH), row(N_GATES),
                  _resident_layer((SSM_WIDTH, SSM_WIDTH), prm["layer"]), _resident((1, SSM_WIDTH)),
                  _resident((1, ATTN_WIDTH)), _resident_layer((D_MODEL, D_MODEL), prm["layer"])],
        out_specs=row(D_MODEL),
        out_shape=jax.ShapeDtypeStruct((m, D_MODEL), F32),
        compiler_params=_cparams(("parallel",), 48),
        name="mix_outproj",
    )(x, y, oc, osl, ow, gates, prm["ssm_w_glu"], prm["norm_ssm_out"], prm["norm_attn_out"], prm["w_out"])


FFN_TF = 512


def _ffn_kernel(x_ref, g_ref, wg_ref, wu_ref, wd_ref, o_ref, xn_ref, acc_ref):
    j = pl.program_id(1)

    @pl.when(j == 0)
    def _():
        xn_ref[...] = _rms(x_ref[...], g_ref[...]).astype(BF16)
        acc_ref[...] = jnp.zeros(acc_ref.shape, F32)

    xn = xn_ref[...]
    a = _bdot(xn, wg_ref[...])
    h = (a * jax.nn.sigmoid(a)) * _bdot(xn, wu_ref[...])
    acc_ref[...] += _bdot(h.astype(BF16), wd_ref[...])

    @pl.when(j == pl.num_programs(1) - 1)
    def _():
        o_ref[...] = x_ref[...] + acc_ref[...]


def _ffn(x, prm, tm):
    m = x.shape[0]
    layer = prm["layer"]
    return pl.pallas_call(
        _ffn_kernel,
        grid=(m // tm, D_FF // FFN_TF),
        in_specs=[pl.BlockSpec((tm, D_MODEL), lambda i, j: (i, 0)),
                  pl.BlockSpec((1, D_MODEL), lambda i, j: (0, 0)),
                  pl.BlockSpec((None, D_MODEL, FFN_TF), lambda i, j: (layer, 0, j)),
                  pl.BlockSpec((None, D_MODEL, FFN_TF), lambda i, j: (layer, 0, j)),
                  pl.BlockSpec((None, FFN_TF, D_MODEL), lambda i, j: (layer, j, 0))],
        out_specs=pl.BlockSpec((tm, D_MODEL), lambda i, j: (i, 0)),
        out_shape=jax.ShapeDtypeStruct((m, D_MODEL), F32),
        scratch_shapes=[pltpu.VMEM((tm, D_MODEL), BF16), pltpu.VMEM((tm, D_MODEL), F32)],
        compiler_params=_cparams(("parallel", "arbitrary"), 48),
        name="ffn",
    )(x, prm["norm_ffn"], prm["w_ffn_gate"], prm["w_ffn_up"], prm["w_ffn_down"])


S5_CHUNK = 64
S5_GROUPS_PER_STEP = 2
SAMPLE_S5_PAD = 8


BIG_WEIGHTS = ("w_in", "ssm_w_glu", "w_out", "w_ffn_gate", "w_ffn_up", "w_ffn_down")


def _layer_params(l, w, big):
    def w1cat(w1):
        half = CMP_STRIDE * HEAD_DIM
        return jnp.concatenate([w1[:half], w1[half:]], axis=1).astype(BF16)

    row = lambda v: v.reshape(1, -1)
    a_re, a_im = w["ssm_a_re"][l], w["ssm_a_im"][l]
    log_dt = jnp.broadcast_to(w["ssm_log_dt"][l][:, None], a_re.shape)
    lrow = jnp.stack([a_re, a_im, log_dt], axis=1)
    d = w["ssm_d"][l]
    return {
        "layer": l, **big, "norm_mix": row(w["norm_mix"][l]),
        "q_norm": row(w["q_norm"][l]), "k_norm_slc": row(w["k_norm_slc"][l]),
        "k_norm_win": row(w["k_norm_win"][l]), "k_norm_cmp": row(w["k_norm_cmp"][l]),
        "lrow": lrow, "lcol": lrow.transpose(0, 2, 1),
        "bt": jnp.stack([w["ssm_b_re"][l], w["ssm_b_im"][l]], axis=1).transpose(0, 1, 3, 2),
        "bn": jnp.stack([w["ssm_b_re"][l], w["ssm_b_im"][l]], axis=1),
        "ct": jnp.stack([w["ssm_c_re"][l], w["ssm_c_im"][l]], axis=1).transpose(0, 1, 3, 2),
        "dtile": {lp: jnp.tile(d, (1, lp))[:, None, :] for lp in (S5_CHUNK, SAMPLE_S5_PAD)},
        "cmp_w1_k": w1cat(w["cmp_w1_k"][l]), "cmp_w1_v": w1cat(w["cmp_w1_v"][l]),
        "cmp_b1": jnp.stack([w["cmp_b1_k"][l], w["cmp_b1_v"][l]]),
        "cmp_w2_k": w["cmp_w2_k"][l].astype(BF16), "cmp_w2_v": w["cmp_w2_v"][l].astype(BF16),
        "norm_ssm_out": row(w["norm_ssm_out"][l]), "norm_attn_out": row(w["norm_attn_out"][l]),
        "norm_ffn": row(w["norm_ffn"][l]),
    }


def _kv_out(kv, b, t):
    return kv.reshape(b, t, N_KV_HEADS, 2, HEAD_DIM)


PROMPT_TM = 256
MIX_TM = 256
FFN_TM = 512


def _prompt_layer(x, prm):
    b, t, _ = x.shape
    m = b * t
    x2 = x.reshape(m, D_MODEL)
    u, q, kv_cmp, kv_slc, kv_win, gates = _inproj(x2, prm, PROMPT_TM)
    h0 = jnp.zeros((b, SSM_GROUPS, SSM_STATE, 2), F32)
    y, h_last = _s5_mixer(u.reshape(b, t, SSM_WIDTH), h0, prm, S5_CHUNK, S5_CHUNK)
    q3 = q.reshape(b, t, ATTN_WIDTH)
    kv_cmp3, kv_slc3, kv_win3 = (a.reshape(b, t, 2 * KV_WIDTH) for a in (kv_cmp, kv_slc, kv_win))
    ckv = _compress_prompt(kv_cmp3, prm)
    o_cmp, rank = _cmp_select(q3, ckv, 256, t // SEL_LEN, 0, False)
    o_slc = _sel_prompt(q3, kv_slc3, rank, t // SEL_LEN)
    o_win = _win_prompt(q3, kv_win3)
    x2 = _mix(x2, y.reshape(m, SSM_WIDTH), o_cmp.reshape(m, -1), o_slc.reshape(m, -1), o_win.reshape(m, -1),
              gates, prm, MIX_TM)
    x2 = _ffn(x2, prm, FFN_TM)
    wbuf = min(WINDOW, PAST_LEN)
    win_state = kv_win3[:, t - wbuf:] if t >= wbuf else jnp.pad(kv_win3, ((0, 0), (wbuf - t, 0), (0, 0)))
    return (x2.reshape(b, t, D_MODEL), _kv_out(kv_cmp3, b, t), _kv_out(kv_slc3, b, t),
            _kv_out(win_state, b, wbuf), h_last)


def _sample_layer(x, prm, cache_cmp, cache_slc, layer, page_table, win_buf, h0):
    b, t, _ = x.shape
    m = b * t
    assert (PAST_LEN + t) // CMP_STRIDE == PAST_LEN // CMP_STRIDE and PAST_LEN % PAGE_SIZE == 0
    x2 = x.reshape(m, D_MODEL)
    u, q, kv_cmp, kv_slc, kv_win, gates = _inproj(x2, prm, m)
    y, h_last = _s5_mixer(u.reshape(b, t, SSM_WIDTH), h0, prm, SAMPLE_S5_PAD, t)
    q3 = q.reshape(b, t, ATTN_WIDTH)
    kv_cmp3, kv_slc3, kv_win3 = (a.reshape(b, t, 2 * KV_WIDTH) for a in (kv_cmp, kv_slc, kv_win))
    ckv = _compress_sample(cache_cmp, layer, page_table, prm)
    n_blocks = -(-(PAST_LEN + t) // SEL_LEN)
    o_cmp, _, idx = _cmp_select(q3, ckv, t, n_blocks, PAST_LEN, True)
    streams = 2 * N_KV_HEADS

    def stream_rows(kv3, t_pad):
        return jnp.pad(kv3.reshape(b, t * streams, HEAD_DIM), ((0, 0), (0, (t_pad - t) * streams), (0, 0)))

    o_slc = _sel_sample(q3.reshape(b, t, N_HEADS, HEAD_DIM), cache_slc, layer, page_table, idx[..., :N_SEL],
                        stream_rows(kv_slc3, SEL_LEN)).reshape(m, ATTN_WIDTH)
    wbuf = win_buf.shape[1]
    q5 = q3.reshape(b, t, N_KV_HEADS, GQA, HEAD_DIM)
    o_win = _win_sample(q5.transpose(0, 2, 3, 1, 4), win_buf.reshape(b, wbuf * streams, HEAD_DIM),
                        stream_rows(kv_win3, 8))
    o_win = o_win.transpose(0, 3, 1, 2, 4).reshape(m, ATTN_WIDTH)
    x2 = _mix(x2, y.reshape(m, SSM_WIDTH), o_cmp.reshape(m, -1), o_slc, o_win, gates, prm, m)
    x2 = _ffn(x2, prm, m)
    win_state = jnp.concatenate([win_buf, _kv_out(kv_win3, b, t)], axis=1)[:, -wbuf:]
    return (x2.reshape(b, t, D_MODEL), _kv_out(kv_cmp3, b, t), _kv_out(kv_slc3, b, t), win_state, h_last)


def kernel(x_prompt, x_sample, cache_cmp_kv, cache_slc_kv, state_win_kv, state_ssm, page_table, norm_mix, w_in, ssm_a_re, ssm_a_im, ssm_log_dt, ssm_b_re, ssm_b_im, ssm_c_re, ssm_c_im, ssm_d, ssm_w_glu, q_norm, k_norm_cmp, k_norm_slc, k_norm_win, cmp_w1_k, cmp_b1_k, cmp_w2_k, cmp_w1_v, cmp_b1_v, cmp_w2_v, norm_ssm_out, norm_attn_out, w_out, norm_ffn, w_ffn_gate, w_ffn_up, w_ffn_down):
    w = dict(norm_mix=norm_mix, w_in=w_in, ssm_a_re=ssm_a_re, ssm_a_im=ssm_a_im, ssm_log_dt=ssm_log_dt,
             ssm_b_re=ssm_b_re, ssm_b_im=ssm_b_im, ssm_c_re=ssm_c_re, ssm_c_im=ssm_c_im, ssm_d=ssm_d,
             ssm_w_glu=ssm_w_glu, q_norm=q_norm, k_norm_cmp=k_norm_cmp, k_norm_slc=k_norm_slc,
             k_norm_win=k_norm_win, cmp_w1_k=cmp_w1_k, cmp_b1_k=cmp_b1_k, cmp_w2_k=cmp_w2_k,
             cmp_w1_v=cmp_w1_v, cmp_b1_v=cmp_b1_v, cmp_w2_v=cmp_w2_v, norm_ssm_out=norm_ssm_out,
             norm_attn_out=norm_attn_out, w_out=w_out, norm_ffn=norm_ffn, w_ffn_gate=w_ffn_gate,
             w_ffn_up=w_ffn_up, w_ffn_down=w_ffn_down)
    big = {name: w[name].astype(BF16) for name in BIG_WEIGHTS}
    y_p, y_s = x_prompt, x_sample
    outs_p, outs_s = [], []
    for l in range(DEPTH):
        prm = _layer_params(l, w, big)
        y_p, *rest = _prompt_layer(y_p, prm)
        outs_p.append(rest)
        y_s, *rest = _sample_layer(y_s, prm, cache_cmp_kv, cache_slc_kv, l, page_table,
                                   state_win_kv[l], state_ssm[l])
        outs_s.append(rest)
    stack = lambda outs, i: jnp.stack([o[i] for o in outs])
    return (y_p, y_s, stack(outs_p, 0), stack(outs_p, 1), stack(outs_p, 2), stack(outs_p, 3),
            stack(outs_s, 0), stack(outs_s, 1), stack(outs_s, 2), stack(outs_s, 3))
```

```python
import functools
import math

import jax
import jax.numpy as jnp
import numpy as np
from jax import lax
from jax.experimental import pallas as pl
from jax.experimental.pallas import tpu as pltpu

F32 = jnp.float32
BF16 = jnp.bfloat16

D_MODEL = 2048
DEPTH = 2
PAST_LEN = 16384
PAGE_SIZE = 128
SSM_WIDTH = 1024
ATTN_WIDTH = 1024
SSM_CH = 16
SSM_GROUPS = 64
SSM_STATE = 64
HEAD_DIM = 128
N_HEADS = 8
N_KV_HEADS = 2
GQA = 4
KV_WIDTH = 256
CMP_LEN = 32
CMP_STRIDE = 16
SEL_LEN = 64
SEL_SHIFT = SEL_LEN.bit_length() - 1
N_SEL = 16
WINDOW = 512
WIN_TQ = 256
FORCED_SCORE = 1e4
D_FF = 5632
NORM_EPS = 1e-6
LOG2E = math.log2(math.e)
QSCALE = HEAD_DIM ** -0.5 * LOG2E
N_GATES = N_HEADS * 3
IN_WIDTH = SSM_WIDTH + ATTN_WIDTH + 6 * KV_WIDTH + N_GATES
F32_TINY = float(jnp.finfo(jnp.float32).tiny)
NEG_INF = float("-inf")

V7X_VMEM_BYTES = 64 * 1024 * 1024
V7X_LANES = 128


def _cparams(semantics, vmem_mib):
    assert vmem_mib * 1024 * 1024 < V7X_VMEM_BYTES
    return pltpu.CompilerParams(dimension_semantics=semantics, vmem_limit_bytes=vmem_mib * 1024 * 1024)


def _resident(shape):
    nd = len(shape)
    return pl.BlockSpec(shape, lambda *_: (0,) * nd, pipeline_mode=pl.Buffered(1))


def _resident_layer(shape, layer):
    nd = len(shape)
    return pl.BlockSpec((None,) + tuple(shape), lambda *_: (layer,) + (0,) * nd, pipeline_mode=pl.Buffered(1))


def _rms(x, g):
    return x * lax.rsqrt(jnp.mean(x * x, axis=-1, keepdims=True) + NORM_EPS) * g


def _gelu(x):
    c = math.sqrt(2.0 / math.pi)
    return x * (0.5 * (1.0 + jnp.tanh(c * (x + 0.044715 * (x * x * x)))))


def _bdot(a, b):
    return jnp.dot(a, b, preferred_element_type=F32)


def _bdot_nt(a, b):
    return lax.dot_general(a, b, (((1,), (1,)), ((), ())), preferred_element_type=F32)


def _split3(x):
    hi = x.astype(BF16)
    r1 = x - hi.astype(F32)
    mid = r1.astype(BF16)
    lo = (r1 - mid.astype(F32)).astype(BF16)
    return hi, mid, lo


def _dot_small_int(x, e):
    hi, mid, lo = _split3(x)
    return _bdot(hi, e) + _bdot(mid, e) + _bdot(lo, e)


def _dot_f32(a, b):
    ah, am, al = _split3(a)
    bh, bm, bl = _split3(b)
    return (_bdot(ah, bh) + (_bdot(ah, bm) + _bdot(am, bh))
            + (_bdot(ah, bl) + _bdot(al, bh) + _bdot(am, bm)))


def _softmax_parts(s, valid):
    s = jnp.where(valid, s, NEG_INF)
    m = jnp.max(s, axis=-1, keepdims=True)
    m = jnp.where(m == NEG_INF, 0.0, m)
    e = jnp.exp2(s - m)
    return m, e, jnp.sum(e, axis=-1, keepdims=True)


def _flash_step(m_ref, acc_ref, g, s, v_ones):
    m_old = m_ref[g]
    m_new = jnp.maximum(m_old, jnp.max(s, axis=-1, keepdims=True))
    m_safe = jnp.where(m_new == NEG_INF, 0.0, m_new)
    acc_ref[g] = jnp.exp2(m_old - m_safe) * acc_ref[g] + _bdot(jnp.exp2(s - m_safe).astype(BF16), v_ones)
    m_ref[g] = m_new


def _alibi_slope(kvh, g):
    return LOG2E * 2.0 ** (-8.0 * (kvh * GQA + g + 1) / N_HEADS)


def _bf16_terms(x, n=3):
    out = []
    for _ in range(n):
        hi = float(np.asarray(x, dtype=np.float32).astype(BF16))
        out.append(hi)
        x -= hi
    return out


def _head_slope(kvh, g):
    assert N_KV_HEADS == 2
    return jnp.where(kvh == 0, _alibi_slope(0, g), _alibi_slope(1, g))


def _inproj_kernel(x_ref, gmix_ref, w_ref, qn_ref, ksn_ref, kwn_ref,
                   u_ref, q_ref, cmp_ref, slc_ref, win_ref, gate_ref):
    xn = _rms(x_ref[...], gmix_ref[...]).astype(BF16)

    def proj(c0, c1):
        return _bdot(xn, w_ref[:, c0:c1])

    u_ref[...] = proj(0, SSM_WIDTH)
    q = proj(SSM_WIDTH, D_MODEL)
    for h in range(N_HEADS):
        sl = slice(h * HEAD_DIM, (h + 1) * HEAD_DIM)
        q_ref[:, sl] = (_rms(q[:, sl], qn_ref[...]) * QSCALE).astype(BF16)
    c0 = D_MODEL
    for ref, nref in ((cmp_ref, None), (slc_ref, ksn_ref), (win_ref, kwn_ref)):
        k = proj(c0, c0 + KV_WIDTH)
        v = proj(c0 + KV_WIDTH, c0 + 2 * KV_WIDTH)
        c0 += 2 * KV_WIDTH
        for kvh in range(N_KV_HEADS):
            src = slice(kvh * HEAD_DIM, (kvh + 1) * HEAD_DIM)
            dst = kvh * 2 * HEAD_DIM
            ref[:, dst:dst + HEAD_DIM] = k[:, src] if nref is None else _rms(k[:, src], nref[...])
            ref[:, dst + HEAD_DIM:dst + 2 * HEAD_DIM] = v[:, src]
    gate_ref[...] = jax.nn.sigmoid(proj(c0, c0 + N_GATES))


def _inproj(x, prm, tm):
    m = x.shape[0]
    row = lambda n: pl.BlockSpec((tm, n), lambda i: (i, 0))
    kvw = 2 * KV_WIDTH
    return pl.pallas_call(
        _inproj_kernel,
        grid=(m // tm,),
        in_specs=[row(D_MODEL), _resident((1, D_MODEL)), _resident_layer((D_MODEL, IN_WIDTH), prm["layer"]),
                  _resident((1, HEAD_DIM)), _resident((1, HEAD_DIM)), _resident((1, HEAD_DIM))],
        out_specs=[row(SSM_WIDTH), row(ATTN_WIDTH), row(kvw), row(kvw), row(kvw), row(N_GATES)],
        out_shape=[jax.ShapeDtypeStruct((m, SSM_WIDTH), F32), jax.ShapeDtypeStruct((m, ATTN_WIDTH), BF16),
                   jax.ShapeDtypeStruct((m, kvw), F32), jax.ShapeDtypeStruct((m, kvw), F32),
                   jax.ShapeDtypeStruct((m, kvw), F32), jax.ShapeDtypeStruct((m, N_GATES), F32)],
        compiler_params=_cparams(("parallel",), 48),
        name="inproj",
    )(x, prm["norm_mix"], prm["w_in"], prm["q_norm"], prm["k_norm_slc"], prm["k_norm_win"])


def _s5_group_body(gi, u_ref, h0_ref, lrow_ref, lcol_ref, bt_ref, bn_ref, ct_ref, d_ref,
                   y_ref, hl_ref, tm_ref, x_ref, hin_ref, *, lp, lreal, nb, nc):
    k = lp * SSM_CH
    n = SSM_STATE

    def lam_bar(a_re, a_im, log_dt):
        dt = jnp.exp(log_dt)
        e = jnp.exp(a_re * dt)
        return e * jnp.cos(a_im * dt), e * jnp.sin(a_im * dt)

    def zoh_coef(l_re, l_im, a_re, a_im):
        den = a_re * a_re + a_im * a_im
        x_re = l_re - 1.0
        return (x_re * a_re + l_im * a_im) / den, (l_im * a_re - x_re * a_im) / den

    lc = lcol_ref[gi]
    a_re_c, a_im_c = lc[:, 0:1], lc[:, 1:2]
    l_re_c, l_im_c = lam_bar(a_re_c, a_im_c, lc[:, 2:3])
    lr = lrow_ref[gi]
    a_re_r, a_im_r = lr[0:1], lr[1:2]
    l_re_r, l_im_r = lam_bar(a_re_r, a_im_r, lr[2:3])

    def pow_table(e):
        p_re = jnp.ones((n, k), F32)
        p_im = jnp.zeros((n, k), F32)
        s_re, s_im = l_re_c, l_im_c
        for j in range(max(lp.bit_length() - 1, 1)):
            bit = ((e >> j) & 1) == 1
            m_re = jnp.where(bit, s_re, 1.0)
            m_im = jnp.where(bit, s_im, 0.0)
            p_re, p_im = p_re * m_re - p_im * m_im, p_re * m_im + p_im * m_re
            s_re, s_im = s_re * s_re - s_im * s_im, 2.0 * s_re * s_im
        return p_re, p_im

    lane = lax.broadcasted_iota(jnp.int32, (1, k), 1)
    tau = lane >> (SSM_CH.bit_length() - 1)
    ch_onehot = jnp.where((lax.broadcasted_iota(jnp.int32, (SSM_CH, k), 1) & (SSM_CH - 1))
                          == lax.broadcasted_iota(jnp.int32, (SSM_CH, k), 0), 1.0, 0.0).astype(BF16)

    p_re, p_im = pow_table(tau)
    c_re = _dot_small_int(ct_ref[gi, 0], ch_onehot)
    c_im = _dot_small_int(ct_ref[gi, 1], ch_onehot)
    g_re = c_re * p_re - c_im * p_im
    g_im = c_re * p_im + c_im * p_re

    cf_re_r, cf_im_r = zoh_coef(l_re_r, l_im_r, a_re_r, a_im_r)
    bbt_re = cf_re_r * bt_ref[gi, 0] - cf_im_r * bt_ref[gi, 1]
    bbt_im = cf_re_r * bt_ref[gi, 1] + cf_im_r * bt_ref[gi, 0]
    a = _dot_f32(bbt_re, g_re) - _dot_f32(bbt_im, g_im)

    lane16 = lax.broadcasted_iota(jnp.int32, (SSM_CH, k), 1)
    for s in range(lp):
        blk = a if s == 0 else jnp.where(lane16 >= SSM_CH * s, pltpu.roll(a, SSM_CH * s, axis=1), 0.0)
        tm_ref[gi, SSM_CH * s:SSM_CH * (s + 1), :] = blk.astype(BF16)

    cf_re_c, cf_im_c = zoh_coef(l_re_c, l_im_c, a_re_c, a_im_c)
    bbn_re = cf_re_c * bn_ref[gi, 0] - cf_im_c * bn_ref[gi, 1]
    bbn_im = cf_re_c * bn_ref[gi, 1] + cf_im_c * bn_ref[gi, 0]
    bb_re = _dot_small_int(bbn_re, ch_onehot)
    bb_im = _dot_small_int(bbn_im, ch_onehot)
    rev = (lreal - 1) - tau
    q_re, q_im = pow_table(jnp.maximum(rev, 0))
    live = rev >= 0
    zeros_nk = jnp.zeros((n, k), BF16)
    wx_re = jnp.concatenate([jnp.where(live, q_re * bb_re - q_im * bb_im, 0.0).astype(BF16), zeros_nk], axis=0)
    wx_im = jnp.concatenate([jnp.where(live, q_re * bb_im + q_im * bb_re, 0.0).astype(BF16), zeros_nk], axis=0)

    u = u_ref[gi]
    ub = u.astype(BF16)
    x_ref[gi, 0] = _bdot_nt(ub, wx_re)
    x_ref[gi, 1] = _bdot_nt(ub, wx_im)

    ll_re, ll_im = l_re_r, l_im_r
    for _ in range(lreal.bit_length() - 1):
        ll_re, ll_im = ll_re * ll_re - ll_im * ll_im, 2.0 * ll_re * ll_im
    ll_re = jnp.concatenate([ll_re, jnp.zeros_like(ll_re)], axis=1)
    ll_im = jnp.concatenate([ll_im, jnp.zeros_like(ll_im)], axis=1)
    lower = lax.broadcasted_iota(jnp.int32, (1, 2 * n), 1) < n
    h0 = h0_ref[gi]
    h_re = jnp.where(lower, h0, 0.0)
    h_im = jnp.where(lower, pltpu.roll(h0, n, axis=1), 0.0)
    for c in range(nc):
        rows = pl.ds(c, nb, stride=nc) if nc > 1 else slice(0, nb)
        hin_ref[gi, 0, rows, :] = h_re
        hin_ref[gi, 1, rows, :] = h_im
        h_re, h_im = (ll_re * h_re - ll_im * h_im + x_ref[gi, 0, rows, :],
                      ll_re * h_im + ll_im * h_re + x_ref[gi, 1, rows, :])
    hl_ref[gi] = jnp.where(lower, h_re, pltpu.roll(h_im, n, axis=1))

    g1_re = jnp.concatenate([(g_re * l_re_c - g_im * l_im_c).astype(BF16), zeros_nk], axis=0)
    g1_im = jnp.concatenate([(g_re * l_im_c + g_im * l_re_c).astype(BF16), zeros_nk], axis=0)
    y_state = _bdot(hin_ref[gi, 0].astype(BF16), g1_re) - _bdot(hin_ref[gi, 1].astype(BF16), g1_im)

    y_ref[gi] = _bdot(ub, tm_ref[gi]) + y_state + u * d_ref[gi]


def _s5_kernel(*refs, gb, **kw):
    for gi in range(gb):
        _s5_group_body(gi, *refs, **kw)


def _s5(u_g, h0_g, prm, lp, lreal, nb, nc):
    g, m, k = u_g.shape
    gb = S5_GROUPS_PER_STEP
    blk = lambda *s: pl.BlockSpec((gb,) + s, lambda i: (i,) + (0,) * len(s))
    return pl.pallas_call(
        functools.partial(_s5_kernel, gb=gb, lp=lp, lreal=lreal, nb=nb, nc=nc),
        grid=(g // gb,),
        in_specs=[blk(m, k), blk(nb, 2 * SSM_STATE), blk(3, SSM_STATE), blk(SSM_STATE, 3),
                  blk(2, SSM_CH, SSM_STATE), blk(2, SSM_STATE, SSM_CH), blk(2, SSM_STATE, SSM_CH), blk(1, k)],
        out_specs=[blk(m, k), blk(nb, 2 * SSM_STATE)],
        out_shape=[jax.ShapeDtypeStruct((g, m, k), F32), jax.ShapeDtypeStruct((g, nb, 2 * SSM_STATE), F32)],
        scratch_shapes=[pltpu.VMEM((gb, k, k), BF16), pltpu.VMEM((gb, 2, m, 2 * SSM_STATE), F32),
                        pltpu.VMEM((gb, 2, m, 2 * SSM_STATE), F32)],
        compiler_params=_cparams(("parallel",), 32),
        name="s5_group",
    )(u_g, h0_g, prm["lrow"], prm["lcol"], prm["bt"], prm["bn"], prm["ct"], prm["dtile"][lp])


def _s5_mixer(u, h0, prm, lp, lreal):
    b, t, _ = u.shape
    nc = t // lreal
    ug = u.reshape(b * nc, lreal, SSM_GROUPS, SSM_CH)
    if lp != lreal:
        ug = jnp.pad(ug, ((0, 0), (0, lp - lreal), (0, 0), (0, 0)))
    ug = ug.transpose(2, 0, 1, 3).reshape(SSM_GROUPS, b * nc, lp * SSM_CH)
    h0g = h0.transpose(1, 0, 3, 2).reshape(SSM_GROUPS, b, 2 * SSM_STATE)
    yg, hl = _s5(ug, h0g, prm, lp, lreal, b, nc)
    y = yg.reshape(SSM_GROUPS, b * nc, lp, SSM_CH)[:, :, :lreal].transpose(1, 2, 0, 3)
    h_last = hl.reshape(SSM_GROUPS, b, 2, SSM_STATE).transpose(1, 0, 3, 2)
    return y.reshape(b, t, SSM_WIDTH), h_last


def _cmp_proj_kernel(*refs, n_in, rowwise):
    refs = refs[len(refs) - n_in - 3:]
    x_refs, (w1k_ref, w1v_ref, o_ref) = refs[:n_in], refs[n_in:]

    def rows_of(j, r):
        if rowwise:
            n = x_refs[0].shape[0] // (4 * CMP_STRIDE)
            parts = [x[pl.ds(4 * r + j, n, stride=4 * CMP_STRIDE), :] for x in x_refs]
            return parts[0] if n_in == 1 else jnp.concatenate(parts, axis=0)
        x = x_refs[j]
        return x[0, pl.ds(r, x.shape[1] // CMP_STRIDE, stride=CMP_STRIDE), :]

    for j in range(4):
        c = jnp.concatenate([rows_of(j, r) for r in range(CMP_STRIDE)], axis=1).astype(BF16)
        w = w1k_ref if j % 2 == 0 else w1v_ref
        o_ref[0, :, j * 2 * HEAD_DIM:(j + 1) * 2 * HEAD_DIM] = _bdot(c, w[...])


def _cmp_combine_kernel(h_ref, b1_ref, w2k_ref, w2v_ref, kn_ref, o_ref, *, nchunk):
    h = h_ref[0]
    keep = lax.broadcasted_iota(jnp.int32, (nchunk, 1), 0) < nchunk - 1
    for j in range(4):
        kvh, is_v = j // 2, j % 2
        lo = h[:, j * 2 * HEAD_DIM: j * 2 * HEAD_DIM + HEAD_DIM]
        hi = h[:, j * 2 * HEAD_DIM + HEAD_DIM:(j + 1) * 2 * HEAD_DIM]
        pre = lo + pltpu.roll(hi, nchunk - 1, axis=0) + b1_ref[is_v:is_v + 1, :]
        out = _bdot(_gelu(pre).astype(BF16), (w2v_ref if is_v else w2k_ref)[...])
        if not is_v:
            out = _rms(out, kn_ref[...])
        c0 = is_v * N_KV_HEADS * HEAD_DIM + kvh * HEAD_DIM
        o_ref[0, :, c0:c0 + HEAD_DIM] = jnp.where(keep, out, 0.0)


def _cmp_combine(h, prm):
    b, nchunk, w = h.shape
    return pl.pallas_call(
        functools.partial(_cmp_combine_kernel, nchunk=nchunk),
        grid=(b,),
        in_specs=[pl.BlockSpec((1, nchunk, w), lambda i: (i, 0, 0)), _resident((2, HEAD_DIM)),
                  _resident((HEAD_DIM, HEAD_DIM)), _resident((HEAD_DIM, HEAD_DIM)), _resident((1, HEAD_DIM))],
        out_specs=pl.BlockSpec((1, nchunk, 4 * HEAD_DIM), lambda i: (i, 0, 0)),
        out_shape=jax.ShapeDtypeStruct((b, nchunk, 4 * HEAD_DIM), F32),
        compiler_params=_cparams(("parallel",), 40),
        name="cmp_combine",
    )(h, prm["cmp_b1"], prm["cmp_w2_k"], prm["cmp_w2_v"], prm["k_norm_cmp"])


def _compress_prompt(kv_cmp, prm):
    b, t, w = kv_cmp.shape
    nchunk = t // CMP_STRIDE
    streams = w // HEAD_DIM
    h = pl.pallas_call(
        functools.partial(_cmp_proj_kernel, n_in=streams, rowwise=False),
        grid=(b,),
        in_specs=[pl.BlockSpec((1, t, HEAD_DIM), functools.partial(lambda i, j: (i, 0, j), j=j))
                  for j in range(streams)] + [
                  _resident((CMP_STRIDE * HEAD_DIM, 2 * HEAD_DIM)), _resident((CMP_STRIDE * HEAD_DIM, 2 * HEAD_DIM))],
        out_specs=pl.BlockSpec((1, nchunk, 8 * HEAD_DIM), lambda i: (i, 0, 0)),
        out_shape=jax.ShapeDtypeStruct((b, nchunk, 8 * HEAD_DIM), F32),
        compiler_params=_cparams(("parallel",), 40),
        name="cmp_proj_prompt",
    )(*([kv_cmp] * streams), prm["cmp_w1_k"], prm["cmp_w1_v"])
    return _cmp_combine(h, prm)


PAGES_PER_STEP = 16


def _rowwise_pages(cache):
    return cache.reshape(cache.shape[0], cache.shape[1], PAGE_SIZE * 2 * N_KV_HEADS, HEAD_DIM)


def _compress_sample(cache_cmp, layer, page_table, prm):
    bsz, n_pages = page_table.shape
    chunks_per_page = PAGE_SIZE // CMP_STRIDE
    pages = _rowwise_pages(cache_cmp)
    steps = n_pages // PAGES_PER_STEP

    def page_spec(p):
        return pl.BlockSpec((None, None, 4 * PAGE_SIZE, HEAD_DIM),
                            lambda b, s, pt: (layer, pt[b * n_pages + s * PAGES_PER_STEP + p], 0, 0))

    rows = PAGES_PER_STEP * chunks_per_page
    h = pl.pallas_call(
        functools.partial(_cmp_proj_kernel, n_in=PAGES_PER_STEP, rowwise=True),
        grid_spec=pltpu.PrefetchScalarGridSpec(
            num_scalar_prefetch=1,
            grid=(bsz, steps),
            in_specs=[page_spec(p) for p in range(PAGES_PER_STEP)] + [
                pl.BlockSpec((CMP_STRIDE * HEAD_DIM, 2 * HEAD_DIM), lambda b, s, pt: (0, 0)),
                pl.BlockSpec((CMP_STRIDE * HEAD_DIM, 2 * HEAD_DIM), lambda b, s, pt: (0, 0))],
            out_specs=pl.BlockSpec((1, rows, 8 * HEAD_DIM), lambda b, s, pt: (b, s, 0)),
        ),
        out_shape=jax.ShapeDtypeStruct((bsz, n_pages * chunks_per_page, 8 * HEAD_DIM), F32),
        compiler_params=_cparams(("parallel", "parallel"), 40),
        name="cmp_proj_sample",
    )(page_table.reshape(-1), *([pages] * PAGES_PER_STEP), prm["cmp_w1_k"], prm["cmp_w1_v"])
    return _cmp_combine(h, prm)


def _cmp_select_kernel(q_ref, ck_ref, cv_ref, o_ref, rank_ref, *idx_ref, tq, nc, nbp, nblk, qpos0):
    kvh = pl.program_id(1)
    qpos = qpos0 + pl.program_id(2) * tq + lax.broadcasted_iota(jnp.int32, (tq, 1), 0)
    assert tq & (tq - 1) == 0
    start = lax.broadcasted_iota(jnp.int32, (1, nc), 1) * CMP_STRIDE
    ck = ck_ref[0].astype(BF16)
    cv = cv_ref[0].astype(BF16)
    q_rows = jnp.concatenate([q_ref[0, :, g * HEAD_DIM:(g + 1) * HEAD_DIM] for g in range(GQA)], axis=0)
    row = lax.broadcasted_iota(jnp.int32, (GQA * tq, 1), 0)
    qpos_rows = qpos0 + pl.program_id(2) * tq + (row & (tq - 1))
    slope_rows = jnp.zeros((GQA * tq, 1), F32)
    for g in range(GQA):
        slope_rows = jnp.where(row >= g * tq, _head_slope(kvh, g), slope_rows)
    bias = slope_rows * ((qpos_rows - start).astype(F32) - (CMP_LEN - 1) / 2)
    _, e, den = _softmax_parts(_bdot_nt(q_rows, ck) - bias, (start + (CMP_LEN - 1)) <= qpos_rows)
    p = e / jnp.maximum(den, F32_TINY)
    o_rows = _bdot(p.astype(BF16), cv)
    psum = jnp.zeros((tq, nc), F32)
    for g in range(GQA):
        o_ref[0, :, g * HEAD_DIM:(g + 1) * HEAD_DIM] = o_rows[g * tq:(g + 1) * tq]
        psum = psum + p[g * tq:(g + 1) * tq]

    per_shift = (SEL_LEN // CMP_STRIDE).bit_length() - 1

    def ranks(blocks_on_rows):
        nbs = -(-nblk // 8) * 8
        ov_shape = (nbs, nc) if blocks_on_rows else (nc, nbp)
        ci = lax.broadcasted_iota(jnp.int32, ov_shape, 1 if blocks_on_rows else 0)
        bj = lax.broadcasted_iota(jnp.int32, ov_shape, 0 if blocks_on_rows else 1)
        overlap = (jnp.where(ci >> per_shift == bj, 1.0, 0.0)
                   + jnp.where((ci + 1) >> per_shift == bj, 1.0, 0.0)).astype(BF16)
        if blocks_on_rows:
            score = sum(_bdot_nt(overlap, part) for part in _split3(psum))
            j = lax.broadcasted_iota(jnp.int32, (nbs, 1), 0)
            pos = qpos0 + pl.program_id(2) * tq + lax.broadcasted_iota(jnp.int32, (1, tq), 1)
        else:
            score = _dot_small_int(psum, overlap)
            j = lax.broadcasted_iota(jnp.int32, (1, nbp), 1)
            pos = qpos
        qb = pos >> SEL_SHIFT
        forced = (j == 0) | (j == qb) | (j == qb - 1)
        score = jnp.where(forced, FORCED_SCORE, jnp.where((j * SEL_LEN) <= pos, score, -1.0))
        score = jnp.where(j < nblk, score, -2.0)
        rank = jnp.zeros(score.shape, F32)
        for i in range(nblk):
            si = score[i:i + 1, :] if blocks_on_rows else score[:, i:i + 1]
            rank = rank + jnp.where(j > i, jnp.where(si >= score, 1.0, 0.0), jnp.where(si > score, 1.0, 0.0))
        return rank

    if idx_ref or qpos0 >= N_SEL * SEL_LEN:
        rank = ranks(False)
        rank_ref[0, 0] = rank
    else:
        crowded = qpos0 + (pl.program_id(2) + 1) * tq > N_SEL * SEL_LEN

        @pl.when(crowded)
        def _():
            rank_t = ranks(True)
            unused = jnp.full((nbp - rank_t.shape[0], tq), float(nblk), F32)
            rank_ref[0, 0] = jnp.concatenate([rank_t, unused], axis=0).T

        @pl.when(jnp.logical_not(crowded))
        def _():
            rank_ref[0, 0] = jnp.zeros((tq, nbp), F32)

    if idx_ref:
        jf = lax.broadcasted_iota(jnp.int32, (1, nbp), 1).astype(F32)
        lane = lax.broadcasted_iota(jnp.int32, (tq, V7X_LANES), 1)
        idx = jnp.zeros((tq, V7X_LANES), F32)
        for r in range(N_SEL):
            col = jnp.sum(jnp.where(rank == float(r), jf, 0.0), axis=-1, keepdims=True)
            idx = jnp.where(lane == r, col, idx)
        idx_ref[0][0, 0] = idx.astype(jnp.int32)


def _cmp_select(q, ckv, tq, nblk, qpos0, want_idx):
    b, t, _ = q.shape
    nc = ckv.shape[1]
    nbp = -(-nblk // V7X_LANES) * V7X_LANES
    hw = GQA * HEAD_DIM
    out_specs = [pl.BlockSpec((1, tq, hw), lambda bi, k, i: (bi, i, k)),
                 pl.BlockSpec((1, 1, tq, nbp), lambda bi, k, i: (bi, k, i, 0))]
    out_shape = [jax.ShapeDtypeStruct((b, t, ATTN_WIDTH), F32),
                 jax.ShapeDtypeStruct((b, N_KV_HEADS, t, nbp), F32)]
    if want_idx:
        out_specs.append(pl.BlockSpec((1, 1, tq, V7X_LANES), lambda bi, k, i: (bi, k, i, 0)))
        out_shape.append(jax.ShapeDtypeStruct((b, N_KV_HEADS, t, V7X_LANES), jnp.int32))
    return pl.pallas_call(
        functools.partial(_cmp_select_kernel, tq=tq, nc=nc, nbp=nbp, nblk=nblk, qpos0=qpos0),
        grid=(b, N_KV_HEADS, t // tq),
        in_specs=[pl.BlockSpec((1, tq, hw), lambda bi, k, i: (bi, i, k)),
                  pl.BlockSpec((1, nc, HEAD_DIM), lambda bi, k, i: (bi, 0, k)),
                  pl.BlockSpec((1, nc, HEAD_DIM), lambda bi, k, i: (bi, 0, N_KV_HEADS + k))],
        out_specs=out_specs,
        out_shape=out_shape,
        compiler_params=_cparams(("parallel", "parallel", "parallel"), 40),
        name="cmp_select",
    )(q, ckv, ckv)


ATT_TQ = 256
SEL_TK = 512
FILL_ROWS = 256
POS_RADIX = 256
RATE_TERMS = 3
MASK_BIAS = 2.0 ** 100


def _key_features(kidx, nblk):
    lane = lax.broadcasted_iota(jnp.int32, (1, V7X_LANES), 1)
    radix_shift = POS_RADIX.bit_length() - 1
    digits = jnp.where(((lane - nblk) & 1) == 0, (kidx >> radix_shift).astype(F32),
                       (kidx & (POS_RADIX - 1)).astype(F32))
    feats = jnp.where(lane < nblk + 2 * RATE_TERMS, digits, 0.0)
    if nblk:
        feats = jnp.where(lane < nblk, jnp.where((kidx >> SEL_SHIFT) == lane, 1.0, 0.0), feats)
    return feats


def _query_features(kvh, g, rank, nblk, rows):
    lane = lax.broadcasted_iota(jnp.int32, (rows, V7X_LANES), 1)
    feats = jnp.zeros((rows, V7X_LANES), F32)
    if nblk:
        feats = jnp.where((lane < nblk) & (rank >= float(N_SEL)), -MASK_BIAS, 0.0)
    for i, (c0, c1) in enumerate(zip(_bf16_terms(_alibi_slope(0, g), RATE_TERMS),
                                     _bf16_terms(_alibi_slope(1, g), RATE_TERMS))):
        c = jnp.where(kvh == 0, c0, c1)
        feats = jnp.where(lane == nblk + 2 * i, c * POS_RADIX, feats)
        feats = jnp.where(lane == nblk + 2 * i + 1, c, feats)
    return feats


def _fill_key_value_scratch(k_ref, v_ref, kf_ref, vo_ref, nblk):
    def chunk(c, carry):
        r0 = pl.multiple_of(c * FILL_ROWS, FILL_ROWS)
        rows = pl.ds(r0, FILL_ROWS)
        kidx = r0 + lax.broadcasted_iota(jnp.int32, (FILL_ROWS, 1), 0)
        kf_ref[rows, :HEAD_DIM] = k_ref[0, rows, :].astype(BF16)
        kf_ref[rows, HEAD_DIM:] = _key_features(kidx, nblk).astype(BF16)
        vo_ref[rows, :HEAD_DIM] = v_ref[0, rows, :].astype(BF16)
        vo_ref[rows, HEAD_DIM:] = jnp.ones((FILL_ROWS, V7X_LANES), BF16)
        return carry

    lax.fori_loop(0, k_ref.shape[1] // FILL_ROWS, chunk, 0)


def _attn_out(acc):
    return acc[:, :HEAD_DIM] / jnp.maximum(acc[:, HEAD_DIM:HEAD_DIM + 1], F32_TINY)


def _sel_prompt_kernel(q_ref, k_ref, v_ref, rank_ref, o_ref, kf_ref, vo_ref, qf_ref, m_ref, acc_ref, *, nblk):
    kvh, i = pl.program_id(1), pl.program_id(2)

    @pl.when(i == 0)
    def _():
        _fill_key_value_scratch(k_ref, v_ref, kf_ref, vo_ref, nblk)

    q0 = i * ATT_TQ
    qpos = q0 + (lax.broadcasted_iota(jnp.int32, (GQA * ATT_TQ, 1), 0) & (ATT_TQ - 1))
    rank = rank_ref[0, 0]
    for g in range(GQA):
        rows = slice(g * ATT_TQ, (g + 1) * ATT_TQ)
        qf_ref[rows, :HEAD_DIM] = q_ref[0, :, g * HEAD_DIM:(g + 1) * HEAD_DIM]
        qf_ref[rows, HEAD_DIM:] = _query_features(kvh, g, rank, nblk, ATT_TQ).astype(BF16)
    m_ref[...] = jnp.full(m_ref.shape, NEG_INF, F32)
    acc_ref[...] = jnp.zeros(acc_ref.shape, F32)

    def tile(jt, carry):
        k0 = pl.multiple_of(jt * SEL_TK, SEL_TK)
        causal = (k0 + lax.broadcasted_iota(jnp.int32, (1, SEL_TK), 1)) <= qpos
        s = jnp.where(causal, _bdot_nt(qf_ref[...], kf_ref[pl.ds(k0, SEL_TK), :]), NEG_INF)
        _flash_step(m_ref, acc_ref, 0, s, vo_ref[pl.ds(k0, SEL_TK), :])
        return carry

    lax.fori_loop(0, (q0 + ATT_TQ + SEL_TK - 1) // SEL_TK, tile, 0)
    out = _attn_out(acc_ref[0])
    for g in range(GQA):
        o_ref[0, :, g * HEAD_DIM:(g + 1) * HEAD_DIM] = out[g * ATT_TQ:(g + 1) * ATT_TQ]


def _sel_prompt(q, kv_slc, rank, nblk):
    b, t, _ = q.shape
    nbp = rank.shape[-1]
    hw = GQA * HEAD_DIM
    assert nbp == V7X_LANES and nblk + 2 * RATE_TERMS <= V7X_LANES and t <= POS_RADIX * POS_RADIX
    return pl.pallas_call(
        functools.partial(_sel_prompt_kernel, nblk=nblk),
        grid=(b, N_KV_HEADS, t // ATT_TQ),
        in_specs=[pl.BlockSpec((1, ATT_TQ, hw), lambda bi, k, i: (bi, i, k)),
                  pl.BlockSpec((1, t, HEAD_DIM), lambda bi, k, i: (bi, 0, 2 * k)),
                  pl.BlockSpec((1, t, HEAD_DIM), lambda bi, k, i: (bi, 0, 2 * k + 1)),
                  pl.BlockSpec((1, 1, ATT_TQ, nbp), lambda bi, k, i: (bi, k, i, 0))],
        out_specs=pl.BlockSpec((1, ATT_TQ, hw), lambda bi, k, i: (bi, i, k)),
        out_shape=jax.ShapeDtypeStruct((b, t, ATTN_WIDTH), F32),
        scratch_shapes=[pltpu.VMEM((t, 2 * HEAD_DIM), BF16), pltpu.VMEM((t, 2 * HEAD_DIM), BF16),
                        pltpu.VMEM((GQA * ATT_TQ, 2 * HEAD_DIM), BF16),
                        pltpu.VMEM((1, GQA * ATT_TQ, 1), F32), pltpu.VMEM((1, GQA * ATT_TQ, 2 * HEAD_DIM), F32)],
        compiler_params=_cparams(("parallel", "parallel", "arbitrary"), 32),
        name="sel_prompt",
    )(q, kv_slc, kv_slc, rank)


def _win_prompt_kernel(q_ref, k_ref, v_ref, o_ref, kf_ref, vo_ref):
    kvh, i = pl.program_id(1), pl.program_id(2)

    @pl.when(i == 0)
    def _():
        _fill_key_value_scratch(k_ref, v_ref, kf_ref, vo_ref, 0)

    span = WINDOW + WIN_TQ
    qpos = i * WIN_TQ + (lax.broadcasted_iota(jnp.int32, (GQA * WIN_TQ, 1), 0) & (WIN_TQ - 1))
    k0 = pl.multiple_of(jnp.maximum(i * WIN_TQ - WINDOW, 0), WIN_TQ)
    d = qpos - (k0 + lax.broadcasted_iota(jnp.int32, (1, span), 1))
    qf = jnp.concatenate(
        [jnp.concatenate([q_ref[0, :, g * HEAD_DIM:(g + 1) * HEAD_DIM],
                          _query_features(kvh, g, None, 0, WIN_TQ).astype(BF16)], axis=1) for g in range(GQA)],
        axis=0)
    s = jnp.where((d >= 0) & (d < WINDOW), _bdot_nt(qf, kf_ref[pl.ds(k0, span), :]), NEG_INF)
    e = jnp.exp2(s - jnp.max(s, axis=-1, keepdims=True))
    out = _attn_out(_bdot(e.astype(BF16), vo_ref[pl.ds(k0, span), :]))
    for g in range(GQA):
        o_ref[0, :, g * HEAD_DIM:(g + 1) * HEAD_DIM] = out[g * WIN_TQ:(g + 1) * WIN_TQ]


def _win_prompt(q, kv_win):
    b, t, _ = q.shape
    hw = GQA * HEAD_DIM
    assert t <= POS_RADIX * POS_RADIX
    return pl.pallas_call(
        _win_prompt_kernel,
        grid=(b, N_KV_HEADS, t // WIN_TQ),
        in_specs=[pl.BlockSpec((1, WIN_TQ, hw), lambda bi, k, i: (bi, i, k)),
                  pl.BlockSpec((1, t, HEAD_DIM), lambda bi, k, i: (bi, 0, 2 * k)),
                  pl.BlockSpec((1, t, HEAD_DIM), lambda bi, k, i: (bi, 0, 2 * k + 1))],
        out_specs=pl.BlockSpec((1, WIN_TQ, hw), lambda bi, k, i: (bi, i, k)),
        out_shape=jax.ShapeDtypeStruct((b, t, ATTN_WIDTH), F32),
        scratch_shapes=[pltpu.VMEM((t, 2 * HEAD_DIM), BF16), pltpu.VMEM((t, 2 * HEAD_DIM), BF16)],
        compiler_params=_cparams(("parallel", "parallel", "arbitrary"), 32),
        name="win_prompt",
    )(q, kv_win, kv_win)


def _slope_col(kvh):
    g = lax.broadcasted_iota(jnp.int32, (GQA, 1), 0)
    col = jnp.zeros((GQA, 1), F32)
    for i in range(GQA):
        col = jnp.where(g == i, _alibi_slope(kvh, i), col)
    return col


def _stream_rows(ref, lead, stream, n):
    return ref[lead + (pl.ds(stream, n, stride=2 * N_KV_HEADS), slice(None))]


def _sel_sample_kernel(idx_ref, pt_ref, q_ref, *refs, t_len):
    past, new_ref, o_ref = refs[:N_KV_HEADS * N_SEL], refs[N_KV_HEADS * N_SEL], refs[N_KV_HEADS * N_SEL + 1]
    b, t = pl.program_id(0), pl.program_id(1)
    past_blocks = PAST_LEN // SEL_LEN
    lane = lax.broadcasted_iota(jnp.int32, (1, N_SEL * SEL_LEN), 1)
    for kvh in range(N_KV_HEADS):
        ks, vs = [], []
        base = jnp.zeros((1, N_SEL * SEL_LEN), jnp.int32)
        for r in range(N_SEL):
            blk = idx_ref[((b * N_KV_HEADS + kvh) * t_len + t) * N_SEL + r]
            is_new = blk >= past_blocks
            ref = past[kvh * N_SEL + r]
            ks.append(jnp.where(is_new, _stream_rows(new_ref, (0,), 2 * kvh, SEL_LEN),
                                _stream_rows(ref, (), 2 * kvh, SEL_LEN)).astype(BF16))
            vs.append(jnp.where(is_new, _stream_rows(new_ref, (0,), 2 * kvh + 1, SEL_LEN),
                                _stream_rows(ref, (), 2 * kvh + 1, SEL_LEN)).astype(BF16))
            base = jnp.where((lane >> SEL_SHIFT) == r, blk * SEL_LEN, base)
        d = (PAST_LEN + t) - (base + (lane & (SEL_LEN - 1)))
        rows = slice(kvh * GQA, (kvh + 1) * GQA)
        s = _bdot_nt(q_ref[0, 0, rows, :], jnp.concatenate(ks, axis=0)) - _slope_col(kvh) * d.astype(F32)
        _, e, z = _softmax_parts(s, d >= 0)
        o_ref[0, 0, rows, :] = _bdot(e.astype(BF16), jnp.concatenate(vs, axis=0)) / jnp.maximum(z, F32_TINY)


def _sel_sample(q_h, cache_slc, layer, page_table, idx, kv_new_rows):
    bsz, t_len = q_h.shape[:2]
    n_pages = page_table.shape[1]
    per_page = PAGE_SIZE // SEL_LEN
    past_blocks = PAST_LEN // SEL_LEN
    pages = _rowwise_pages(cache_slc)
    blk_rows = SEL_LEN * 2 * N_KV_HEADS

    def past_spec(kvh, r):
        def imap(b, t, idx_ref, pt_ref):
            blk = jnp.minimum(idx_ref[((b * N_KV_HEADS + kvh) * t_len + t) * N_SEL + r], past_blocks - 1)
            return (layer, pt_ref[b * n_pages + blk // per_page], blk % per_page, 0)
        return pl.BlockSpec((None, None, blk_rows, HEAD_DIM), imap)

    qspec = pl.BlockSpec((1, 1, N_HEADS, HEAD_DIM), lambda b, t, i_, p_: (b, t, 0, 0))
    return pl.pallas_call(
        functools.partial(_sel_sample_kernel, t_len=t_len),
        grid_spec=pltpu.PrefetchScalarGridSpec(
            num_scalar_prefetch=2,
            grid=(bsz, t_len),
            in_specs=[qspec] + [past_spec(kvh, r) for kvh in range(N_KV_HEADS) for r in range(N_SEL)]
            + [pl.BlockSpec((1, blk_rows, HEAD_DIM), lambda b, t, i_, p_: (b, 0, 0))],
            out_specs=qspec,
        ),
        out_shape=jax.ShapeDtypeStruct(q_h.shape, F32),
        compiler_params=_cparams(("parallel", "parallel"), 32),
        name="sel_sample",
    )(idx.reshape(-1), page_table.reshape(-1), q_h, *([pages] * (N_KV_HEADS * N_SEL)), kv_new_rows)


def _win_sample_kernel(q_ref, past_ref, new_ref, o_ref, *, t_len, t_pad, wbuf):
    t = lax.broadcasted_iota(jnp.int32, (t_len, 1), 0)
    d_past = wbuf + t - lax.broadcasted_iota(jnp.int32, (1, wbuf), 1)
    row_new = lax.broadcasted_iota(jnp.int32, (1, t_pad), 1)
    d_new = t - row_new
    valid_past = (d_past >= 0) & (d_past < WINDOW)
    valid_new = (d_new >= 0) & (d_new < WINDOW) & (row_new < t_len)
    for kvh in range(N_KV_HEADS):
        kp = _stream_rows(past_ref, (0,), 2 * kvh, wbuf).astype(BF16)
        vp = _stream_rows(past_ref, (0,), 2 * kvh + 1, wbuf).astype(BF16)
        kn = _stream_rows(new_ref, (0,), 2 * kvh, t_pad).astype(BF16)
        vn = _stream_rows(new_ref, (0,), 2 * kvh + 1, t_pad).astype(BF16)
        for g in range(GQA):
            q = q_ref[0, kvh, g]
            slope = _alibi_slope(kvh, g)
            m1, e1, z1 = _softmax_parts(_bdot_nt(q, kp) - slope * d_past.astype(F32), valid_past)
            m2, e2, z2 = _softmax_parts(_bdot_nt(q, kn) - slope * d_new.astype(F32), valid_new)
            m = jnp.maximum(m1, m2)
            a1 = jnp.where(z1 > 0.0, jnp.exp2(m1 - m), 0.0)
            a2 = jnp.where(z2 > 0.0, jnp.exp2(m2 - m), 0.0)
            num = a1 * _bdot(e1.astype(BF16), vp) + a2 * _bdot(e2.astype(BF16), vn)
            o_ref[0, kvh, g] = num / jnp.maximum(a1 * z1 + a2 * z2, F32_TINY)


def _win_sample(q_g, win_rows, new_rows):
    bsz, _, _, t_len, _ = q_g.shape
    streams = 2 * N_KV_HEADS
    wbuf, t_pad = win_rows.shape[1] // streams, new_rows.shape[1] // streams
    qspec = pl.BlockSpec((1, N_KV_HEADS, GQA, t_len, HEAD_DIM), lambda b: (b, 0, 0, 0, 0))
    return pl.pallas_call(
        functools.partial(_win_sample_kernel, t_len=t_len, t_pad=t_pad, wbuf=wbuf),
        grid=(bsz,),
        in_specs=[qspec, pl.BlockSpec((1, wbuf * streams, HEAD_DIM), lambda b: (b, 0, 0)),
                  pl.BlockSpec((1, t_pad * streams, HEAD_DIM), lambda b: (b, 0, 0))],
        out_specs=qspec,
        out_shape=jax.ShapeDtypeStruct(q_g.shape, F32),
        compiler_params=_cparams(("parallel",), 16),
        name="win_sample",
    )(q_g, win_rows, new_rows)


def _mix_kernel(x_ref, y_ref, oc_ref, os_ref, ow_ref, gate_ref, wglu_ref, gs_ref, ga_ref, wout_ref, o_ref):
    z = _gelu(y_ref[...])
    z = z * jax.nn.sigmoid(_bdot(z.astype(BF16), wglu_ref[...]))
    gate = gate_ref[...]
    cols = []
    for h in range(N_HEADS):
        sl = slice(h * HEAD_DIM, (h + 1) * HEAD_DIM)
        cols.append(gate[:, 3 * h:3 * h + 1] * oc_ref[:, sl] + gate[:, 3 * h + 1:3 * h + 2] * os_ref[:, sl]
                    + gate[:, 3 * h + 2:3 * h + 3] * ow_ref[:, sl])
    attn = jnp.concatenate(cols, axis=1)
    hs = _rms(z, gs_ref[...]).astype(BF16)
    ha = _rms(attn, ga_ref[...]).astype(BF16)
    o_ref[...] = x_ref[...] + (_bdot(hs, wout_ref[:SSM_WIDTH, :]) + _bdot(ha, wout_ref[SSM_WIDTH:, :]))


def _mix(x, y, oc, osl, ow, gates, prm, tm):
    m = x.shape[0]
    row = lambda n: pl.BlockSpec((tm, n), lambda i: (i, 0))
    return pl.pallas_call(
        _mix_kernel,
        grid=(m // tm,),
        in_specs=[row(D_MODEL), row(SSM_WIDTH), row(ATTN_WIDTH), row(ATTN_WIDTH), row(ATTN_WIDTH), row(N_GATES),
                  _resident_layer((SSM_WIDTH, SSM_WIDTH), prm["layer"]), _resident((1, SSM_WIDTH)),
                  _resident((1, ATTN_WIDTH)), _resident_layer((D_MODEL, D_MODEL), prm["layer"])],
        out_specs=row(D_MODEL),
        out_shape=jax.ShapeDtypeStruct((m, D_MODEL), F32),
        compiler_params=_cparams(("parallel",), 48),
        name="mix_outproj",
    )(x, y, oc, osl, ow, gates, prm["ssm_w_glu"], prm["norm_ssm_out"], prm["norm_attn_out"], prm["w_out"])


FFN_TF = 512


def _ffn_kernel(x_ref, g_ref, wg_ref, wu_ref, wd_ref, o_ref, xn_ref, acc_ref):
    j = pl.program_id(1)

    @pl.when(j == 0)
    def _():
        xn_ref[...] = _rms(x_ref[...], g_ref[...]).astype(BF16)
        acc_ref[...] = jnp.zeros(acc_ref.shape, F32)

    xn = xn_ref[...]
    a = _bdot(xn, wg_ref[...])
    h = (a * jax.nn.sigmoid(a)) * _bdot(xn, wu_ref[...])
    acc_ref[...] += _bdot(h.astype(BF16), wd_ref[...])

    @pl.when(j == pl.num_programs(1) - 1)
    def _():
        o_ref[...] = x_ref[...] + acc_ref[...]


def _ffn(x, prm, tm):
    m = x.shape[0]
    layer = prm["layer"]
    return pl.pallas_call(
        _ffn_kernel,
        grid=(m // tm, D_FF // FFN_TF),
        in_specs=[pl.BlockSpec((tm, D_MODEL), lambda i, j: (i, 0)),
                  pl.BlockSpec((1, D_MODEL), lambda i, j: (0, 0)),
                  pl.BlockSpec((None, D_MODEL, FFN_TF), lambda i, j: (layer, 0, j)),
                  pl.BlockSpec((None, D_MODEL, FFN_TF), lambda i, j: (layer, 0, j)),
                  pl.BlockSpec((None, FFN_TF, D_MODEL), lambda i, j: (layer, j, 0))],
        out_specs=pl.BlockSpec((tm, D_MODEL), lambda i, j: (i, 0)),
        out_shape=jax.ShapeDtypeStruct((m, D_MODEL), F32),
        scratch_shapes=[pltpu.VMEM((tm, D_MODEL), BF16), pltpu.VMEM((tm, D_MODEL), F32)],
        compiler_params=_cparams(("parallel", "arbitrary"), 48),
        name="ffn",
    )(x, prm["norm_ffn"], prm["w_ffn_gate"], prm["w_ffn_up"], prm["w_ffn_down"])


S5_CHUNK = 32
S5_GROUPS_PER_STEP = 2
SAMPLE_S5_PAD = 8


BIG_WEIGHTS = ("w_in", "ssm_w_glu", "w_out", "w_ffn_gate", "w_ffn_up", "w_ffn_down")


def _layer_params(l, w, big):
    def w1cat(w1):
        half = CMP_STRIDE * HEAD_DIM
        return jnp.concatenate([w1[:half], w1[half:]], axis=1).astype(BF16)

    row = lambda v: v.reshape(1, -1)
    a_re, a_im = w["ssm_a_re"][l], w["ssm_a_im"][l]
    log_dt = jnp.broadcast_to(w["ssm_log_dt"][l][:, None], a_re.shape)
    lrow = jnp.stack([a_re, a_im, log_dt], axis=1)
    d = w["ssm_d"][l]
    return {
        "layer": l, **big, "norm_mix": row(w["norm_mix"][l]),
        "q_norm": row(w["q_norm"][l]), "k_norm_slc": row(w["k_norm_slc"][l]),
        "k_norm_win": row(w["k_norm_win"][l]), "k_norm_cmp": row(w["k_norm_cmp"][l]),
        "lrow": lrow, "lcol": lrow.transpose(0, 2, 1),
        "bt": jnp.stack([w["ssm_b_re"][l], w["ssm_b_im"][l]], axis=1).transpose(0, 1, 3, 2),
        "bn": jnp.stack([w["ssm_b_re"][l], w["ssm_b_im"][l]], axis=1),
        "ct": jnp.stack([w["ssm_c_re"][l], w["ssm_c_im"][l]], axis=1).transpose(0, 1, 3, 2),
        "dtile": {lp: jnp.tile(d, (1, lp))[:, None, :] for lp in (S5_CHUNK, SAMPLE_S5_PAD)},
        "cmp_w1_k": w1cat(w["cmp_w1_k"][l]), "cmp_w1_v": w1cat(w["cmp_w1_v"][l]),
        "cmp_b1": jnp.stack([w["cmp_b1_k"][l], w["cmp_b1_v"][l]]),
        "cmp_w2_k": w["cmp_w2_k"][l].astype(BF16), "cmp_w2_v": w["cmp_w2_v"][l].astype(BF16),
        "norm_ssm_out": row(w["norm_ssm_out"][l]), "norm_attn_out": row(w["norm_attn_out"][l]),
        "norm_ffn": row(w["norm_ffn"][l]),
    }


def _kv_out(kv, b, t):
    return kv.reshape(b, t, N_KV_HEADS, 2, HEAD_DIM)


PROMPT_TM = 256
MIX_TM = 256
FFN_TM = 512


def _prompt_layer(x, prm):
    b, t, _ = x.shape
    m = b * t
    x2 = x.reshape(m, D_MODEL)
    u, q, kv_cmp, kv_slc, kv_win, gates = _inproj(x2, prm, PROMPT_TM)
    h0 = jnp.zeros((b, SSM_GROUPS, SSM_STATE, 2), F32)
    y, h_last = _s5_mixer(u.reshape(b, t, SSM_WIDTH), h0, prm, S5_CHUNK, S5_CHUNK)
    q3 = q.reshape(b, t, ATTN_WIDTH)
    kv_cmp3, kv_slc3, kv_win3 = (a.reshape(b, t, 2 * KV_WIDTH) for a in (kv_cmp, kv_slc, kv_win))
    ckv = _compress_prompt(kv_cmp3, prm)
    o_cmp, rank = _cmp_select(q3, ckv, 256, t // SEL_LEN, 0, False)
    o_slc = _sel_prompt(q3, kv_slc3, rank, t // SEL_LEN)
    o_win = _win_prompt(q3, kv_win3)
    x2 = _mix(x2, y.reshape(m, SSM_WIDTH), o_cmp.reshape(m, -1), o_slc.reshape(m, -1), o_win.reshape(m, -1),
              gates, prm, MIX_TM)
    x2 = _ffn(x2, prm, FFN_TM)
    wbuf = min(WINDOW, PAST_LEN)
    win_state = kv_win3[:, t - wbuf:] if t >= wbuf else jnp.pad(kv_win3, ((0, 0), (wbuf - t, 0), (0, 0)))
    return (x2.reshape(b, t, D_MODEL), _kv_out(kv_cmp3, b, t), _kv_out(kv_slc3, b, t),
            _kv_out(win_state, b, wbuf), h_last)


def _sample_layer(x, prm, cache_cmp, cache_slc, layer, page_table, win_buf, h0):
    b, t, _ = x.shape
    m = b * t
    assert (PAST_LEN + t) // CMP_STRIDE == PAST_LEN // CMP_STRIDE and PAST_LEN % PAGE_SIZE == 0
    x2 = x.reshape(m, D_MODEL)
    u, q, kv_cmp, kv_slc, kv_win, gates = _inproj(x2, prm, m)
    y, h_last = _s5_mixer(u.reshape(b, t, SSM_WIDTH), h0, prm, SAMPLE_S5_PAD, t)
    q3 = q.reshape(b, t, ATTN_WIDTH)
    kv_cmp3, kv_slc3, kv_win3 = (a.reshape(b, t, 2 * KV_WIDTH) for a in (kv_cmp, kv_slc, kv_win))
    ckv = _compress_sample(cache_cmp, layer, page_table, prm)
    n_blocks = -(-(PAST_LEN + t) // SEL_LEN)
    o_cmp, _, idx = _cmp_select(q3, ckv, t, n_blocks, PAST_LEN, True)
    streams = 2 * N_KV_HEADS

    def stream_rows(kv3, t_pad):
        return jnp.pad(kv3.reshape(b, t * streams, HEAD_DIM), ((0, 0), (0, (t_pad - t) * streams), (0, 0)))

    o_slc = _sel_sample(q3.reshape(b, t, N_HEADS, HEAD_DIM), cache_slc, layer, page_table, idx[..., :N_SEL],
                        stream_rows(kv_slc3, SEL_LEN)).reshape(m, ATTN_WIDTH)
    wbuf = win_buf.shape[1]
    q5 = q3.reshape(b, t, N_KV_HEADS, GQA, HEAD_DIM)
    o_win = _win_sample(q5.transpose(0, 2, 3, 1, 4), win_buf.reshape(b, wbuf * streams, HEAD_DIM),
                        stream_rows(kv_win3, 8))
    o_win = o_win.transpose(0, 3, 1, 2, 4).reshape(m, ATTN_WIDTH)
    x2 = _mix(x2, y.reshape(m, SSM_WIDTH), o_cmp.reshape(m, -1), o_slc, o_win, gates, prm, m)
    x2 = _ffn(x2, prm, m)
    win_state = jnp.concatenate([win_buf, _kv_out(kv_win3, b, t)], axis=1)[:, -wbuf:]
    return (x2.reshape(b, t, D_MODEL), _kv_out(kv_cmp3, b, t), _kv_out(kv_slc3, b, t), win_state, h_last)


def kernel(x_prompt, x_sample, cache_cmp_kv, cache_slc_kv, state_win_kv, state_ssm, page_table, norm_mix, w_in, ssm_a_re, ssm_a_im, ssm_log_dt, ssm_b_re, ssm_b_im, ssm_c_re, ssm_c_im, ssm_d, ssm_w_glu, q_norm, k_norm_cmp, k_norm_slc, k_norm_win, cmp_w1_k, cmp_b1_k, cmp_w2_k, cmp_w1_v, cmp_b1_v, cmp_w2_v, norm_ssm_out, norm_attn_out, w_out, norm_ffn, w_ffn_gate, w_ffn_up, w_ffn_down):
    w = dict(norm_mix=norm_mix, w_in=w_in, ssm_a_re=ssm_a_re, ssm_a_im=ssm_a_im, ssm_log_dt=ssm_log_dt,
             ssm_b_re=ssm_b_re, ssm_b_im=ssm_b_im, ssm_c_re=ssm_c_re, ssm_c_im=ssm_c_im, ssm_d=ssm_d,
             ssm_w_glu=ssm_w_glu, q_norm=q_norm, k_norm_cmp=k_norm_cmp, k_norm_slc=k_norm_slc,
             k_norm_win=k_norm_win, cmp_w1_k=cmp_w1_k, cmp_b1_k=cmp_b1_k, cmp_w2_k=cmp_w2_k,
             cmp_w1_v=cmp_w1_v, cmp_b1_v=cmp_b1_v, cmp_w2_v=cmp_w2_v, norm_ssm_out=norm_ssm_out,
             norm_attn_out=norm_attn_out, w_out=w_out, norm_ffn=norm_ffn, w_ffn_gate=w_ffn_gate,
             w_ffn_up=w_ffn_up, w_ffn_down=w_ffn_down)
    big = {name: w[name].astype(BF16) for name in BIG_WEIGHTS}
    y_p, y_s = x_prompt, x_sample
    outs_p, outs_s = [], []
    for l in range(DEPTH):
        prm = _layer_params(l, w, big)
        y_p, *rest = _prompt_layer(y_p, prm)
        outs_p.append(rest)
        y_s, *rest = _sample_layer(y_s, prm, cache_cmp_kv, cache_slc_kv, l, page_table,
                                   state_win_kv[l], state_ssm[l])
        outs_s.append(rest)
    stack = lambda outs, i: jnp.stack([o[i] for o in outs])
    return (y_p, y_s, stack(outs_p, 0), stack(outs_p, 1), stack(outs_p, 2), stack(outs_p, 3),
            stack(outs_s, 0), stack(outs_s, 1), stack(outs_s, 2), stack(outs_s, 3))
```

```python
import functools
import math

import jax
import jax.numpy as jnp
import numpy as np
from jax import lax
from jax.experimental import pallas as pl
from jax.experimental.pallas import tpu as pltpu

F32 = jnp.float32
BF16 = jnp.bfloat16

D_MODEL = 2048
DEPTH = 2
PAST_LEN = 16384
PAGE_SIZE = 128
SSM_WIDTH = 1024
ATTN_WIDTH = 1024
SSM_CH = 16
SSM_GROUPS = 64
SSM_STATE = 64
HEAD_DIM = 128
N_HEADS = 8
N_KV_HEADS = 2
GQA = 4
KV_WIDTH = 256
KV_STREAMS = 2 * N_KV_HEADS
CMP_LEN = 32
CMP_STRIDE = 16
SEL_LEN = 64
SEL_SHIFT = SEL_LEN.bit_length() - 1
N_SEL = 16
WINDOW = 512
WIN_TQ = 256
FORCED_SCORE = 1e4
D_FF = 5632
NORM_EPS = 1e-6
LOG2E = math.log2(math.e)
QSCALE = HEAD_DIM ** -0.5 * LOG2E
N_GATES = N_HEADS * 3
IN_WIDTH = SSM_WIDTH + ATTN_WIDTH + 6 * KV_WIDTH + N_GATES
F32_TINY = float(jnp.finfo(jnp.float32).tiny)
NEG_INF = float("-inf")

V7X_VMEM_BYTES = 64 * 1024 * 1024
V7X_LANES = 128


def _cparams(semantics, vmem_mib):
    assert vmem_mib * 1024 * 1024 < V7X_VMEM_BYTES
    return pltpu.CompilerParams(dimension_semantics=semantics, vmem_limit_bytes=vmem_mib * 1024 * 1024)


def _resident(shape):
    nd = len(shape)
    return pl.BlockSpec(shape, lambda *_: (0,) * nd, pipeline_mode=pl.Buffered(1))


def _resident_layer(shape, layer):
    nd = len(shape)
    return pl.BlockSpec((None,) + tuple(shape), lambda *_: (layer,) + (0,) * nd, pipeline_mode=pl.Buffered(1))


def _rms(x, g):
    return x * lax.rsqrt(jnp.mean(x * x, axis=-1, keepdims=True) + NORM_EPS) * g


def _gelu(x):
    c = math.sqrt(2.0 / math.pi)
    return x * (0.5 * (1.0 + jnp.tanh(c * (x + 0.044715 * (x * x * x)))))


def _bdot(a, b):
    return jnp.dot(a, b, preferred_element_type=F32)


def _bdot_nt(a, b):
    return lax.dot_general(a, b, (((1,), (1,)), ((), ())), preferred_element_type=F32)


def _split3(x):
    hi = x.astype(BF16)
    r1 = x - hi.astype(F32)
    mid = r1.astype(BF16)
    lo = (r1 - mid.astype(F32)).astype(BF16)
    return hi, mid, lo


def _dot_small_int(x, e):
    hi, mid, lo = _split3(x)
    return _bdot(hi, e) + _bdot(mid, e) + _bdot(lo, e)


def _dot_f32(a, b):
    ah, am, al = _split3(a)
    bh, bm, bl = _split3(b)
    return (_bdot(ah, bh) + (_bdot(ah, bm) + _bdot(am, bh))
            + (_bdot(ah, bl) + _bdot(al, bh) + _bdot(am, bm)))


def _softmax_parts(s, valid):
    s = jnp.where(valid, s, NEG_INF)
    m = jnp.max(s, axis=-1, keepdims=True)
    m = jnp.where(m == NEG_INF, 0.0, m)
    e = jnp.exp2(s - m)
    return m, e, jnp.sum(e, axis=-1, keepdims=True)


def _flash_step(m_ref, acc_ref, g, s, v_ones):
    m_old = m_ref[g]
    m_new = jnp.maximum(m_old, jnp.max(s, axis=-1, keepdims=True))
    m_safe = jnp.where(m_new == NEG_INF, 0.0, m_new)
    acc_ref[g] = jnp.exp2(m_old - m_safe) * acc_ref[g] + _bdot(jnp.exp2(s - m_safe).astype(BF16), v_ones)
    m_ref[g] = m_new


def _alibi_slope(kvh, g):
    return LOG2E * 2.0 ** (-8.0 * (kvh * GQA + g + 1) / N_HEADS)


def _bf16_terms(x, n=3):
    out = []
    for _ in range(n):
        hi = float(np.asarray(x, dtype=np.float32).astype(BF16))
        out.append(hi)
        x -= hi
    return out


def _head_slope(kvh, g):
    assert N_KV_HEADS == 2
    return jnp.where(kvh == 0, _alibi_slope(0, g), _alibi_slope(1, g))


def _inproj_kernel(x_ref, gmix_ref, w_ref, qn_ref, ksn_ref, kwn_ref,
                   u_ref, q_ref, cmp_ref, slc_ref, win_ref, gate_ref):
    xn = _rms(x_ref[...], gmix_ref[...]).astype(BF16)

    def proj(c0, c1):
        return _bdot(xn, w_ref[:, c0:c1])

    u_ref[...] = proj(0, SSM_WIDTH)
    q = proj(SSM_WIDTH, D_MODEL)
    for h in range(N_HEADS):
        sl = slice(h * HEAD_DIM, (h + 1) * HEAD_DIM)
        q_ref[:, sl] = (_rms(q[:, sl], qn_ref[...]) * QSCALE).astype(BF16)
    c0 = D_MODEL
    tm = x_ref.shape[0]
    for ref, nref in ((cmp_ref, None), (slc_ref, ksn_ref), (win_ref, kwn_ref)):
        k = proj(c0, c0 + KV_WIDTH)
        v = proj(c0 + KV_WIDTH, c0 + 2 * KV_WIDTH)
        c0 += 2 * KV_WIDTH
        for kvh in range(N_KV_HEADS):
            src = slice(kvh * HEAD_DIM, (kvh + 1) * HEAD_DIM)
            ref[pl.ds(2 * kvh, tm, stride=KV_STREAMS), :] = (k[:, src] if nref is None
                                                             else _rms(k[:, src], nref[...]))
            ref[pl.ds(2 * kvh + 1, tm, stride=KV_STREAMS), :] = v[:, src]
    gate_ref[...] = jax.nn.sigmoid(proj(c0, c0 + N_GATES))


def _inproj(x, prm, tm):
    m = x.shape[0]
    row = lambda n: pl.BlockSpec((tm, n), lambda i: (i, 0))
    kv_rows = pl.BlockSpec((tm * KV_STREAMS, HEAD_DIM), lambda i: (i, 0))
    kv_shape = jax.ShapeDtypeStruct((m * KV_STREAMS, HEAD_DIM), F32)
    return pl.pallas_call(
        _inproj_kernel,
        grid=(m // tm,),
        in_specs=[row(D_MODEL), _resident((1, D_MODEL)), _resident_layer((D_MODEL, IN_WIDTH), prm["layer"]),
                  _resident((1, HEAD_DIM)), _resident((1, HEAD_DIM)), _resident((1, HEAD_DIM))],
        out_specs=[row(SSM_WIDTH), row(ATTN_WIDTH), kv_rows, kv_rows, kv_rows, row(N_GATES)],
        out_shape=[jax.ShapeDtypeStruct((m, SSM_WIDTH), F32), jax.ShapeDtypeStruct((m, ATTN_WIDTH), BF16),
                   kv_shape, kv_shape, kv_shape, jax.ShapeDtypeStruct((m, N_GATES), F32)],
        compiler_params=_cparams(("parallel",), 48),
        name="inproj",
    )(x, prm["norm_mix"], prm["w_in"], prm["q_norm"], prm["k_norm_slc"], prm["k_norm_win"])


def _s5_group_body(gi, u_ref, h0_ref, lrow_ref, lcol_ref, bt_ref, bn_ref, ct_ref, d_ref,
                   y_ref, hl_ref, tm_ref, x_ref, hin_ref, *, lp, lreal, nb, nc):
    k = lp * SSM_CH
    n = SSM_STATE

    def lam_bar(a_re, a_im, log_dt):
        dt = jnp.exp(log_dt)
        e = jnp.exp(a_re * dt)
        return e * jnp.cos(a_im * dt), e * jnp.sin(a_im * dt)

    def zoh_coef(l_re, l_im, a_re, a_im):
        den = a_re * a_re + a_im * a_im
        x_re = l_re - 1.0
        return (x_re * a_re + l_im * a_im) / den, (l_im * a_re - x_re * a_im) / den

    lc = lcol_ref[gi]
    a_re_c, a_im_c = lc[:, 0:1], lc[:, 1:2]
    l_re_c, l_im_c = lam_bar(a_re_c, a_im_c, lc[:, 2:3])
    lr = lrow_ref[gi]
    a_re_r, a_im_r = lr[0:1], lr[1:2]
    l_re_r, l_im_r = lam_bar(a_re_r, a_im_r, lr[2:3])

    def pow_table(e):
        p_re = jnp.ones((n, k), F32)
        p_im = jnp.zeros((n, k), F32)
        s_re, s_im = l_re_c, l_im_c
        for j in range(max(lp.bit_length() - 1, 1)):
            bit = ((e >> j) & 1) == 1
            m_re = jnp.where(bit, s_re, 1.0)
            m_im = jnp.where(bit, s_im, 0.0)
            p_re, p_im = p_re * m_re - p_im * m_im, p_re * m_im + p_im * m_re
            s_re, s_im = s_re * s_re - s_im * s_im, 2.0 * s_re * s_im
        return p_re, p_im

    lane = lax.broadcasted_iota(jnp.int32, (1, k), 1)
    tau = lane >> (SSM_CH.bit_length() - 1)
    ch_onehot = jnp.where((lax.broadcasted_iota(jnp.int32, (SSM_CH, k), 1) & (SSM_CH - 1))
                          == lax.broadcasted_iota(jnp.int32, (SSM_CH, k), 0), 1.0, 0.0).astype(BF16)

    p_re, p_im = pow_table(tau)
    c_re = _dot_small_int(ct_ref[gi, 0], ch_onehot)
    c_im = _dot_small_int(ct_ref[gi, 1], ch_onehot)
    g_re = c_re * p_re - c_im * p_im
    g_im = c_re * p_im + c_im * p_re

    cf_re_r, cf_im_r = zoh_coef(l_re_r, l_im_r, a_re_r, a_im_r)
    bbt_re = cf_re_r * bt_ref[gi, 0] - cf_im_r * bt_ref[gi, 1]
    bbt_im = cf_re_r * bt_ref[gi, 1] + cf_im_r * bt_ref[gi, 0]
    a = _dot_f32(bbt_re, g_re) - _dot_f32(bbt_im, g_im)

    lane16 = lax.broadcasted_iota(jnp.int32, (SSM_CH, k), 1)
    for s in range(lp):
        blk = a if s == 0 else jnp.where(lane16 >= SSM_CH * s, pltpu.roll(a, SSM_CH * s, axis=1), 0.0)
        tm_ref[gi, SSM_CH * s:SSM_CH * (s + 1), :] = blk.astype(BF16)

    cf_re_c, cf_im_c = zoh_coef(l_re_c, l_im_c, a_re_c, a_im_c)
    bbn_re = cf_re_c * bn_ref[gi, 0] - cf_im_c * bn_ref[gi, 1]
    bbn_im = cf_re_c * bn_ref[gi, 1] + cf_im_c * bn_ref[gi, 0]
    bb_re = _dot_small_int(bbn_re, ch_onehot)
    bb_im = _dot_small_int(bbn_im, ch_onehot)
    rev = (lreal - 1) - tau
    q_re, q_im = pow_table(jnp.maximum(rev, 0))
    live = rev >= 0
    zeros_nk = jnp.zeros((n, k), BF16)
    wx_re = jnp.concatenate([jnp.where(live, q_re * bb_re - q_im * bb_im, 0.0).astype(BF16), zeros_nk], axis=0)
    wx_im = jnp.concatenate([jnp.where(live, q_re * bb_im + q_im * bb_re, 0.0).astype(BF16), zeros_nk], axis=0)

    u = u_ref[gi]
    ub = u.astype(BF16)
    x_ref[gi, 0] = _bdot_nt(ub, wx_re)
    x_ref[gi, 1] = _bdot_nt(ub, wx_im)

    ll_re, ll_im = l_re_r, l_im_r
    for _ in range(lreal.bit_length() - 1):
        ll_re, ll_im = ll_re * ll_re - ll_im * ll_im, 2.0 * ll_re * ll_im
    ll_re = jnp.concatenate([ll_re, jnp.zeros_like(ll_re)], axis=1)
    ll_im = jnp.concatenate([ll_im, jnp.zeros_like(ll_im)], axis=1)
    lower = lax.broadcasted_iota(jnp.int32, (1, 2 * n), 1) < n
    h0 = h0_ref[gi]
    h_re = jnp.where(lower, h0, 0.0)
    h_im = jnp.where(lower, pltpu.roll(h0, n, axis=1), 0.0)
    for c in range(nc):
        rows = pl.ds(c, nb, stride=nc) if nc > 1 else slice(0, nb)
        hin_ref[gi, 0, rows, :] = h_re
        hin_ref[gi, 1, rows, :] = h_im
        h_re, h_im = (ll_re * h_re - ll_im * h_im + x_ref[gi, 0, rows, :],
                      ll_re * h_im + ll_im * h_re + x_ref[gi, 1, rows, :])
    hl_ref[gi] = jnp.where(lower, h_re, pltpu.roll(h_im, n, axis=1))

    g1_re = jnp.concatenate([(g_re * l_re_c - g_im * l_im_c).astype(BF16), zeros_nk], axis=0)
    g1_im = jnp.concatenate([(g_re * l_im_c + g_im * l_re_c).astype(BF16), zeros_nk], axis=0)
    y_state = _bdot(hin_ref[gi, 0].astype(BF16), g1_re) - _bdot(hin_ref[gi, 1].astype(BF16), g1_im)

    y_ref[gi] = _bdot(ub, tm_ref[gi]) + y_state + u * d_ref[gi]


def _s5_kernel(*refs, gb, **kw):
    for gi in range(gb):
        _s5_group_body(gi, *refs, **kw)


def _s5(u_g, h0_g, prm, lp, lreal, nb, nc):
    g, m, k = u_g.shape
    gb = S5_GROUPS_PER_STEP
    blk = lambda *s: pl.BlockSpec((gb,) + s, lambda i: (i,) + (0,) * len(s))
    return pl.pallas_call(
        functools.partial(_s5_kernel, gb=gb, lp=lp, lreal=lreal, nb=nb, nc=nc),
        grid=(g // gb,),
        in_specs=[blk(m, k), blk(nb, 2 * SSM_STATE), blk(3, SSM_STATE), blk(SSM_STATE, 3),
                  blk(2, SSM_CH, SSM_STATE), blk(2, SSM_STATE, SSM_CH), blk(2, SSM_STATE, SSM_CH), blk(1, k)],
        out_specs=[blk(m, k), blk(nb, 2 * SSM_STATE)],
        out_shape=[jax.ShapeDtypeStruct((g, m, k), F32), jax.ShapeDtypeStruct((g, nb, 2 * SSM_STATE), F32)],
        scratch_shapes=[pltpu.VMEM((gb, k, k), BF16), pltpu.VMEM((gb, 2, m, 2 * SSM_STATE), F32),
                        pltpu.VMEM((gb, 2, m, 2 * SSM_STATE), F32)],
        compiler_params=_cparams(("parallel",), 32),
        name="s5_group",
    )(u_g, h0_g, prm["lrow"], prm["lcol"], prm["bt"], prm["bn"], prm["ct"], prm["dtile"][lp])


def _s5_mixer(u, h0, prm, lp, lreal):
    b, t, _ = u.shape
    nc = t // lreal
    ug = u.reshape(b * nc, lreal, SSM_GROUPS, SSM_CH)
    if lp != lreal:
        ug = jnp.pad(ug, ((0, 0), (0, lp - lreal), (0, 0), (0, 0)))
    ug = ug.transpose(2, 0, 1, 3).reshape(SSM_GROUPS, b * nc, lp * SSM_CH)
    h0g = h0.transpose(1, 0, 3, 2).reshape(SSM_GROUPS, b, 2 * SSM_STATE)
    yg, hl = _s5(ug, h0g, prm, lp, lreal, b, nc)
    y = yg.reshape(SSM_GROUPS, b * nc, lp, SSM_CH)[:, :, :lreal].transpose(1, 2, 0, 3)
    h_last = hl.reshape(SSM_GROUPS, b, 2, SSM_STATE).transpose(1, 0, 3, 2)
    return y.reshape(b, t, SSM_WIDTH), h_last


def _cmp_proj_kernel(*refs, n_in):
    refs = refs[len(refs) - n_in - 3:]
    x_refs, (w1k_ref, w1v_ref, o_ref) = refs[:n_in], refs[n_in:]

    def rows_of(j, r):
        n = x_refs[0].shape[0] // (KV_STREAMS * CMP_STRIDE)
        parts = [x[pl.ds(KV_STREAMS * r + j, n, stride=KV_STREAMS * CMP_STRIDE), :] for x in x_refs]
        return parts[0] if n_in == 1 else jnp.concatenate(parts, axis=0)

    for j in range(KV_STREAMS):
        c = jnp.concatenate([rows_of(j, r) for r in range(CMP_STRIDE)], axis=1).astype(BF16)
        w = w1k_ref if j % 2 == 0 else w1v_ref
        o_ref[0, :, j * 2 * HEAD_DIM:(j + 1) * 2 * HEAD_DIM] = _bdot(c, w[...])


def _cmp_combine_kernel(h_ref, b1_ref, w2k_ref, w2v_ref, kn_ref, o_ref, *, nchunk):
    h = h_ref[0]
    keep = lax.broadcasted_iota(jnp.int32, (nchunk, 1), 0) < nchunk - 1
    for j in range(4):
        kvh, is_v = j // 2, j % 2
        lo = h[:, j * 2 * HEAD_DIM: j * 2 * HEAD_DIM + HEAD_DIM]
        hi = h[:, j * 2 * HEAD_DIM + HEAD_DIM:(j + 1) * 2 * HEAD_DIM]
        pre = lo + pltpu.roll(hi, nchunk - 1, axis=0) + b1_ref[is_v:is_v + 1, :]
        out = _bdot(_gelu(pre).astype(BF16), (w2v_ref if is_v else w2k_ref)[...])
        if not is_v:
            out = _rms(out, kn_ref[...])
        c0 = is_v * N_KV_HEADS * HEAD_DIM + kvh * HEAD_DIM
        o_ref[0, :, c0:c0 + HEAD_DIM] = jnp.where(keep, out, 0.0)


def _cmp_combine(h, prm):
    b, nchunk, w = h.shape
    return pl.pallas_call(
        functools.partial(_cmp_combine_kernel, nchunk=nchunk),
        grid=(b,),
        in_specs=[pl.BlockSpec((1, nchunk, w), lambda i: (i, 0, 0)), _resident((2, HEAD_DIM)),
                  _resident((HEAD_DIM, HEAD_DIM)), _resident((HEAD_DIM, HEAD_DIM)), _resident((1, HEAD_DIM))],
        out_specs=pl.BlockSpec((1, nchunk, 4 * HEAD_DIM), lambda i: (i, 0, 0)),
        out_shape=jax.ShapeDtypeStruct((b, nchunk, 4 * HEAD_DIM), F32),
        compiler_params=_cparams(("parallel",), 40),
        name="cmp_combine",
    )(h, prm["cmp_b1"], prm["cmp_w2_k"], prm["cmp_w2_v"], prm["k_norm_cmp"])


def _compress_prompt(kv_rows, b, prm):
    rows = kv_rows.shape[0] // b
    nchunk = rows // (KV_STREAMS * CMP_STRIDE)
    h = pl.pallas_call(
        functools.partial(_cmp_proj_kernel, n_in=1),
        grid=(b,),
        in_specs=[pl.BlockSpec((rows, HEAD_DIM), lambda i: (i, 0)),
                  _resident((CMP_STRIDE * HEAD_DIM, 2 * HEAD_DIM)), _resident((CMP_STRIDE * HEAD_DIM, 2 * HEAD_DIM))],
        out_specs=pl.BlockSpec((1, nchunk, 8 * HEAD_DIM), lambda i: (i, 0, 0)),
        out_shape=jax.ShapeDtypeStruct((b, nchunk, 8 * HEAD_DIM), F32),
        compiler_params=_cparams(("parallel",), 40),
        name="cmp_proj_prompt",
    )(kv_rows, prm["cmp_w1_k"], prm["cmp_w1_v"])
    return _cmp_combine(h, prm)


PAGES_PER_STEP = 16


def _rowwise_pages(cache):
    return cache.reshape(cache.shape[0], cache.shape[1], PAGE_SIZE * 2 * N_KV_HEADS, HEAD_DIM)


def _compress_sample(cache_cmp, layer, page_table, prm):
    bsz, n_pages = page_table.shape
    chunks_per_page = PAGE_SIZE // CMP_STRIDE
    pages = _rowwise_pages(cache_cmp)
    steps = n_pages // PAGES_PER_STEP

    def page_spec(p):
        return pl.BlockSpec((None, None, 4 * PAGE_SIZE, HEAD_DIM),
                            lambda b, s, pt: (layer, pt[b * n_pages + s * PAGES_PER_STEP + p], 0, 0))

    rows = PAGES_PER_STEP * chunks_per_page
    h = pl.pallas_call(
        functools.partial(_cmp_proj_kernel, n_in=PAGES_PER_STEP),
        grid_spec=pltpu.PrefetchScalarGridSpec(
            num_scalar_prefetch=1,
            grid=(bsz, steps),
            in_specs=[page_spec(p) for p in range(PAGES_PER_STEP)] + [
                pl.BlockSpec((CMP_STRIDE * HEAD_DIM, 2 * HEAD_DIM), lambda b, s, pt: (0, 0)),
                pl.BlockSpec((CMP_STRIDE * HEAD_DIM, 2 * HEAD_DIM), lambda b, s, pt: (0, 0))],
            out_specs=pl.BlockSpec((1, rows, 8 * HEAD_DIM), lambda b, s, pt: (b, s, 0)),
        ),
        out_shape=jax.ShapeDtypeStruct((bsz, n_pages * chunks_per_page, 8 * HEAD_DIM), F32),
        compiler_params=_cparams(("parallel", "parallel"), 40),
        name="cmp_proj_sample",
    )(page_table.reshape(-1), *([pages] * PAGES_PER_STEP), prm["cmp_w1_k"], prm["cmp_w1_v"])
    return _cmp_combine(h, prm)


def _cmp_select_kernel(q_ref, ck_ref, cv_ref, o_ref, rank_ref, *idx_ref, tq, nc, nbp, nblk, qpos0):
    kvh = pl.program_id(1)
    qpos = qpos0 + pl.program_id(2) * tq + lax.broadcasted_iota(jnp.int32, (tq, 1), 0)
    assert tq & (tq - 1) == 0
    start = lax.broadcasted_iota(jnp.int32, (1, nc), 1) * CMP_STRIDE
    ck = ck_ref[0].astype(BF16)
    cv = cv_ref[0].astype(BF16)
    q_rows = jnp.concatenate([q_ref[0, :, g * HEAD_DIM:(g + 1) * HEAD_DIM] for g in range(GQA)], axis=0)
    row = lax.broadcasted_iota(jnp.int32, (GQA * tq, 1), 0)
    qpos_rows = qpos0 + pl.program_id(2) * tq + (row & (tq - 1))
    slope_rows = jnp.zeros((GQA * tq, 1), F32)
    for g in range(GQA):
        slope_rows = jnp.where(row >= g * tq, _head_slope(kvh, g), slope_rows)
    bias = slope_rows * ((qpos_rows - start).astype(F32) - (CMP_LEN - 1) / 2)
    _, e, den = _softmax_parts(_bdot_nt(q_rows, ck) - bias, (start + (CMP_LEN - 1)) <= qpos_rows)
    p = e / jnp.maximum(den, F32_TINY)
    o_rows = _bdot(p.astype(BF16), cv)
    psum = jnp.zeros((tq, nc), F32)
    for g in range(GQA):
        o_ref[0, :, g * HEAD_DIM:(g + 1) * HEAD_DIM] = o_rows[g * tq:(g + 1) * tq]
        psum = psum + p[g * tq:(g + 1) * tq]

    per_shift = (SEL_LEN // CMP_STRIDE).bit_length() - 1

    def ranks(blocks_on_rows):
        nbs = -(-nblk // 8) * 8
        ov_shape = (nbs, nc) if blocks_on_rows else (nc, nbp)
        ci = lax.broadcasted_iota(jnp.int32, ov_shape, 1 if blocks_on_rows else 0)
        bj = lax.broadcasted_iota(jnp.int32, ov_shape, 0 if blocks_on_rows else 1)
        overlap = (jnp.where(ci >> per_shift == bj, 1.0, 0.0)
                   + jnp.where((ci + 1) >> per_shift == bj, 1.0, 0.0)).astype(BF16)
        if blocks_on_rows:
            score = sum(_bdot_nt(overlap, part) for part in _split3(psum))
            j = lax.broadcasted_iota(jnp.int32, (nbs, 1), 0)
            pos = qpos0 + pl.program_id(2) * tq + lax.broadcasted_iota(jnp.int32, (1, tq), 1)
        else:
            score = _dot_small_int(psum, overlap)
            j = lax.broadcasted_iota(jnp.int32, (1, nbp), 1)
            pos = qpos
        qb = pos >> SEL_SHIFT
        forced = (j == 0) | (j == qb) | (j == qb - 1)
        score = jnp.where(forced, FORCED_SCORE, jnp.where((j * SEL_LEN) <= pos, score, -1.0))
        score = jnp.where(j < nblk, score, -2.0)
        rank = jnp.zeros(score.shape, F32)
        for i in range(nblk):
            si = score[i:i + 1, :] if blocks_on_rows else score[:, i:i + 1]
            rank = rank + jnp.where(j > i, jnp.where(si >= score, 1.0, 0.0), jnp.where(si > score, 1.0, 0.0))
        return rank

    if idx_ref or qpos0 >= N_SEL * SEL_LEN:
        rank = ranks(False)
        rank_ref[0, 0] = rank
    else:
        crowded = qpos0 + (pl.program_id(2) + 1) * tq > N_SEL * SEL_LEN

        @pl.when(crowded)
        def _():
            rank_t = ranks(True)
            unused = jnp.full((nbp - rank_t.shape[0], tq), float(nblk), F32)
            rank_ref[0, 0] = jnp.concatenate([rank_t, unused], axis=0).T

        @pl.when(jnp.logical_not(crowded))
        def _():
            rank_ref[0, 0] = jnp.zeros((tq, nbp), F32)

    if idx_ref:
        jf = lax.broadcasted_iota(jnp.int32, (1, nbp), 1).astype(F32)
        lane = lax.broadcasted_iota(jnp.int32, (tq, V7X_LANES), 1)
        idx = jnp.zeros((tq, V7X_LANES), F32)
        for r in range(N_SEL):
            col = jnp.sum(jnp.where(rank == float(r), jf, 0.0), axis=-1, keepdims=True)
            idx = jnp.where(lane == r, col, idx)
        idx_ref[0][0, 0] = idx.astype(jnp.int32)


def _cmp_select(q, ckv, tq, nblk, qpos0, want_idx):
    b, t, _ = q.shape
    nc = ckv.shape[1]
    nbp = -(-nblk // V7X_LANES) * V7X_LANES
    hw = GQA * HEAD_DIM
    out_specs = [pl.BlockSpec((1, tq, hw), lambda bi, k, i: (bi, i, k)),
                 pl.BlockSpec((1, 1, tq, nbp), lambda bi, k, i: (bi, k, i, 0))]
    out_shape = [jax.ShapeDtypeStruct((b, t, ATTN_WIDTH), F32),
                 jax.ShapeDtypeStruct((b, N_KV_HEADS, t, nbp), F32)]
    if want_idx:
        out_specs.append(pl.BlockSpec((1, 1, tq, V7X_LANES), lambda bi, k, i: (bi, k, i, 0)))
        out_shape.append(jax.ShapeDtypeStruct((b, N_KV_HEADS, t, V7X_LANES), jnp.int32))
    return pl.pallas_call(
        functools.partial(_cmp_select_kernel, tq=tq, nc=nc, nbp=nbp, nblk=nblk, qpos0=qpos0),
        grid=(b, N_KV_HEADS, t // tq),
        in_specs=[pl.BlockSpec((1, tq, hw), lambda bi, k, i: (bi, i, k)),
                  pl.BlockSpec((1, nc, HEAD_DIM), lambda bi, k, i: (bi, 0, k)),
                  pl.BlockSpec((1, nc, HEAD_DIM), lambda bi, k, i: (bi, 0, N_KV_HEADS + k))],
        out_specs=out_specs,
        out_shape=out_shape,
        compiler_params=_cparams(("parallel", "parallel", "parallel"), 40),
        name="cmp_select",
    )(q, ckv, ckv)


ATT_TQ = 256
SEL_TK = 512
FILL_ROWS = 256
POS_RADIX = 256
RATE_TERMS = 3
MASK_BIAS = 2.0 ** 100


def _key_features(kidx, nblk):
    lane = lax.broadcasted_iota(jnp.int32, (1, V7X_LANES), 1)
    radix_shift = POS_RADIX.bit_length() - 1
    digits = jnp.where(((lane - nblk) & 1) == 0, (kidx >> radix_shift).astype(F32),
                       (kidx & (POS_RADIX - 1)).astype(F32))
    feats = jnp.where(lane < nblk + 2 * RATE_TERMS, digits, 0.0)
    if nblk:
        feats = jnp.where(lane < nblk, jnp.where((kidx >> SEL_SHIFT) == lane, 1.0, 0.0), feats)
    return feats


def _query_features(kvh, g, rank, nblk, rows):
    lane = lax.broadcasted_iota(jnp.int32, (rows, V7X_LANES), 1)
    feats = jnp.zeros((rows, V7X_LANES), F32)
    if nblk:
        feats = jnp.where((lane < nblk) & (rank >= float(N_SEL)), -MASK_BIAS, 0.0)
    for i, (c0, c1) in enumerate(zip(_bf16_terms(_alibi_slope(0, g), RATE_TERMS),
                                     _bf16_terms(_alibi_slope(1, g), RATE_TERMS))):
        c = jnp.where(kvh == 0, c0, c1)
        feats = jnp.where(lane == nblk + 2 * i, c * POS_RADIX, feats)
        feats = jnp.where(lane == nblk + 2 * i + 1, c, feats)
    return feats


def _fill_key_value_scratch(kv_ref, kvh, kf_ref, vo_ref, nblk):
    def chunk(c, carry):
        r0 = pl.multiple_of(c * FILL_ROWS, FILL_ROWS)
        rows = pl.ds(r0, FILL_ROWS)
        kidx = r0 + lax.broadcasted_iota(jnp.int32, (FILL_ROWS, 1), 0)
        k_rows = pl.ds(r0 * KV_STREAMS + 2 * kvh, FILL_ROWS, stride=KV_STREAMS)
        v_rows = pl.ds(r0 * KV_STREAMS + 2 * kvh + 1, FILL_ROWS, stride=KV_STREAMS)
        kf_ref[rows, :HEAD_DIM] = kv_ref[0, k_rows, :].astype(BF16)
        kf_ref[rows, HEAD_DIM:] = _key_features(kidx, nblk).astype(BF16)
        vo_ref[rows, :HEAD_DIM] = kv_ref[0, v_rows, :].astype(BF16)
        vo_ref[rows, HEAD_DIM:] = jnp.ones((FILL_ROWS, V7X_LANES), BF16)
        return carry

    lax.fori_loop(0, kv_ref.shape[1] // (KV_STREAMS * FILL_ROWS), chunk, 0)


def _attn_out(acc):
    return acc[:, :HEAD_DIM] / jnp.maximum(acc[:, HEAD_DIM:HEAD_DIM + 1], F32_TINY)


def _sel_prompt_kernel(q_ref, kv_ref, rank_ref, o_ref, kf_ref, vo_ref, qf_ref, m_ref, acc_ref, *, nblk):
    kvh, i = pl.program_id(1), pl.program_id(2)

    @pl.when(i == 0)
    def _():
        _fill_key_value_scratch(kv_ref, kvh, kf_ref, vo_ref, nblk)

    q0 = i * ATT_TQ
    qpos = q0 + (lax.broadcasted_iota(jnp.int32, (GQA * ATT_TQ, 1), 0) & (ATT_TQ - 1))
    rank = rank_ref[0, 0]
    for g in range(GQA):
        rows = slice(g * ATT_TQ, (g + 1) * ATT_TQ)
        qf_ref[rows, :HEAD_DIM] = q_ref[0, :, g * HEAD_DIM:(g + 1) * HEAD_DIM]
        qf_ref[rows, HEAD_DIM:] = _query_features(kvh, g, rank, nblk, ATT_TQ).astype(BF16)
    m_ref[...] = jnp.full(m_ref.shape, NEG_INF, F32)
    acc_ref[...] = jnp.zeros(acc_ref.shape, F32)

    def tile(jt, carry):
        k0 = pl.multiple_of(jt * SEL_TK, SEL_TK)
        causal = (k0 + lax.broadcasted_iota(jnp.int32, (1, SEL_TK), 1)) <= qpos
        s = jnp.where(causal, _bdot_nt(qf_ref[...], kf_ref[pl.ds(k0, SEL_TK), :]), NEG_INF)
        _flash_step(m_ref, acc_ref, 0, s, vo_ref[pl.ds(k0, SEL_TK), :])
        return carry

    lax.fori_loop(0, (q0 + ATT_TQ + SEL_TK - 1) // SEL_TK, tile, 0)
    out = _attn_out(acc_ref[0])
    for g in range(GQA):
        o_ref[0, :, g * HEAD_DIM:(g + 1) * HEAD_DIM] = out[g * ATT_TQ:(g + 1) * ATT_TQ]


def _sel_prompt(q, kv_slc, rank, nblk):
    b, t, _ = q.shape
    nbp = rank.shape[-1]
    hw = GQA * HEAD_DIM
    assert nbp == V7X_LANES and nblk + 2 * RATE_TERMS <= V7X_LANES and t <= POS_RADIX * POS_RADIX
    return pl.pallas_call(
        functools.partial(_sel_prompt_kernel, nblk=nblk),
        grid=(b, N_KV_HEADS, t // ATT_TQ),
        in_specs=[pl.BlockSpec((1, ATT_TQ, hw), lambda bi, k, i: (bi, i, k)),
                  pl.BlockSpec((1, t * KV_STREAMS, HEAD_DIM), lambda bi, k, i: (bi, 0, 0)),
                  pl.BlockSpec((1, 1, ATT_TQ, nbp), lambda bi, k, i: (bi, k, i, 0))],
        out_specs=pl.BlockSpec((1, ATT_TQ, hw), lambda bi, k, i: (bi, i, k)),
        out_shape=jax.ShapeDtypeStruct((b, t, ATTN_WIDTH), F32),
        scratch_shapes=[pltpu.VMEM((t, 2 * HEAD_DIM), BF16), pltpu.VMEM((t, 2 * HEAD_DIM), BF16),
                        pltpu.VMEM((GQA * ATT_TQ, 2 * HEAD_DIM), BF16),
                        pltpu.VMEM((1, GQA * ATT_TQ, 1), F32), pltpu.VMEM((1, GQA * ATT_TQ, 2 * HEAD_DIM), F32)],
        compiler_params=_cparams(("parallel", "parallel", "arbitrary"), 32),
        name="sel_prompt",
    )(q, kv_slc, rank)


def _win_prompt_kernel(q_ref, kv_ref, o_ref, kf_ref, vo_ref):
    kvh, i = pl.program_id(1), pl.program_id(2)

    @pl.when(i == 0)
    def _():
        _fill_key_value_scratch(kv_ref, kvh, kf_ref, vo_ref, 0)

    span = WINDOW + WIN_TQ
    qpos = i * WIN_TQ + (lax.broadcasted_iota(jnp.int32, (GQA * WIN_TQ, 1), 0) & (WIN_TQ - 1))
    k0 = pl.multiple_of(jnp.maximum(i * WIN_TQ - WINDOW, 0), WIN_TQ)
    d = qpos - (k0 + lax.broadcasted_iota(jnp.int32, (1, span), 1))
    qf = jnp.concatenate(
        [jnp.concatenate([q_ref[0, :, g * HEAD_DIM:(g + 1) * HEAD_DIM],
                          _query_features(kvh, g, None, 0, WIN_TQ).astype(BF16)], axis=1) for g in range(GQA)],
        axis=0)
    s = jnp.where((d >= 0) & (d < WINDOW), _bdot_nt(qf, kf_ref[pl.ds(k0, span), :]), NEG_INF)
    e = jnp.exp2(s - jnp.max(s, axis=-1, keepdims=True))
    out = _attn_out(_bdot(e.astype(BF16), vo_ref[pl.ds(k0, span), :]))
    for g in range(GQA):
        o_ref[0, :, g * HEAD_DIM:(g + 1) * HEAD_DIM] = out[g * WIN_TQ:(g + 1) * WIN_TQ]


def _win_prompt(q, kv_win):
    b, t, _ = q.shape
    hw = GQA * HEAD_DIM
    assert t <= POS_RADIX * POS_RADIX
    return pl.pallas_call(
        _win_prompt_kernel,
        grid=(b, N_KV_HEADS, t // WIN_TQ),
        in_specs=[pl.BlockSpec((1, WIN_TQ, hw), lambda bi, k, i: (bi, i, k)),
                  pl.BlockSpec((1, t * KV_STREAMS, HEAD_DIM), lambda bi, k, i: (bi, 0, 0))],
        out_specs=pl.BlockSpec((1, WIN_TQ, hw), lambda bi, k, i: (bi, i, k)),
        out_shape=jax.ShapeDtypeStruct((b, t, ATTN_WIDTH), F32),
        scratch_shapes=[pltpu.VMEM((t, 2 * HEAD_DIM), BF16), pltpu.VMEM((t, 2 * HEAD_DIM), BF16)],
        compiler_params=_cparams(("parallel", "parallel", "arbitrary"), 32),
        name="win_prompt",
    )(q, kv_win)


def _slope_col(kvh):
    g = lax.broadcasted_iota(jnp.int32, (GQA, 1), 0)
    col = jnp.zeros((GQA, 1), F32)
    for i in range(GQA):
        col = jnp.where(g == i, _alibi_slope(kvh, i), col)
    return col


def _stream_rows(ref, lead, stream, n):
    return ref[lead + (pl.ds(stream, n, stride=2 * N_KV_HEADS), slice(None))]


def _sel_sample_kernel(idx_ref, pt_ref, q_ref, *refs, t_len):
    past, new_ref, o_ref = refs[:N_KV_HEADS * N_SEL], refs[N_KV_HEADS * N_SEL], refs[N_KV_HEADS * N_SEL + 1]
    b, t = pl.program_id(0), pl.program_id(1)
    past_blocks = PAST_LEN // SEL_LEN
    lane = lax.broadcasted_iota(jnp.int32, (1, N_SEL * SEL_LEN), 1)
    for kvh in range(N_KV_HEADS):
        ks, vs = [], []
        base = jnp.zeros((1, N_SEL * SEL_LEN), jnp.int32)
        for r in range(N_SEL):
            blk = idx_ref[((b * N_KV_HEADS + kvh) * t_len + t) * N_SEL + r]
            is_new = blk >= past_blocks
            ref = past[kvh * N_SEL + r]
            ks.append(jnp.where(is_new, _stream_rows(new_ref, (0,), 2 * kvh, SEL_LEN),
                                _stream_rows(ref, (), 2 * kvh, SEL_LEN)).astype(BF16))
            vs.append(jnp.where(is_new, _stream_rows(new_ref, (0,), 2 * kvh + 1, SEL_LEN),
                                _stream_rows(ref, (), 2 * kvh + 1, SEL_LEN)).astype(BF16))
            base = jnp.where((lane >> SEL_SHIFT) == r, blk * SEL_LEN, base)
        d = (PAST_LEN + t) - (base + (lane & (SEL_LEN - 1)))
        rows = slice(kvh * GQA, (kvh + 1) * GQA)
        s = _bdot_nt(q_ref[0, 0, rows, :], jnp.concatenate(ks, axis=0)) - _slope_col(kvh) * d.astype(F32)
        _, e, z = _softmax_parts(s, d >= 0)
        o_ref[0, 0, rows, :] = _bdot(e.astype(BF16), jnp.concatenate(vs, axis=0)) / jnp.maximum(z, F32_TINY)


def _sel_sample(q_h, cache_slc, layer, page_table, idx, kv_new_rows):
    bsz, t_len = q_h.shape[:2]
    n_pages = page_table.shape[1]
    per_page = PAGE_SIZE // SEL_LEN
    past_blocks = PAST_LEN // SEL_LEN
    pages = _rowwise_pages(cache_slc)
    blk_rows = SEL_LEN * 2 * N_KV_HEADS

    def past_spec(kvh, r):
        def imap(b, t, idx_ref, pt_ref):
            blk = jnp.minimum(idx_ref[((b * N_KV_HEADS + kvh) * t_len + t) * N_SEL + r], past_blocks - 1)
            return (layer, pt_ref[b * n_pages + blk // per_page], blk % per_page, 0)
        return pl.BlockSpec((None, None, blk_rows, HEAD_DIM), imap)

    qspec = pl.BlockSpec((1, 1, N_HEADS, HEAD_DIM), lambda b, t, i_, p_: (b, t, 0, 0))
    return pl.pallas_call(
        functools.partial(_sel_sample_kernel, t_len=t_len),
        grid_spec=pltpu.PrefetchScalarGridSpec(
            num_scalar_prefetch=2,
            grid=(bsz, t_len),
            in_specs=[qspec] + [past_spec(kvh, r) for kvh in range(N_KV_HEADS) for r in range(N_SEL)]
            + [pl.BlockSpec((1, blk_rows, HEAD_DIM), lambda b, t, i_, p_: (b, 0, 0))],
            out_specs=qspec,
        ),
        out_shape=jax.ShapeDtypeStruct(q_h.shape, F32),
        compiler_params=_cparams(("parallel", "parallel"), 32),
        name="sel_sample",
    )(idx.reshape(-1), page_table.reshape(-1), q_h, *([pages] * (N_KV_HEADS * N_SEL)), kv_new_rows)


def _win_sample_kernel(q_ref, past_ref, new_ref, o_ref, *, t_len, t_pad, wbuf):
    t = lax.broadcasted_iota(jnp.int32, (t_len, 1), 0)
    d_past = wbuf + t - lax.broadcasted_iota(jnp.int32, (1, wbuf), 1)
    row_new = lax.broadcasted_iota(jnp.int32, (1, t_pad), 1)
    d_new = t - row_new
    valid_past = (d_past >= 0) & (d_past < WINDOW)
    valid_new = (d_new >= 0) & (d_new < WINDOW) & (row_new < t_len)
    for kvh in range(N_KV_HEADS):
        kp = _stream_rows(past_ref, (0,), 2 * kvh, wbuf).astype(BF16)
        vp = _stream_rows(past_ref, (0,), 2 * kvh + 1, wbuf).astype(BF16)
        kn = _stream_rows(new_ref, (0,), 2 * kvh, t_pad).astype(BF16)
        vn = _stream_rows(new_ref, (0,), 2 * kvh + 1, t_pad).astype(BF16)
        for g in range(GQA):
            q = q_ref[0, kvh, g]
            slope = _alibi_slope(kvh, g)
            m1, e1, z1 = _softmax_parts(_bdot_nt(q, kp) - slope * d_past.astype(F32), valid_past)
            m2, e2, z2 = _softmax_parts(_bdot_nt(q, kn) - slope * d_new.astype(F32), valid_new)
            m = jnp.maximum(m1, m2)
            a1 = jnp.where(z1 > 0.0, jnp.exp2(m1 - m), 0.0)
            a2 = jnp.where(z2 > 0.0, jnp.exp2(m2 - m), 0.0)
            num = a1 * _bdot(e1.astype(BF16), vp) + a2 * _bdot(e2.astype(BF16), vn)
            o_ref[0, kvh, g] = num / jnp.maximum(a1 * z1 + a2 * z2, F32_TINY)


def _win_sample(q_g, win_rows, new_rows):
    bsz, _, _, t_len, _ = q_g.shape
    streams = 2 * N_KV_HEADS
    wbuf, t_pad = win_rows.shape[1] // streams, new_rows.shape[1] // streams
    qspec = pl.BlockSpec((1, N_KV_HEADS, GQA, t_len, HEAD_DIM), lambda b: (b, 0, 0, 0, 0))
    return pl.pallas_call(
        functools.partial(_win_sample_kernel, t_len=t_len, t_pad=t_pad, wbuf=wbuf),
        grid=(bsz,),
        in_specs=[qspec, pl.BlockSpec((1, wbuf * streams, HEAD_DIM), lambda b: (b, 0, 0)),
                  pl.BlockSpec((1, t_pad * streams, HEAD_DIM), lambda b: (b, 0, 0))],
        out_specs=qspec,
        out_shape=jax.ShapeDtypeStruct(q_g.shape, F32),
        compiler_params=_cparams(("parallel",), 16),
        name="win_sample",
    )(q_g, win_rows, new_rows)


def _mix_kernel(x_ref, y_ref, oc_ref, os_ref, ow_ref, gate_ref, wglu_ref, gs_ref, ga_ref, wout_ref, o_ref):
    z = _gelu(y_ref[...])
    z = z * jax.nn.sigmoid(_bdot(z.astype(BF16), wglu_ref[...]))
    gate = gate_ref[...]
    cols = []
    for h in range(N_HEADS):
        sl = slice(h * HEAD_DIM, (h + 1) * HEAD_DIM)
        cols.append(gate[:, 3 * h:3 * h + 1] * oc_ref[:, sl] + gate[:, 3 * h + 1:3 * h + 2] * os_ref[:, sl]
                    + gate[:, 3 * h + 2:3 * h + 3] * ow_ref[:, sl])
    attn = jnp.concatenate(cols, axis=1)
    hs = _rms(z, gs_ref[...]).astype(BF16)
    ha = _rms(attn, ga_ref[...]).astype(BF16)
    o_ref[...] = x_ref[...] + (_bdot(hs, wout_ref[:SSM_WIDTH, :]) + _bdot(ha, wout_ref[SSM_WIDTH:, :]))


def _mix(x, y, oc, osl, ow, gates, prm, tm):
    m = x.shape[0]
    row = lambda n: pl.BlockSpec((tm, n), lambda i: (i, 0))
    return pl.pallas_call(
        _mix_kernel,
        grid=(m // tm,),
        in_specs=[row(D_MODEL), row(SSM_WIDTH), row(ATTN_WIDTH), row(ATTN_WIDTH), row(ATTN_WIDTH), row(N_GATES),
                  _resident_layer((SSM_WIDTH, SSM_WIDTH), prm["layer"]), _resident((1, SSM_WIDTH)),
                  _resident((1, ATTN_WIDTH)), _resident_layer((D_MODEL, D_MODEL), prm["layer"])],
        out_specs=row(D_MODEL),
        out_shape=jax.ShapeDtypeStruct((m, D_MODEL), F32),
        compiler_params=_cparams(("parallel",), 48),
        name="mix_outproj",
    )(x, y, oc, osl, ow, gates, prm["ssm_w_glu"], prm["norm_ssm_out"], prm["norm_attn_out"], prm["w_out"])


FFN_TF = 512


def _ffn_kernel(x_ref, g_ref, wg_ref, wu_ref, wd_ref, o_ref, xn_ref, acc_ref):
    j = pl.program_id(1)

    @pl.when(j == 0)
    def _():
        xn_ref[...] = _rms(x_ref[...], g_ref[...]).astype(BF16)
        acc_ref[...] = jnp.zeros(acc_ref.shape, F32)

    xn = xn_ref[...]
    a = _bdot(xn, wg_ref[...])
    h = (a * jax.nn.sigmoid(a)) * _bdot(xn, wu_ref[...])
    acc_ref[...] += _bdot(h.astype(BF16), wd_ref[...])

    @pl.when(j == pl.num_programs(1) - 1)
    def _():
        o_ref[...] = x_ref[...] + acc_ref[...]


def _ffn(x, prm, tm):
    m = x.shape[0]
    layer = prm["layer"]
    return pl.pallas_call(
        _ffn_kernel,
        grid=(m // tm, D_FF // FFN_TF),
        in_specs=[pl.BlockSpec((tm, D_MODEL), lambda i, j: (i, 0)),
                  pl.BlockSpec((1, D_MODEL), lambda i, j: (0, 0)),
                  pl.BlockSpec((None, D_MODEL, FFN_TF), lambda i, j: (layer, 0, j)),
                  pl.BlockSpec((None, D_MODEL, FFN_TF), lambda i, j: (layer, 0, j)),
                  pl.BlockSpec((None, FFN_TF, D_MODEL), lambda i, j: (layer, j, 0))],
        out_specs=pl.BlockSpec((tm, D_MODEL), lambda i, j: (i, 0)),
        out_shape=jax.ShapeDtypeStruct((m, D_MODEL), F32),
        scratch_shapes=[pltpu.VMEM((tm, D_MODEL), BF16), pltpu.VMEM((tm, D_MODEL), F32)],
        compiler_params=_cparams(("parallel", "arbitrary"), 48),
        name="ffn",
    )(x, prm["norm_ffn"], prm["w_ffn_gate"], prm["w_ffn_up"], prm["w_ffn_down"])


S5_CHUNK = 64
S5_GROUPS_PER_STEP = 2
SAMPLE_S5_PAD = 8


BIG_WEIGHTS = ("w_in", "ssm_w_glu", "w_out", "w_ffn_gate", "w_ffn_up", "w_ffn_down")


def _layer_params(l, w, big):
    def w1cat(w1):
        half = CMP_STRIDE * HEAD_DIM
        return jnp.concatenate([w1[:half], w1[half:]], axis=1).astype(BF16)

    row = lambda v: v.reshape(1, -1)
    a_re, a_im = w["ssm_a_re"][l], w["ssm_a_im"][l]
    log_dt = jnp.broadcast_to(w["ssm_log_dt"][l][:, None], a_re.shape)
    lrow = jnp.stack([a_re, a_im, log_dt], axis=1)
    d = w["ssm_d"][l]
    return {
        "layer": l, **big, "norm_mix": row(w["norm_mix"][l]),
        "q_norm": row(w["q_norm"][l]), "k_norm_slc": row(w["k_norm_slc"][l]),
        "k_norm_win": row(w["k_norm_win"][l]), "k_norm_cmp": row(w["k_norm_cmp"][l]),
        "lrow": lrow, "lcol": lrow.transpose(0, 2, 1),
        "bt": jnp.stack([w["ssm_b_re"][l], w["ssm_b_im"][l]], axis=1).transpose(0, 1, 3, 2),
        "bn": jnp.stack([w["ssm_b_re"][l], w["ssm_b_im"][l]], axis=1),
        "ct": jnp.stack([w["ssm_c_re"][l], w["ssm_c_im"][l]], axis=1).transpose(0, 1, 3, 2),
        "dtile": {lp: jnp.tile(d, (1, lp))[:, None, :] for lp in (S5_CHUNK, SAMPLE_S5_PAD)},
        "cmp_w1_k": w1cat(w["cmp_w1_k"][l]), "cmp_w1_v": w1cat(w["cmp_w1_v"][l]),
        "cmp_b1": jnp.stack([w["cmp_b1_k"][l], w["cmp_b1_v"][l]]),
        "cmp_w2_k": w["cmp_w2_k"][l].astype(BF16), "cmp_w2_v": w["cmp_w2_v"][l].astype(BF16),
        "norm_ssm_out": row(w["norm_ssm_out"][l]), "norm_attn_out": row(w["norm_attn_out"][l]),
        "norm_ffn": row(w["norm_ffn"][l]),
    }


def _kv_out(kv, b, t):
    return kv.reshape(b, t, N_KV_HEADS, 2, HEAD_DIM)


PROMPT_TM = 256
MIX_TM = 256
FFN_TM = 512


def _prompt_layer(x, prm):
    b, t, _ = x.shape
    m = b * t
    x2 = x.reshape(m, D_MODEL)
    u, q, kv_cmp, kv_slc, kv_win, gates = _inproj(x2, prm, PROMPT_TM)
    h0 = jnp.zeros((b, SSM_GROUPS, SSM_STATE, 2), F32)
    y, h_last = _s5_mixer(u.reshape(b, t, SSM_WIDTH), h0, prm, S5_CHUNK, S5_CHUNK)
    q3 = q.reshape(b, t, ATTN_WIDTH)
    ckv = _compress_prompt(kv_cmp, b, prm)
    o_cmp, rank = _cmp_select(q3, ckv, 256, t // SEL_LEN, 0, False)
    o_slc = _sel_prompt(q3, kv_slc.reshape(b, t * KV_STREAMS, HEAD_DIM), rank, t // SEL_LEN)
    o_win = _win_prompt(q3, kv_win.reshape(b, t * KV_STREAMS, HEAD_DIM))
    x2 = _mix(x2, y.reshape(m, SSM_WIDTH), o_cmp.reshape(m, -1), o_slc.reshape(m, -1), o_win.reshape(m, -1),
              gates, prm, MIX_TM)
    x2 = _ffn(x2, prm, FFN_TM)
    wbuf = min(WINDOW, PAST_LEN)
    kv_win5 = _kv_out(kv_win, b, t)
    win_state = kv_win5[:, t - wbuf:] if t >= wbuf else jnp.pad(kv_win5, ((0, 0), (wbuf - t, 0)) + ((0, 0),) * 3)
    return (x2.reshape(b, t, D_MODEL), _kv_out(kv_cmp, b, t), _kv_out(kv_slc, b, t), win_state, h_last)


def _sample_layer(x, prm, cache_cmp, cache_slc, layer, page_table, win_buf, h0):
    b, t, _ = x.shape
    m = b * t
    assert (PAST_LEN + t) // CMP_STRIDE == PAST_LEN // CMP_STRIDE and PAST_LEN % PAGE_SIZE == 0
    x2 = x.reshape(m, D_MODEL)
    u, q, kv_cmp, kv_slc, kv_win, gates = _inproj(x2, prm, m)
    y, h_last = _s5_mixer(u.reshape(b, t, SSM_WIDTH), h0, prm, SAMPLE_S5_PAD, t)
    q3 = q.reshape(b, t, ATTN_WIDTH)
    ckv = _compress_sample(cache_cmp, layer, page_table, prm)
    n_blocks = -(-(PAST_LEN + t) // SEL_LEN)
    o_cmp, _, idx = _cmp_select(q3, ckv, t, n_blocks, PAST_LEN, True)

    def padded_rows(kv_rows, t_pad):
        return jnp.pad(kv_rows.reshape(b, t * KV_STREAMS, HEAD_DIM), ((0, 0), (0, (t_pad - t) * KV_STREAMS), (0, 0)))

    o_slc = _sel_sample(q3.reshape(b, t, N_HEADS, HEAD_DIM), cache_slc, layer, page_table, idx[..., :N_SEL],
                        padded_rows(kv_slc, SEL_LEN)).reshape(m, ATTN_WIDTH)
    wbuf = win_buf.shape[1]
    q5 = q3.reshape(b, t, N_KV_HEADS, GQA, HEAD_DIM)
    o_win = _win_sample(q5.transpose(0, 2, 3, 1, 4), win_buf.reshape(b, wbuf * KV_STREAMS, HEAD_DIM),
                        padded_rows(kv_win, 8))
    o_win = o_win.transpose(0, 3, 1, 2, 4).reshape(m, ATTN_WIDTH)
    x2 = _mix(x2, y.reshape(m, SSM_WIDTH), o_cmp.reshape(m, -1), o_slc, o_win, gates, prm, m)
    x2 = _ffn(x2, prm, m)
    win_state = jnp.concatenate([win_buf, _kv_out(kv_win, b, t)], axis=1)[:, -wbuf:]
    return (x2.reshape(b, t, D_MODEL), _kv_out(kv_cmp, b, t), _kv_out(kv_slc, b, t), win_state, h_last)


def kernel(x_prompt, x_sample, cache_cmp_kv, cache_slc_kv, state_win_kv, state_ssm, page_table, norm_mix, w_in, ssm_a_re, ssm_a_im, ssm_log_dt, ssm_b_re, ssm_b_im, ssm_c_re, ssm_c_im, ssm_d, ssm_w_glu, q_norm, k_norm_cmp, k_norm_slc, k_norm_win, cmp_w1_k, cmp_b1_k, cmp_w2_k, cmp_w1_v, cmp_b1_v, cmp_w2_v, norm_ssm_out, norm_attn_out, w_out, norm_ffn, w_ffn_gate, w_ffn_up, w_ffn_down):
    w = dict(norm_mix=norm_mix, w_in=w_in, ssm_a_re=ssm_a_re, ssm_a_im=ssm_a_im, ssm_log_dt=ssm_log_dt,
             ssm_b_re=ssm_b_re, ssm_b_im=ssm_b_im, ssm_c_re=ssm_c_re, ssm_c_im=ssm_c_im, ssm_d=ssm_d,
             ssm_w_glu=ssm_w_glu, q_norm=q_norm, k_norm_cmp=k_norm_cmp, k_norm_slc=k_norm_slc,
             k_norm_win=k_norm_win, cmp_w1_k=cmp_w1_k, cmp_b1_k=cmp_b1_k, cmp_w2_k=cmp_w2_k,
             cmp_w1_v=cmp_w1_v, cmp_b1_v=cmp_b1_v, cmp_w2_v=cmp_w2_v, norm_ssm_out=norm_ssm_out,
             norm_attn_out=norm_attn_out, w_out=w_out, norm_ffn=norm_ffn, w_ffn_gate=w_ffn_gate,
             w_ffn_up=w_ffn_up, w_ffn_down=w_ffn_down)
    big = {name: w[name].astype(BF16) for name in BIG_WEIGHTS}
    y_p, y_s = x_prompt, x_sample
    outs_p, outs_s = [], []
    for l in range(DEPTH):
        prm = _layer_params(l, w, big)
        y_p, *rest = _prompt_layer(y_p, prm)
        outs_p.append(rest)
        y_s, *rest = _sample_layer(y_s, prm, cache_cmp_kv, cache_slc_kv, l, page_table,
                                   state_win_kv[l], state_ssm[l])
        outs_s.append(rest)
    stack = lambda outs, i: jnp.stack([o[i] for o in outs])
    return (y_p, y_s, stack(outs_p, 0), stack(outs_p, 1), stack(outs_p, 2), stack(outs_p, 3),
            stack(outs_s, 0), stack(outs_s, 1), stack(outs_s, 2), stack(outs_s, 3))
```

```python
import functools
import math

import jax
import jax.numpy as jnp
import numpy as np
from jax import lax
from jax.experimental import pallas as pl
from jax.experimental.pallas import tpu as pltpu

F32 = jnp.float32
BF16 = jnp.bfloat16
S5_IO_DTYPE = BF16

D_MODEL = 2048
DEPTH = 2
PAST_LEN = 16384
PAGE_SIZE = 128
SSM_WIDTH = 1024
ATTN_WIDTH = 1024
SSM_CH = 16
SSM_GROUPS = 64
SSM_STATE = 64
HEAD_DIM = 128
N_HEADS = 8
N_KV_HEADS = 2
GQA = 4
KV_WIDTH = 256
KV_STREAMS = 2 * N_KV_HEADS
CMP_LEN = 32
CMP_STRIDE = 16
SEL_LEN = 64
SEL_SHIFT = SEL_LEN.bit_length() - 1
N_SEL = 16
WINDOW = 512
WIN_TQ = 256
FORCED_SCORE = 1e4
D_FF = 5632
NORM_EPS = 1e-6
LOG2E = math.log2(math.e)
QSCALE = HEAD_DIM ** -0.5 * LOG2E
N_GATES = N_HEADS * 3
IN_WIDTH = SSM_WIDTH + ATTN_WIDTH + 6 * KV_WIDTH + N_GATES
F32_TINY = float(jnp.finfo(jnp.float32).tiny)
NEG_INF = float("-inf")

V7X_VMEM_BYTES = 64 * 1024 * 1024
V7X_LANES = 128


def _cparams(semantics, vmem_mib):
    assert vmem_mib * 1024 * 1024 < V7X_VMEM_BYTES
    return pltpu.CompilerParams(dimension_semantics=semantics, vmem_limit_bytes=vmem_mib * 1024 * 1024)


def _resident(shape):
    nd = len(shape)
    return pl.BlockSpec(shape, lambda *_: (0,) * nd, pipeline_mode=pl.Buffered(1))


def _resident_layer(shape, layer):
    nd = len(shape)
    return pl.BlockSpec((None,) + tuple(shape), lambda *_: (layer,) + (0,) * nd, pipeline_mode=pl.Buffered(1))


def _rms(x, g):
    return x * lax.rsqrt(jnp.mean(x * x, axis=-1, keepdims=True) + NORM_EPS) * g


def _gelu(x):
    c = math.sqrt(2.0 / math.pi)
    return x * (0.5 * (1.0 + jnp.tanh(c * (x + 0.044715 * (x * x * x)))))


def _bdot(a, b):
    return jnp.dot(a, b, preferred_element_type=F32)


def _bdot_nt(a, b):
    return lax.dot_general(a, b, (((1,), (1,)), ((), ())), preferred_element_type=F32)


def _split3(x):
    hi = x.astype(BF16)
    r1 = x - hi.astype(F32)
    mid = r1.astype(BF16)
    lo = (r1 - mid.astype(F32)).astype(BF16)
    return hi, mid, lo


def _dot_small_int(x, e):
    hi, mid, lo = _split3(x)
    return _bdot(hi, e) + _bdot(mid, e) + _bdot(lo, e)


def _dot_f32(a, b):
    ah, am, al = _split3(a)
    bh, bm, bl = _split3(b)
    return (_bdot(ah, bh) + (_bdot(ah, bm) + _bdot(am, bh))
            + (_bdot(ah, bl) + _bdot(al, bh) + _bdot(am, bm)))


def _softmax_parts(s, valid):
    s = jnp.where(valid, s, NEG_INF)
    m = jnp.max(s, axis=-1, keepdims=True)
    m = jnp.where(m == NEG_INF, 0.0, m)
    e = jnp.exp2(s - m)
    return m, e, jnp.sum(e, axis=-1, keepdims=True)


def _flash_step(m_ref, acc_ref, g, s, v_ones):
    m_old = m_ref[g]
    m_new = jnp.maximum(m_old, jnp.max(s, axis=-1, keepdims=True))
    m_safe = jnp.where(m_new == NEG_INF, 0.0, m_new)
    acc_ref[g] = jnp.exp2(m_old - m_safe) * acc_ref[g] + _bdot(jnp.exp2(s - m_safe).astype(BF16), v_ones)
    m_ref[g] = m_new


def _alibi_slope(kvh, g):
    return LOG2E * 2.0 ** (-8.0 * (kvh * GQA + g + 1) / N_HEADS)


def _bf16_terms(x, n=3):
    out = []
    for _ in range(n):
        hi = float(np.asarray(x, dtype=np.float32).astype(BF16))
        out.append(hi)
        x -= hi
    return out


def _head_slope(kvh, g):
    assert N_KV_HEADS == 2
    return jnp.where(kvh == 0, _alibi_slope(0, g), _alibi_slope(1, g))


def _inproj_kernel(x_ref, gmix_ref, w_ref, qn_ref, ksn_ref, kwn_ref,
                   u_ref, q_ref, cmp_ref, slc_ref, win_ref, gate_ref):
    xn = _rms(x_ref[...], gmix_ref[...]).astype(BF16)

    def proj(c0, c1):
        return _bdot(xn, w_ref[:, c0:c1])

    u_ref[...] = proj(0, SSM_WIDTH).astype(S5_IO_DTYPE)
    q = proj(SSM_WIDTH, D_MODEL)
    for h in range(N_HEADS):
        sl = slice(h * HEAD_DIM, (h + 1) * HEAD_DIM)
        q_ref[:, sl] = (_rms(q[:, sl], qn_ref[...]) * QSCALE).astype(BF16)
    c0 = D_MODEL
    tm = x_ref.shape[0]
    for ref, nref in ((cmp_ref, None), (slc_ref, ksn_ref), (win_ref, kwn_ref)):
        k = proj(c0, c0 + KV_WIDTH)
        v = proj(c0 + KV_WIDTH, c0 + 2 * KV_WIDTH)
        c0 += 2 * KV_WIDTH
        for kvh in range(N_KV_HEADS):
            src = slice(kvh * HEAD_DIM, (kvh + 1) * HEAD_DIM)
            ref[pl.ds(2 * kvh, tm, stride=KV_STREAMS), :] = (k[:, src] if nref is None
                                                             else _rms(k[:, src], nref[...]))
            ref[pl.ds(2 * kvh + 1, tm, stride=KV_STREAMS), :] = v[:, src]
    gate_ref[...] = jax.nn.sigmoid(proj(c0, c0 + N_GATES))


def _inproj(x, prm, tm):
    m = x.shape[0]
    row = lambda n: pl.BlockSpec((tm, n), lambda i: (i, 0))
    kv_rows = pl.BlockSpec((tm * KV_STREAMS, HEAD_DIM), lambda i: (i, 0))
    kv_shape = jax.ShapeDtypeStruct((m * KV_STREAMS, HEAD_DIM), F32)
    return pl.pallas_call(
        _inproj_kernel,
        grid=(m // tm,),
        in_specs=[row(D_MODEL), _resident((1, D_MODEL)), _resident_layer((D_MODEL, IN_WIDTH), prm["layer"]),
                  _resident((1, HEAD_DIM)), _resident((1, HEAD_DIM)), _resident((1, HEAD_DIM))],
        out_specs=[row(SSM_WIDTH), row(ATTN_WIDTH), kv_rows, kv_rows, kv_rows, row(N_GATES)],
        out_shape=[jax.ShapeDtypeStruct((m, SSM_WIDTH), S5_IO_DTYPE), jax.ShapeDtypeStruct((m, ATTN_WIDTH), BF16),
                   kv_shape, kv_shape, kv_shape, jax.ShapeDtypeStruct((m, N_GATES), F32)],
        compiler_params=_cparams(("parallel",), 48),
        name="inproj",
    )(x, prm["norm_mix"], prm["w_in"], prm["q_norm"], prm["k_norm_slc"], prm["k_norm_win"])


def _s5_group_body(gi, u_ref, h0_ref, lrow_ref, lcol_ref, bt_ref, bn_ref, ct_ref, d_ref,
                   y_ref, hl_ref, tm_ref, x_ref, hin_ref, *, lp, lreal, nb, nc):
    k = lp * SSM_CH
    n = SSM_STATE

    def lam_bar(a_re, a_im, log_dt):
        dt = jnp.exp(log_dt)
        e = jnp.exp(a_re * dt)
        return e * jnp.cos(a_im * dt), e * jnp.sin(a_im * dt)

    def zoh_coef(l_re, l_im, a_re, a_im):
        den = a_re * a_re + a_im * a_im
        x_re = l_re - 1.0
        return (x_re * a_re + l_im * a_im) / den, (l_im * a_re - x_re * a_im) / den

    lc = lcol_ref[gi]
    a_re_c, a_im_c = lc[:, 0:1], lc[:, 1:2]
    l_re_c, l_im_c = lam_bar(a_re_c, a_im_c, lc[:, 2:3])
    lr = lrow_ref[gi]
    a_re_r, a_im_r = lr[0:1], lr[1:2]
    l_re_r, l_im_r = lam_bar(a_re_r, a_im_r, lr[2:3])

    def pow_table(e):
        p_re = jnp.ones((n, k), F32)
        p_im = jnp.zeros((n, k), F32)
        s_re, s_im = l_re_c, l_im_c
        for j in range(max(lp.bit_length() - 1, 1)):
            bit = ((e >> j) & 1) == 1
            m_re = jnp.where(bit, s_re, 1.0)
            m_im = jnp.where(bit, s_im, 0.0)
            p_re, p_im = p_re * m_re - p_im * m_im, p_re * m_im + p_im * m_re
            s_re, s_im = s_re * s_re - s_im * s_im, 2.0 * s_re * s_im
        return p_re, p_im

    lane = lax.broadcasted_iota(jnp.int32, (1, k), 1)
    tau = lane >> (SSM_CH.bit_length() - 1)
    ch_onehot = jnp.where((lax.broadcasted_iota(jnp.int32, (SSM_CH, k), 1) & (SSM_CH - 1))
                          == lax.broadcasted_iota(jnp.int32, (SSM_CH, k), 0), 1.0, 0.0).astype(BF16)

    p_re, p_im = pow_table(tau)
    c_re = _dot_small_int(ct_ref[gi, 0], ch_onehot)
    c_im = _dot_small_int(ct_ref[gi, 1], ch_onehot)
    g_re = c_re * p_re - c_im * p_im
    g_im = c_re * p_im + c_im * p_re

    cf_re_r, cf_im_r = zoh_coef(l_re_r, l_im_r, a_re_r, a_im_r)
    bbt_re = cf_re_r * bt_ref[gi, 0] - cf_im_r * bt_ref[gi, 1]
    bbt_im = cf_re_r * bt_ref[gi, 1] + cf_im_r * bt_ref[gi, 0]
    a = _dot_f32(bbt_re, g_re) - _dot_f32(bbt_im, g_im)

    lane16 = lax.broadcasted_iota(jnp.int32, (SSM_CH, k), 1)
    for s in range(lp):
        blk = a if s == 0 else jnp.where(lane16 >= SSM_CH * s, pltpu.roll(a, SSM_CH * s, axis=1), 0.0)
        tm_ref[gi, SSM_CH * s:SSM_CH * (s + 1), :] = blk.astype(BF16)

    cf_re_c, cf_im_c = zoh_coef(l_re_c, l_im_c, a_re_c, a_im_c)
    bbn_re = cf_re_c * bn_ref[gi, 0] - cf_im_c * bn_ref[gi, 1]
    bbn_im = cf_re_c * bn_ref[gi, 1] + cf_im_c * bn_ref[gi, 0]
    bb_re = _dot_small_int(bbn_re, ch_onehot)
    bb_im = _dot_small_int(bbn_im, ch_onehot)
    rev = (lreal - 1) - tau
    q_re, q_im = pow_table(jnp.maximum(rev, 0))
    live = rev >= 0
    zeros_nk = jnp.zeros((n, k), BF16)
    wx_re = jnp.concatenate([jnp.where(live, q_re * bb_re - q_im * bb_im, 0.0).astype(BF16), zeros_nk], axis=0)
    wx_im = jnp.concatenate([jnp.where(live, q_re * bb_im + q_im * bb_re, 0.0).astype(BF16), zeros_nk], axis=0)

    ub = u_ref[gi].astype(BF16)
    u = ub.astype(F32)
    x_ref[gi, 0] = _bdot_nt(ub, wx_re)
    x_ref[gi, 1] = _bdot_nt(ub, wx_im)

    ll_re, ll_im = l_re_r, l_im_r
    for _ in range(lreal.bit_length() - 1):
        ll_re, ll_im = ll_re * ll_re - ll_im * ll_im, 2.0 * ll_re * ll_im
    ll_re = jnp.concatenate([ll_re, jnp.zeros_like(ll_re)], axis=1)
    ll_im = jnp.concatenate([ll_im, jnp.zeros_like(ll_im)], axis=1)
    lower = lax.broadcasted_iota(jnp.int32, (1, 2 * n), 1) < n
    h0 = h0_ref[gi]
    h_re = jnp.where(lower, h0, 0.0)
    h_im = jnp.where(lower, pltpu.roll(h0, n, axis=1), 0.0)
    for c in range(nc):
        rows = pl.ds(c, nb, stride=nc) if nc > 1 else slice(0, nb)
        hin_ref[gi, 0, rows, :] = h_re
        hin_ref[gi, 1, rows, :] = h_im
        h_re, h_im = (ll_re * h_re - ll_im * h_im + x_ref[gi, 0, rows, :],
                      ll_re * h_im + ll_im * h_re + x_ref[gi, 1, rows, :])
    hl_ref[gi] = jnp.where(lower, h_re, pltpu.roll(h_im, n, axis=1))

    g1_re = jnp.concatenate([(g_re * l_re_c - g_im * l_im_c).astype(BF16), zeros_nk], axis=0)
    g1_im = jnp.concatenate([(g_re * l_im_c + g_im * l_re_c).astype(BF16), zeros_nk], axis=0)
    y_state = _bdot(hin_ref[gi, 0].astype(BF16), g1_re) - _bdot(hin_ref[gi, 1].astype(BF16), g1_im)

    y_ref[gi] = (_bdot(ub, tm_ref[gi]) + y_state + u * d_ref[gi]).astype(S5_IO_DTYPE)


def _s5_kernel(*refs, gb, **kw):
    for gi in range(gb):
        _s5_group_body(gi, *refs, **kw)


def _s5(u_g, h0_g, prm, lp, lreal, nb, nc):
    g, m, k = u_g.shape
    gb = S5_GROUPS_PER_STEP
    blk = lambda *s: pl.BlockSpec((gb,) + s, lambda i: (i,) + (0,) * len(s))
    return pl.pallas_call(
        functools.partial(_s5_kernel, gb=gb, lp=lp, lreal=lreal, nb=nb, nc=nc),
        grid=(g // gb,),
        in_specs=[blk(m, k), blk(nb, 2 * SSM_STATE), blk(3, SSM_STATE), blk(SSM_STATE, 3),
                  blk(2, SSM_CH, SSM_STATE), blk(2, SSM_STATE, SSM_CH), blk(2, SSM_STATE, SSM_CH), blk(1, k)],
        out_specs=[blk(m, k), blk(nb, 2 * SSM_STATE)],
        out_shape=[jax.ShapeDtypeStruct((g, m, k), S5_IO_DTYPE), jax.ShapeDtypeStruct((g, nb, 2 * SSM_STATE), F32)],
        scratch_shapes=[pltpu.VMEM((gb, k, k), BF16), pltpu.VMEM((gb, 2, m, 2 * SSM_STATE), F32),
                        pltpu.VMEM((gb, 2, m, 2 * SSM_STATE), F32)],
        compiler_params=_cparams(("parallel",), 32),
        name="s5_group",
    )(u_g, h0_g, prm["lrow"], prm["lcol"], prm["bt"], prm["bn"], prm["ct"], prm["dtile"][lp])


def _s5_mixer(u, h0, prm, lp, lreal):
    b, t, _ = u.shape
    nc = t // lreal
    ug = u.reshape(b * nc, lreal, SSM_GROUPS, SSM_CH)
    if lp != lreal:
        ug = jnp.pad(ug, ((0, 0), (0, lp - lreal), (0, 0), (0, 0)))
    ug = ug.transpose(2, 0, 1, 3).reshape(SSM_GROUPS, b * nc, lp * SSM_CH)
    h0g = h0.transpose(1, 0, 3, 2).reshape(SSM_GROUPS, b, 2 * SSM_STATE)
    yg, hl = _s5(ug, h0g, prm, lp, lreal, b, nc)
    y = yg.reshape(SSM_GROUPS, b * nc, lp, SSM_CH)[:, :, :lreal].transpose(1, 2, 0, 3)
    h_last = hl.reshape(SSM_GROUPS, b, 2, SSM_STATE).transpose(1, 0, 3, 2)
    return y.reshape(b, t, SSM_WIDTH), h_last


def _cmp_proj_kernel(*refs, n_in):
    refs = refs[len(refs) - n_in - 3:]
    x_refs, (w1k_ref, w1v_ref, o_ref) = refs[:n_in], refs[n_in:]

    def rows_of(j, r):
        n = x_refs[0].shape[0] // (KV_STREAMS * CMP_STRIDE)
        parts = [x[pl.ds(KV_STREAMS * r + j, n, stride=KV_STREAMS * CMP_STRIDE), :] for x in x_refs]
        return parts[0] if n_in == 1 else jnp.concatenate(parts, axis=0)

    for j in range(KV_STREAMS):
        c = jnp.concatenate([rows_of(j, r) for r in range(CMP_STRIDE)], axis=1).astype(BF16)
        w = w1k_ref if j % 2 == 0 else w1v_ref
        o_ref[0, :, j * 2 * HEAD_DIM:(j + 1) * 2 * HEAD_DIM] = _bdot(c, w[...])


def _cmp_combine_kernel(h_ref, b1_ref, w2k_ref, w2v_ref, kn_ref, o_ref, *, nchunk):
    h = h_ref[0]
    keep = lax.broadcasted_iota(jnp.int32, (nchunk, 1), 0) < nchunk - 1
    for j in range(4):
        kvh, is_v = j // 2, j % 2
        lo = h[:, j * 2 * HEAD_DIM: j * 2 * HEAD_DIM + HEAD_DIM]
        hi = h[:, j * 2 * HEAD_DIM + HEAD_DIM:(j + 1) * 2 * HEAD_DIM]
        pre = lo + pltpu.roll(hi, nchunk - 1, axis=0) + b1_ref[is_v:is_v + 1, :]
        out = _bdot(_gelu(pre).astype(BF16), (w2v_ref if is_v else w2k_ref)[...])
        if not is_v:
            out = _rms(out, kn_ref[...])
        c0 = is_v * N_KV_HEADS * HEAD_DIM + kvh * HEAD_DIM
        o_ref[0, :, c0:c0 + HEAD_DIM] = jnp.where(keep, out, 0.0)


def _cmp_combine(h, prm):
    b, nchunk, w = h.shape
    return pl.pallas_call(
        functools.partial(_cmp_combine_kernel, nchunk=nchunk),
        grid=(b,),
        in_specs=[pl.BlockSpec((1, nchunk, w), lambda i: (i, 0, 0)), _resident((2, HEAD_DIM)),
                  _resident((HEAD_DIM, HEAD_DIM)), _resident((HEAD_DIM, HEAD_DIM)), _resident((1, HEAD_DIM))],
        out_specs=pl.BlockSpec((1, nchunk, 4 * HEAD_DIM), lambda i: (i, 0, 0)),
        out_shape=jax.ShapeDtypeStruct((b, nchunk, 4 * HEAD_DIM), F32),
        compiler_params=_cparams(("parallel",), 40),
        name="cmp_combine",
    )(h, prm["cmp_b1"], prm["cmp_w2_k"], prm["cmp_w2_v"], prm["k_norm_cmp"])


def _compress_prompt(kv_rows, b, prm):
    rows = kv_rows.shape[0] // b
    nchunk = rows // (KV_STREAMS * CMP_STRIDE)
    h = pl.pallas_call(
        functools.partial(_cmp_proj_kernel, n_in=1),
        grid=(b,),
        in_specs=[pl.BlockSpec((rows, HEAD_DIM), lambda i: (i, 0)),
                  _resident((CMP_STRIDE * HEAD_DIM, 2 * HEAD_DIM)), _resident((CMP_STRIDE * HEAD_DIM, 2 * HEAD_DIM))],
        out_specs=pl.BlockSpec((1, nchunk, 8 * HEAD_DIM), lambda i: (i, 0, 0)),
        out_shape=jax.ShapeDtypeStruct((b, nchunk, 8 * HEAD_DIM), F32),
        compiler_params=_cparams(("parallel",), 40),
        name="cmp_proj_prompt",
    )(kv_rows, prm["cmp_w1_k"], prm["cmp_w1_v"])
    return _cmp_combine(h, prm)


PAGES_PER_STEP = 16


def _rowwise_pages(cache):
    return cache.reshape(cache.shape[0], cache.shape[1], PAGE_SIZE * 2 * N_KV_HEADS, HEAD_DIM)


def _compress_sample(cache_cmp, layer, page_table, prm):
    bsz, n_pages = page_table.shape
    chunks_per_page = PAGE_SIZE // CMP_STRIDE
    pages = _rowwise_pages(cache_cmp)
    steps = n_pages // PAGES_PER_STEP

    def page_spec(p):
        return pl.BlockSpec((None, None, 4 * PAGE_SIZE, HEAD_DIM),
                            lambda b, s, pt: (layer, pt[b * n_pages + s * PAGES_PER_STEP + p], 0, 0))

    rows = PAGES_PER_STEP * chunks_per_page
    h = pl.pallas_call(
        functools.partial(_cmp_proj_kernel, n_in=PAGES_PER_STEP),
        grid_spec=pltpu.PrefetchScalarGridSpec(
            num_scalar_prefetch=1,
            grid=(bsz, steps),
            in_specs=[page_spec(p) for p in range(PAGES_PER_STEP)] + [
                pl.BlockSpec((CMP_STRIDE * HEAD_DIM, 2 * HEAD_DIM), lambda b, s, pt: (0, 0)),
                pl.BlockSpec((CMP_STRIDE * HEAD_DIM, 2 * HEAD_DIM), lambda b, s, pt: (0, 0))],
            out_specs=pl.BlockSpec((1, rows, 8 * HEAD_DIM), lambda b, s, pt: (b, s, 0)),
        ),
        out_shape=jax.ShapeDtypeStruct((bsz, n_pages * chunks_per_page, 8 * HEAD_DIM), F32),
        compiler_params=_cparams(("parallel", "parallel"), 40),
        name="cmp_proj_sample",
    )(page_table.reshape(-1), *([pages] * PAGES_PER_STEP), prm["cmp_w1_k"], prm["cmp_w1_v"])
    return _cmp_combine(h, prm)


def _cmp_select_kernel(q_ref, ck_ref, cv_ref, o_ref, rank_ref, *idx_ref, tq, nc, nbp, nblk, qpos0):
    kvh = pl.program_id(1)
    qpos = qpos0 + pl.program_id(2) * tq + lax.broadcasted_iota(jnp.int32, (tq, 1), 0)
    assert tq & (tq - 1) == 0
    start = lax.broadcasted_iota(jnp.int32, (1, nc), 1) * CMP_STRIDE
    ck = ck_ref[0].astype(BF16)
    cv = cv_ref[0].astype(BF16)
    q_rows = jnp.concatenate([q_ref[0, :, g * HEAD_DIM:(g + 1) * HEAD_DIM] for g in range(GQA)], axis=0)
    row = lax.broadcasted_iota(jnp.int32, (GQA * tq, 1), 0)
    qpos_rows = qpos0 + pl.program_id(2) * tq + (row & (tq - 1))
    slope_rows = jnp.zeros((GQA * tq, 1), F32)
    for g in range(GQA):
        slope_rows = jnp.where(row >= g * tq, _head_slope(kvh, g), slope_rows)
    bias = slope_rows * ((qpos_rows - start).astype(F32) - (CMP_LEN - 1) / 2)
    _, e, den = _softmax_parts(_bdot_nt(q_rows, ck) - bias, (start + (CMP_LEN - 1)) <= qpos_rows)
    p = e / jnp.maximum(den, F32_TINY)
    o_rows = _bdot(p.astype(BF16), cv)
    psum = jnp.zeros((tq, nc), F32)
    for g in range(GQA):
        o_ref[0, :, g * HEAD_DIM:(g + 1) * HEAD_DIM] = o_rows[g * tq:(g + 1) * tq]
        psum = psum + p[g * tq:(g + 1) * tq]

    per_shift = (SEL_LEN // CMP_STRIDE).bit_length() - 1

    def ranks(blocks_on_rows):
        nbs = -(-nblk // 8) * 8
        ov_shape = (nbs, nc) if blocks_on_rows else (nc, nbp)
        ci = lax.broadcasted_iota(jnp.int32, ov_shape, 1 if blocks_on_rows else 0)
        bj = lax.broadcasted_iota(jnp.int32, ov_shape, 0 if blocks_on_rows else 1)
        overlap = (jnp.where(ci >> per_shift == bj, 1.0, 0.0)
                   + jnp.where((ci + 1) >> per_shift == bj, 1.0, 0.0)).astype(BF16)
        if blocks_on_rows:
            score = sum(_bdot_nt(overlap, part) for part in _split3(psum))
            j = lax.broadcasted_iota(jnp.int32, (nbs, 1), 0)
            pos = qpos0 + pl.program_id(2) * tq + lax.broadcasted_iota(jnp.int32, (1, tq), 1)
        else:
            score = _dot_small_int(psum, overlap)
            j = lax.broadcasted_iota(jnp.int32, (1, nbp), 1)
            pos = qpos
        qb = pos >> SEL_SHIFT
        forced = (j == 0) | (j == qb) | (j == qb - 1)
        score = jnp.where(forced, FORCED_SCORE, jnp.where((j * SEL_LEN) <= pos, score, -1.0))
        score = jnp.where(j < nblk, score, -2.0)
        rank = jnp.zeros(score.shape, F32)
        for i in range(nblk):
            si = score[i:i + 1, :] if blocks_on_rows else score[:, i:i + 1]
            rank = rank + jnp.where(j > i, jnp.where(si >= score, 1.0, 0.0), jnp.where(si > score, 1.0, 0.0))
        return rank

    if idx_ref or qpos0 >= N_SEL * SEL_LEN:
        rank = ranks(False)
        rank_ref[0, 0] = rank
    else:
        crowded = qpos0 + (pl.program_id(2) + 1) * tq > N_SEL * SEL_LEN

        @pl.when(crowded)
        def _():
            rank_t = ranks(True)
            unused = jnp.full((nbp - rank_t.shape[0], tq), float(nblk), F32)
            rank_ref[0, 0] = jnp.concatenate([rank_t, unused], axis=0).T

        @pl.when(jnp.logical_not(crowded))
        def _():
            rank_ref[0, 0] = jnp.zeros((tq, nbp), F32)

    if idx_ref:
        jf = lax.broadcasted_iota(jnp.int32, (1, nbp), 1).astype(F32)
        lane = lax.broadcasted_iota(jnp.int32, (tq, V7X_LANES), 1)
        idx = jnp.zeros((tq, V7X_LANES), F32)
        for r in range(N_SEL):
            col = jnp.sum(jnp.where(rank == float(r), jf, 0.0), axis=-1, keepdims=True)
            idx = jnp.where(lane == r, col, idx)
        idx_ref[0][0, 0] = idx.astype(jnp.int32)


def _cmp_select(q, ckv, tq, nblk, qpos0, want_idx):
    b, t, _ = q.shape
    nc = ckv.shape[1]
    nbp = -(-nblk // V7X_LANES) * V7X_LANES
    hw = GQA * HEAD_DIM
    out_specs = [pl.BlockSpec((1, tq, hw), lambda bi, k, i: (bi, i, k)),
                 pl.BlockSpec((1, 1, tq, nbp), lambda bi, k, i: (bi, k, i, 0))]
    out_shape = [jax.ShapeDtypeStruct((b, t, ATTN_WIDTH), F32),
                 jax.ShapeDtypeStruct((b, N_KV_HEADS, t, nbp), F32)]
    if want_idx:
        out_specs.append(pl.BlockSpec((1, 1, tq, V7X_LANES), lambda bi, k, i: (bi, k, i, 0)))
        out_shape.append(jax.ShapeDtypeStruct((b, N_KV_HEADS, t, V7X_LANES), jnp.int32))
    return pl.pallas_call(
        functools.partial(_cmp_select_kernel, tq=tq, nc=nc, nbp=nbp, nblk=nblk, qpos0=qpos0),
        grid=(b, N_KV_HEADS, t // tq),
        in_specs=[pl.BlockSpec((1, tq, hw), lambda bi, k, i: (bi, i, k)),
                  pl.BlockSpec((1, nc, HEAD_DIM), lambda bi, k, i: (bi, 0, k)),
                  pl.BlockSpec((1, nc, HEAD_DIM), lambda bi, k, i: (bi, 0, N_KV_HEADS + k))],
        out_specs=out_specs,
        out_shape=out_shape,
        compiler_params=_cparams(("parallel", "parallel", "parallel"), 40),
        name="cmp_select",
    )(q, ckv, ckv)


ATT_TQ = 256
SEL_TK = 512
FILL_ROWS = 256
POS_RADIX = 256
RATE_TERMS = 3
MASK_BIAS = 2.0 ** 100


def _key_features(kidx, nblk):
    lane = lax.broadcasted_iota(jnp.int32, (1, V7X_LANES), 1)
    radix_shift = POS_RADIX.bit_length() - 1
    digits = jnp.where(((lane - nblk) & 1) == 0, (kidx >> radix_shift).astype(F32),
                       (kidx & (POS_RADIX - 1)).astype(F32))
    feats = jnp.where(lane < nblk + 2 * RATE_TERMS, digits, 0.0)
    if nblk:
        feats = jnp.where(lane < nblk, jnp.where((kidx >> SEL_SHIFT) == lane, 1.0, 0.0), feats)
    return feats


def _query_features(kvh, g, rank, nblk, rows):
    lane = lax.broadcasted_iota(jnp.int32, (rows, V7X_LANES), 1)
    feats = jnp.zeros((rows, V7X_LANES), F32)
    if nblk:
        feats = jnp.where((lane < nblk) & (rank >= float(N_SEL)), -MASK_BIAS, 0.0)
    for i, (c0, c1) in enumerate(zip(_bf16_terms(_alibi_slope(0, g), RATE_TERMS),
                                     _bf16_terms(_alibi_slope(1, g), RATE_TERMS))):
        c = jnp.where(kvh == 0, c0, c1)
        feats = jnp.where(lane == nblk + 2 * i, c * POS_RADIX, feats)
        feats = jnp.where(lane == nblk + 2 * i + 1, c, feats)
    return feats


def _fill_key_value_scratch(kv_ref, kvh, kf_ref, vo_ref, nblk):
    def chunk(c, carry):
        r0 = pl.multiple_of(c * FILL_ROWS, FILL_ROWS)
        rows = pl.ds(r0, FILL_ROWS)
        kidx = r0 + lax.broadcasted_iota(jnp.int32, (FILL_ROWS, 1), 0)
        k_rows = pl.ds(r0 * KV_STREAMS + 2 * kvh, FILL_ROWS, stride=KV_STREAMS)
        v_rows = pl.ds(r0 * KV_STREAMS + 2 * kvh + 1, FILL_ROWS, stride=KV_STREAMS)
        kf_ref[rows, :HEAD_DIM] = kv_ref[0, k_rows, :].astype(BF16)
        kf_ref[rows, HEAD_DIM:] = _key_features(kidx, nblk).astype(BF16)
        vo_ref[rows, :HEAD_DIM] = kv_ref[0, v_rows, :].astype(BF16)
        vo_ref[rows, HEAD_DIM:] = jnp.ones((FILL_ROWS, V7X_LANES), BF16)
        return carry

    lax.fori_loop(0, kv_ref.shape[1] // (KV_STREAMS * FILL_ROWS), chunk, 0)


def _attn_out(acc):
    return acc[:, :HEAD_DIM] / jnp.maximum(acc[:, HEAD_DIM:HEAD_DIM + 1], F32_TINY)


def _sel_prompt_kernel(q_ref, kv_ref, rank_ref, o_ref, kf_ref, vo_ref, qf_ref, m_ref, acc_ref, *, nblk):
    kvh, i = pl.program_id(1), pl.program_id(2)

    @pl.when(i == 0)
    def _():
        _fill_key_value_scratch(kv_ref, kvh, kf_ref, vo_ref, nblk)

    q0 = i * ATT_TQ
    qpos = q0 + (lax.broadcasted_iota(jnp.int32, (GQA * ATT_TQ, 1), 0) & (ATT_TQ - 1))
    rank = rank_ref[0, 0]
    for g in range(GQA):
        rows = slice(g * ATT_TQ, (g + 1) * ATT_TQ)
        qf_ref[rows, :HEAD_DIM] = q_ref[0, :, g * HEAD_DIM:(g + 1) * HEAD_DIM]
        qf_ref[rows, HEAD_DIM:] = _query_features(kvh, g, rank, nblk, ATT_TQ).astype(BF16)
    m_ref[...] = jnp.full(m_ref.shape, NEG_INF, F32)
    acc_ref[...] = jnp.zeros(acc_ref.shape, F32)

    def tile(jt, carry):
        k0 = pl.multiple_of(jt * SEL_TK, SEL_TK)
        causal = (k0 + lax.broadcasted_iota(jnp.int32, (1, SEL_TK), 1)) <= qpos
        s = jnp.where(causal, _bdot_nt(qf_ref[...], kf_ref[pl.ds(k0, SEL_TK), :]), NEG_INF)
        _flash_step(m_ref, acc_ref, 0, s, vo_ref[pl.ds(k0, SEL_TK), :])
        return carry

    lax.fori_loop(0, (q0 + ATT_TQ + SEL_TK - 1) // SEL_TK, tile, 0)
    out = _attn_out(acc_ref[0])
    for g in range(GQA):
        o_ref[0, :, g * HEAD_DIM:(g + 1) * HEAD_DIM] = out[g * ATT_TQ:(g + 1) * ATT_TQ]


def _sel_prompt(q, kv_slc, rank, nblk):
    b, t, _ = q.shape
    nbp = rank.shape[-1]
    hw = GQA * HEAD_DIM
    assert nbp == V7X_LANES and nblk + 2 * RATE_TERMS <= V7X_LANES and t <= POS_RADIX * POS_RADIX
    return pl.pallas_call(
        functools.partial(_sel_prompt_kernel, nblk=nblk),
        grid=(b, N_KV_HEADS, t // ATT_TQ),
        in_specs=[pl.BlockSpec((1, ATT_TQ, hw), lambda bi, k, i: (bi, i, k)),
                  pl.BlockSpec((1, t * KV_STREAMS, HEAD_DIM), lambda bi, k, i: (bi, 0, 0)),
                  pl.BlockSpec((1, 1, ATT_TQ, nbp), lambda bi, k, i: (bi, k, i, 0))],
        out_specs=pl.BlockSpec((1, ATT_TQ, hw), lambda bi, k, i: (bi, i, k)),
        out_shape=jax.ShapeDtypeStruct((b, t, ATTN_WIDTH), F32),
        scratch_shapes=[pltpu.VMEM((t, 2 * HEAD_DIM), BF16), pltpu.VMEM((t, 2 * HEAD_DIM), BF16),
                        pltpu.VMEM((GQA * ATT_TQ, 2 * HEAD_DIM), BF16),
                        pltpu.VMEM((1, GQA * ATT_TQ, 1), F32), pltpu.VMEM((1, GQA * ATT_TQ, 2 * HEAD_DIM), F32)],
        compiler_params=_cparams(("parallel", "parallel", "arbitrary"), 32),
        name="sel_prompt",
    )(q, kv_slc, rank)


def _win_prompt_kernel(q_ref, kv_ref, o_ref, kf_ref, vo_ref):
    kvh, i = pl.program_id(1), pl.program_id(2)

    @pl.when(i == 0)
    def _():
        _fill_key_value_scratch(kv_ref, kvh, kf_ref, vo_ref, 0)

    span = WINDOW + WIN_TQ
    qpos = i * WIN_TQ + (lax.broadcasted_iota(jnp.int32, (GQA * WIN_TQ, 1), 0) & (WIN_TQ - 1))
    k0 = pl.multiple_of(jnp.maximum(i * WIN_TQ - WINDOW, 0), WIN_TQ)
    d = qpos - (k0 + lax.broadcasted_iota(jnp.int32, (1, span), 1))
    qf = jnp.concatenate(
        [jnp.concatenate([q_ref[0, :, g * HEAD_DIM:(g + 1) * HEAD_DIM],
                          _query_features(kvh, g, None, 0, WIN_TQ).astype(BF16)], axis=1) for g in range(GQA)],
        axis=0)
    s = jnp.where((d >= 0) & (d < WINDOW), _bdot_nt(qf, kf_ref[pl.ds(k0, span), :]), NEG_INF)
    e = jnp.exp2(s - jnp.max(s, axis=-1, keepdims=True))
    out = _attn_out(_bdot(e.astype(BF16), vo_ref[pl.ds(k0, span), :]))
    for g in range(GQA):
        o_ref[0, :, g * HEAD_DIM:(g + 1) * HEAD_DIM] = out[g * WIN_TQ:(g + 1) * WIN_TQ]


def _win_prompt(q, kv_win):
    b, t, _ = q.shape
    hw = GQA * HEAD_DIM
    assert t <= POS_RADIX * POS_RADIX
    return pl.pallas_call(
        _win_prompt_kernel,
        grid=(b, N_KV_HEADS, t // WIN_TQ),
        in_specs=[pl.BlockSpec((1, WIN_TQ, hw), lambda bi, k, i: (bi, i, k)),
                  pl.BlockSpec((1, t * KV_STREAMS, HEAD_DIM), lambda bi, k, i: (bi, 0, 0))],
        out_specs=pl.BlockSpec((1, WIN_TQ, hw), lambda bi, k, i: (bi, i, k)),
        out_shape=jax.ShapeDtypeStruct((b, t, ATTN_WIDTH), F32),
        scratch_shapes=[pltpu.VMEM((t, 2 * HEAD_DIM), BF16), pltpu.VMEM((t, 2 * HEAD_DIM), BF16)],
        compiler_params=_cparams(("parallel", "parallel", "arbitrary"), 32),
        name="win_prompt",
    )(q, kv_win)


def _slope_col(kvh):
    g = lax.broadcasted_iota(jnp.int32, (GQA, 1), 0)
    col = jnp.zeros((GQA, 1), F32)
    for i in range(GQA):
        col = jnp.where(g == i, _alibi_slope(kvh, i), col)
    return col


def _stream_rows(ref, lead, stream, n):
    return ref[lead + (pl.ds(stream, n, stride=2 * N_KV_HEADS), slice(None))]


def _sel_sample_kernel(idx_ref, pt_ref, q_ref, *refs, t_len):
    past, new_ref, o_ref = refs[:N_KV_HEADS * N_SEL], refs[N_KV_HEADS * N_SEL], refs[N_KV_HEADS * N_SEL + 1]
    b, t = pl.program_id(0), pl.program_id(1)
    past_blocks = PAST_LEN // SEL_LEN
    lane = lax.broadcasted_iota(jnp.int32, (1, N_SEL * SEL_LEN), 1)
    for kvh in range(N_KV_HEADS):
        ks, vs = [], []
        base = jnp.zeros((1, N_SEL * SEL_LEN), jnp.int32)
        for r in range(N_SEL):
            blk = idx_ref[((b * N_KV_HEADS + kvh) * t_len + t) * N_SEL + r]
            is_new = blk >= past_blocks
            ref = past[kvh * N_SEL + r]
            ks.append(jnp.where(is_new, _stream_rows(new_ref, (0,), 2 * kvh, SEL_LEN),
                                _stream_rows(ref, (), 2 * kvh, SEL_LEN)).astype(BF16))
            vs.append(jnp.where(is_new, _stream_rows(new_ref, (0,), 2 * kvh + 1, SEL_LEN),
                                _stream_rows(ref, (), 2 * kvh + 1, SEL_LEN)).astype(BF16))
            base = jnp.where((lane >> SEL_SHIFT) == r, blk * SEL_LEN, base)
        d = (PAST_LEN + t) - (base + (lane & (SEL_LEN - 1)))
        rows = slice(kvh * GQA, (kvh + 1) * GQA)
        s = _bdot_nt(q_ref[0, 0, rows, :], jnp.concatenate(ks, axis=0)) - _slope_col(kvh) * d.astype(F32)
        _, e, z = _softmax_parts(s, d >= 0)
        o_ref[0, 0, rows, :] = _bdot(e.astype(BF16), jnp.concatenate(vs, axis=0)) / jnp.maximum(z, F32_TINY)


def _sel_sample(q_h, cache_slc, layer, page_table, idx, kv_new_rows):
    bsz, t_len = q_h.shape[:2]
    n_pages = page_table.shape[1]
    per_page = PAGE_SIZE // SEL_LEN
    past_blocks = PAST_LEN // SEL_LEN
    pages = _rowwise_pages(cache_slc)
    blk_rows = SEL_LEN * 2 * N_KV_HEADS

    def past_spec(kvh, r):
        def imap(b, t, idx_ref, pt_ref):
            blk = jnp.minimum(idx_ref[((b * N_KV_HEADS + kvh) * t_len + t) * N_SEL + r], past_blocks - 1)
            return (layer, pt_ref[b * n_pages + blk // per_page], blk % per_page, 0)
        return pl.BlockSpec((None, None, blk_rows, HEAD_DIM), imap)

    qspec = pl.BlockSpec((1, 1, N_HEADS, HEAD_DIM), lambda b, t, i_, p_: (b, t, 0, 0))
    return pl.pallas_call(
        functools.partial(_sel_sample_kernel, t_len=t_len),
        grid_spec=pltpu.PrefetchScalarGridSpec(
            num_scalar_prefetch=2,
            grid=(bsz, t_len),
            in_specs=[qspec] + [past_spec(kvh, r) for kvh in range(N_KV_HEADS) for r in range(N_SEL)]
            + [pl.BlockSpec((1, blk_rows, HEAD_DIM), lambda b, t, i_, p_: (b, 0, 0))],
            out_specs=qspec,
        ),
        out_shape=jax.ShapeDtypeStruct(q_h.shape, F32),
        compiler_params=_cparams(("parallel", "parallel"), 32),
        name="sel_sample",
    )(idx.reshape(-1), page_table.reshape(-1), q_h, *([pages] * (N_KV_HEADS * N_SEL)), kv_new_rows)


def _win_sample_kernel(q_ref, past_ref, new_ref, o_ref, *, t_len, t_pad, wbuf):
    t = lax.broadcasted_iota(jnp.int32, (t_len, 1), 0)
    d_past = wbuf + t - lax.broadcasted_iota(jnp.int32, (1, wbuf), 1)
    row_new = lax.broadcasted_iota(jnp.int32, (1, t_pad), 1)
    d_new = t - row_new
    valid_past = (d_past >= 0) & (d_past < WINDOW)
    valid_new = (d_new >= 0) & (d_new < WINDOW) & (row_new < t_len)
    for kvh in range(N_KV_HEADS):
        kp = _stream_rows(past_ref, (0,), 2 * kvh, wbuf).astype(BF16)
        vp = _stream_rows(past_ref, (0,), 2 * kvh + 1, wbuf).astype(BF16)
        kn = _stream_rows(new_ref, (0,), 2 * kvh, t_pad).astype(BF16)
        vn = _stream_rows(new_ref, (0,), 2 * kvh + 1, t_pad).astype(BF16)
        for g in range(GQA):
            q = q_ref[0, kvh, g]
            slope = _alibi_slope(kvh, g)
            m1, e1, z1 = _softmax_parts(_bdot_nt(q, kp) - slope * d_past.astype(F32), valid_past)
            m2, e2, z2 = _softmax_parts(_bdot_nt(q, kn) - slope * d_new.astype(F32), valid_new)
            m = jnp.maximum(m1, m2)
            a1 = jnp.where(z1 > 0.0, jnp.exp2(m1 - m), 0.0)
            a2 = jnp.where(z2 > 0.0, jnp.exp2(m2 - m), 0.0)
            num = a1 * _bdot(e1.astype(BF16), vp) + a2 * _bdot(e2.astype(BF16), vn)
            o_ref[0, kvh, g] = num / jnp.maximum(a1 * z1 + a2 * z2, F32_TINY)


def _win_sample(q_g, win_rows, new_rows):
    bsz, _, _, t_len, _ = q_g.shape
    streams = 2 * N_KV_HEADS
    wbuf, t_pad = win_rows.shape[1] // streams, new_rows.shape[1] // streams
    qspec = pl.BlockSpec((1, N_KV_HEADS, GQA, t_len, HEAD_DIM), lambda b: (b, 0, 0, 0, 0))
    return pl.pallas_call(
        functools.partial(_win_sample_kernel, t_len=t_len, t_pad=t_pad, wbuf=wbuf),
        grid=(bsz,),
        in_specs=[qspec, pl.BlockSpec((1, wbuf * streams, HEAD_DIM), lambda b: (b, 0, 0)),
                  pl.BlockSpec((1, t_pad * streams, HEAD_DIM), lambda b: (b, 0, 0))],
        out_specs=qspec,
        out_shape=jax.ShapeDtypeStruct(q_g.shape, F32),
        compiler_params=_cparams(("parallel",), 16),
        name="win_sample",
    )(q_g, win_rows, new_rows)


def _mix_kernel(x_ref, y_ref, oc_ref, os_ref, ow_ref, gate_ref, wglu_ref, gs_ref, ga_ref, wout_ref, o_ref):
    z = _gelu(y_ref[...].astype(F32))
    z = z * jax.nn.sigmoid(_bdot(z.astype(BF16), wglu_ref[...]))
    gate = gate_ref[...]
    cols = []
    for h in range(N_HEADS):
        sl = slice(h * HEAD_DIM, (h + 1) * HEAD_DIM)
        cols.append(gate[:, 3 * h:3 * h + 1] * oc_ref[:, sl] + gate[:, 3 * h + 1:3 * h + 2] * os_ref[:, sl]
                    + gate[:, 3 * h + 2:3 * h + 3] * ow_ref[:, sl])
    attn = jnp.concatenate(cols, axis=1)
    hs = _rms(z, gs_ref[...]).astype(BF16)
    ha = _rms(attn, ga_ref[...]).astype(BF16)
    o_ref[...] = x_ref[...] + (_bdot(hs, wout_ref[:SSM_WIDTH, :]) + _bdot(ha, wout_ref[SSM_WIDTH:, :]))


def _mix(x, y, oc, osl, ow, gates, prm, tm):
    m = x.shape[0]
    row = lambda n: pl.BlockSpec((tm, n), lambda i: (i, 0))
    return pl.pallas_call(
        _mix_kernel,
        grid=(m // tm,),
        in_specs=[row(D_MODEL), row(SSM_WIDTH), row(ATTN_WIDTH), row(ATTN_WIDTH), row(ATTN_WIDTH), row(N_GATES),
                  _resident_layer((SSM_WIDTH, SSM_WIDTH), prm["layer"]), _resident((1, SSM_WIDTH)),
                  _resident((1, ATTN_WIDTH)), _resident_layer((D_MODEL, D_MODEL), prm["layer"])],
        out_specs=row(D_MODEL),
        out_shape=jax.ShapeDtypeStruct((m, D_MODEL), F32),
        compiler_params=_cparams(("parallel",), 48),
        name="mix_outproj",
    )(x, y, oc, osl, ow, gates, prm["ssm_w_glu"], prm["norm_ssm_out"], prm["norm_attn_out"], prm["w_out"])


FFN_TF = 512


def _ffn_kernel(x_ref, g_ref, wg_ref, wu_ref, wd_ref, o_ref, xn_ref, acc_ref):
    j = pl.program_id(1)

    @pl.when(j == 0)
    def _():
        xn_ref[...] = _rms(x_ref[...], g_ref[...]).astype(BF16)
        acc_ref[...] = jnp.zeros(acc_ref.shape, F32)

    xn = xn_ref[...]
    a = _bdot(xn, wg_ref[...])
    h = (a * jax.nn.sigmoid(a)) * _bdot(xn, wu_ref[...])
    acc_ref[...] += _bdot(h.astype(BF16), wd_ref[...])

    @pl.when(j == pl.num_programs(1) - 1)
    def _():
        o_ref[...] = x_ref[...] + acc_ref[...]


def _ffn(x, prm, tm):
    m = x.shape[0]
    layer = prm["layer"]
    return pl.pallas_call(
        _ffn_kernel,
        grid=(m // tm, D_FF // FFN_TF),
        in_specs=[pl.BlockSpec((tm, D_MODEL), lambda i, j: (i, 0)),
                  pl.BlockSpec((1, D_MODEL), lambda i, j: (0, 0)),
                  pl.BlockSpec((None, D_MODEL, FFN_TF), lambda i, j: (layer, 0, j)),
                  pl.BlockSpec((None, D_MODEL, FFN_TF), lambda i, j: (layer, 0, j)),
                  pl.BlockSpec((None, FFN_TF, D_MODEL), lambda i, j: (layer, j, 0))],
        out_specs=pl.BlockSpec((tm, D_MODEL), lambda i, j: (i, 0)),
        out_shape=jax.ShapeDtypeStruct((m, D_MODEL), F32),
        scratch_shapes=[pltpu.VMEM((tm, D_MODEL), BF16), pltpu.VMEM((tm, D_MODEL), F32)],
        compiler_params=_cparams(("parallel", "arbitrary"), 48),
        name="ffn",
    )(x, prm["norm_ffn"], prm["w_ffn_gate"], prm["w_ffn_up"], prm["w_ffn_down"])


S5_CHUNK = 64
S5_GROUPS_PER_STEP = 2
SAMPLE_S5_PAD = 8


BIG_WEIGHTS = ("w_in", "ssm_w_glu", "w_out", "w_ffn_gate", "w_ffn_up", "w_ffn_down")


def _layer_params(l, w, big):
    def w1cat(w1):
        half = CMP_STRIDE * HEAD_DIM
        return jnp.concatenate([w1[:half], w1[half:]], axis=1).astype(BF16)

    row = lambda v: v.reshape(1, -1)
    a_re, a_im = w["ssm_a_re"][l], w["ssm_a_im"][l]
    log_dt = jnp.broadcast_to(w["ssm_log_dt"][l][:, None], a_re.shape)
    lrow = jnp.stack([a_re, a_im, log_dt], axis=1)
    d = w["ssm_d"][l]
    return {
        "layer": l, **big, "norm_mix": row(w["norm_mix"][l]),
        "q_norm": row(w["q_norm"][l]), "k_norm_slc": row(w["k_norm_slc"][l]),
        "k_norm_win": row(w["k_norm_win"][l]), "k_norm_cmp": row(w["k_norm_cmp"][l]),
        "lrow": lrow, "lcol": lrow.transpose(0, 2, 1),
        "bt": jnp.stack([w["ssm_b_re"][l], w["ssm_b_im"][l]], axis=1).transpose(0, 1, 3, 2),
        "bn": jnp.stack([w["ssm_b_re"][l], w["ssm_b_im"][l]], axis=1),
        "ct": jnp.stack([w["ssm_c_re"][l], w["ssm_c_im"][l]], axis=1).transpose(0, 1, 3, 2),
        "dtile": {lp: jnp.tile(d, (1, lp))[:, None, :] for lp in (S5_CHUNK, SAMPLE_S5_PAD)},
        "cmp_w1_k": w1cat(w["cmp_w1_k"][l]), "cmp_w1_v": w1cat(w["cmp_w1_v"][l]),
        "cmp_b1": jnp.stack([w["cmp_b1_k"][l], w["cmp_b1_v"][l]]),
        "cmp_w2_k": w["cmp_w2_k"][l].astype(BF16), "cmp_w2_v": w["cmp_w2_v"][l].astype(BF16),
        "norm_ssm_out": row(w["norm_ssm_out"][l]), "norm_attn_out": row(w["norm_attn_out"][l]),
        "norm_ffn": row(w["norm_ffn"][l]),
    }


def _kv_out(kv, b, t):
    return kv.reshape(b, t, N_KV_HEADS, 2, HEAD_DIM)


PROMPT_TM = 256
MIX_TM = 256
FFN_TM = 512


def _prompt_layer(x, prm):
    b, t, _ = x.shape
    m = b * t
    x2 = x.reshape(m, D_MODEL)
    u, q, kv_cmp, kv_slc, kv_win, gates = _inproj(x2, prm, PROMPT_TM)
    h0 = jnp.zeros((b, SSM_GROUPS, SSM_STATE, 2), F32)
    y, h_last = _s5_mixer(u.reshape(b, t, SSM_WIDTH), h0, prm, S5_CHUNK, S5_CHUNK)
    q3 = q.reshape(b, t, ATTN_WIDTH)
    ckv = _compress_prompt(kv_cmp, b, prm)
    o_cmp, rank = _cmp_select(q3, ckv, 256, t // SEL_LEN, 0, False)
    o_slc = _sel_prompt(q3, kv_slc.reshape(b, t * KV_STREAMS, HEAD_DIM), rank, t // SEL_LEN)
    o_win = _win_prompt(q3, kv_win.reshape(b, t * KV_STREAMS, HEAD_DIM))
    x2 = _mix(x2, y.reshape(m, SSM_WIDTH), o_cmp.reshape(m, -1), o_slc.reshape(m, -1), o_win.reshape(m, -1),
              gates, prm, MIX_TM)
    x2 = _ffn(x2, prm, FFN_TM)
    wbuf = min(WINDOW, PAST_LEN)
    kv_win5 = _kv_out(kv_win, b, t)
    win_state = kv_win5[:, t - wbuf:] if t >= wbuf else jnp.pad(kv_win5, ((0, 0), (wbuf - t, 0)) + ((0, 0),) * 3)
    return (x2.reshape(b, t, D_MODEL), _kv_out(kv_cmp, b, t), _kv_out(kv_slc, b, t), win_state, h_last)


def _sample_layer(x, prm, cache_cmp, cache_slc, layer, page_table, win_buf, h0):
    b, t, _ = x.shape
    m = b * t
    assert (PAST_LEN + t) // CMP_STRIDE == PAST_LEN // CMP_STRIDE and PAST_LEN % PAGE_SIZE == 0
    x2 = x.reshape(m, D_MODEL)
    u, q, kv_cmp, kv_slc, kv_win, gates = _inproj(x2, prm, m)
    y, h_last = _s5_mixer(u.reshape(b, t, SSM_WIDTH), h0, prm, SAMPLE_S5_PAD, t)
    q3 = q.reshape(b, t, ATTN_WIDTH)
    ckv = _compress_sample(cache_cmp, layer, page_table, prm)
    n_blocks = -(-(PAST_LEN + t) // SEL_LEN)
    o_cmp, _, idx = _cmp_select(q3, ckv, t, n_blocks, PAST_LEN, True)

    def padded_rows(kv_rows, t_pad):
        return jnp.pad(kv_rows.reshape(b, t * KV_STREAMS, HEAD_DIM), ((0, 0), (0, (t_pad - t) * KV_STREAMS), (0, 0)))

    o_slc = _sel_sample(q3.reshape(b, t, N_HEADS, HEAD_DIM), cache_slc, layer, page_table, idx[..., :N_SEL],
                        padded_rows(kv_slc, SEL_LEN)).reshape(m, ATTN_WIDTH)
    wbuf = win_buf.shape[1]
    q5 = q3.reshape(b, t, N_KV_HEADS, GQA, HEAD_DIM)
    o_win = _win_sample(q5.transpose(0, 2, 3, 1, 4), win_buf.reshape(b, wbuf * KV_STREAMS, HEAD_DIM),
                        padded_rows(kv_win, 8))
    o_win = o_win.transpose(0, 3, 1, 2, 4).reshape(m, ATTN_WIDTH)
    x2 = _mix(x2, y.reshape(m, SSM_WIDTH), o_cmp.reshape(m, -1), o_slc, o_win, gates, prm, m)
    x2 = _ffn(x2, prm, m)
    win_state = jnp.concatenate([win_buf, _kv_out(kv_win, b, t)], axis=1)[:, -wbuf:]
    return (x2.reshape(b, t, D_MODEL), _kv_out(kv_cmp, b, t), _kv_out(kv_slc, b, t), win_state, h_last)


def kernel(x_prompt, x_sample, cache_cmp_kv, cache_slc_kv, state_win_kv, state_ssm, page_table, norm_mix, w_in, ssm_a_re, ssm_a_im, ssm_log_dt, ssm_b_re, ssm_b_im, ssm_c_re, ssm_c_im, ssm_d, ssm_w_glu, q_norm, k_norm_cmp, k_norm_slc, k_norm_win, cmp_w1_k, cmp_b1_k, cmp_w2_k, cmp_w1_v, cmp_b1_v, cmp_w2_v, norm_ssm_out, norm_attn_out, w_out, norm_ffn, w_ffn_gate, w_ffn_up, w_ffn_down):
    w = dict(norm_mix=norm_mix, w_in=w_in, ssm_a_re=ssm_a_re, ssm_a_im=ssm_a_im, ssm_log_dt=ssm_log_dt,
             ssm_b_re=ssm_b_re, ssm_b_im=ssm_b_im, ssm_c_re=ssm_c_re, ssm_c_im=ssm_c_im, ssm_d=ssm_d,
             ssm_w_glu=ssm_w_glu, q_norm=q_norm, k_norm_cmp=k_norm_cmp, k_norm_slc=k_norm_slc,
             k_norm_win=k_norm_win, cmp_w1_k=cmp_w1_k, cmp_b1_k=cmp_b1_k, cmp_w2_k=cmp_w2_k,
             cmp_w1_v=cmp_w1_v, cmp_b1_v=cmp_b1_v, cmp_w2_v=cmp_w2_v, norm_ssm_out=norm_ssm_out,
             norm_attn_out=norm_attn_out, w_out=w_out, norm_ffn=norm_ffn, w_ffn_gate=w_ffn_gate,
             w_ffn_up=w_ffn_up, w_ffn_down=w_ffn_down)
    big = {name: w[name].astype(BF16) for name in BIG_WEIGHTS}
    y_p, y_s = x_prompt, x_sample
    outs_p, outs_s = [], []
    for l in range(DEPTH):
        prm = _layer_params(l, w, big)
        y_p, *rest = _prompt_layer(y_p, prm)
        outs_p.append(rest)
        y_s, *rest = _sample_layer(y_s, prm, cache_cmp_kv, cache_slc_kv, l, page_table,
                                   state_win_kv[l], state_ssm[l])
        outs_s.append(rest)
    stack = lambda outs, i: jnp.stack([o[i] for o in outs])
    return (y_p, y_s, stack(outs_p, 0), stack(outs_p, 1), stack(outs_p, 2), stack(outs_p, 3),
            stack(outs_s, 0), stack(outs_s, 1), stack(outs_s, 2), stack(outs_s, 3))
```
